```python
import jax, jax.numpy as jnp
from jax import lax
import numpy as np

D_MODEL = 2048
BATCH = 1
SEQ = 16384
DEPTH = 1
DEC_BATCH = 32
DEC_SEQ = 4
PAST_LEN = 16384
PAGE_SIZE = 128

POOL_WIDTH = D_MODEL // 2
POOL_WINDOWS = (2, 4, 8, 16)
POOL_GROUP = POOL_WIDTH // len(POOL_WINDOWS)
POOL_STATE = max(POOL_WINDOWS) - 1
HEAD_DIM = 128
N_HEADS = (D_MODEL - POOL_WIDTH) // HEAD_DIM
ATTN_WIDTH = N_HEADS * HEAD_DIM
DILATIONS = ((128, 1), (512, 4), (2048, 16))
MAX_WINDOW = 2048
Q_BLOCK = 128
ROPE_THETA = 10000.0
N_MEM = 256
MEM_HEADS = 4
MEM_HEAD_DIM = 128
MEM_WIDTH = MEM_HEADS * MEM_HEAD_DIM
D_FF = 5632
CONV_WIDTH = 3
EPS = 1e-6
NEG_INF = -1e30
IN_WIDTH = POOL_WIDTH + 3 * ATTN_WIDTH

kernel_name = "hybrid_pool_dilated_attn_decoder_step"


def _rms_norm(x, gain):
    xf = x.astype(jnp.float32)
    y = xf * lax.rsqrt(jnp.mean(xf * xf, axis=-1, keepdims=True) + EPS)
    return (y * gain.astype(jnp.float32)).astype(x.dtype)


def _rope(x, pos):
    half = HEAD_DIM // 2
    inv = 1.0 / (ROPE_THETA ** (jnp.arange(half, dtype=jnp.float32) * (2.0 / HEAD_DIM)))
    ang = pos.astype(jnp.float32)[:, None] * inv[None, :]
    cos = jnp.cos(ang)[None, :, None, :]
    sin = jnp.sin(ang)[None, :, None, :]
    xf = x.astype(jnp.float32)
    x1, x2 = xf[..., :half], xf[..., half:]
    return jnp.concatenate([x1 * cos - x2 * sin, x1 * sin + x2 * cos], axis=-1).astype(x.dtype)


def _multi_scale_pool(u_ext, pos, w_pool, pool_scale):
    t = pos.shape[0]
    prefix = u_ext.shape[1] - t
    uf = u_ext.astype(jnp.float32)
    cs = jnp.concatenate([jnp.zeros_like(uf[:, :1]), jnp.cumsum(uf, axis=1)], axis=1)
    end = cs[:, prefix + 1:]
    u_t = uf[:, prefix:]
    outs = []
    for g, w in enumerate(POOL_WINDOWS):
        sl = slice(g * POOL_GROUP, (g + 1) * POOL_GROUP)
        start = cs[:, prefix + 1 - w: prefix + 1 - w + t, sl]
        cnt = jnp.minimum(pos + 1, w).astype(jnp.float32)[None, :, None]
        diff = (end[..., sl] - start) / cnt - u_t[..., sl]
        outs.append(jnp.einsum("btc,cd->btd", diff, w_pool[g].astype(jnp.float32)))
    y = jnp.concatenate(outs, axis=-1) * pool_scale.astype(jnp.float32)
    return y.astype(u_ext.dtype)


def _dilated_block(q, qpos, qidx, k_ext, v_ext):
    scale = HEAD_DIM ** -0.5
    outs, lses = [], []
    for window, dil in DILATIONS:
        dist = dil * jnp.arange(window // dil + 1)
        valid = (qpos[:, None] - dist[None, :]) >= 0
        idx = jnp.maximum(qidx[:, None] - dist[None, :], 0)
        kg = jnp.take(k_ext, idx, axis=1)
        vg = jnp.take(v_ext, idx, axis=1)
        s = jnp.einsum("bthd,btkhd->bthk", q, kg, preferred_element_type=jnp.float32) * scale
        s = jnp.where(valid[None, :, None, :], s, NEG_INF)
        m = jnp.max(s, axis=-1, keepdims=True)
        e = jnp.exp(s - m)
        den = jnp.sum(e, axis=-1)
        o = jnp.einsum("bthk,btkhd->bthd", e, vg.astype(jnp.float32)) / den[..., None]
        outs.append(o)
        lses.append(m[..., 0] + jnp.log(den))
    w = jax.nn.softmax(jnp.stack(lses, axis=0), axis=0)
    out = jnp.sum(w[..., None] * jnp.stack(outs, axis=0), axis=0)
    return out.astype(q.dtype)


def _dilated_attention(q, pos, k_ext, v_ext):
    t = q.shape[1]
    prefix = k_ext.shape[1] - t
    qidx = prefix + jnp.arange(t)
    if t % Q_BLOCK == 0 and t > Q_BLOCK:
        def blk(b):
            s0 = b * Q_BLOCK
            return _dilated_block(lax.dynamic_slice_in_dim(q, s0, Q_BLOCK, 1),
                                  lax.dynamic_slice_in_dim(pos, s0, Q_BLOCK, 0),
                                  lax.dynamic_slice_in_dim(qidx, s0, Q_BLOCK, 0),
                                  k_ext, v_ext)
        o = lax.map(blk, jnp.arange(t // Q_BLOCK))
        return jnp.moveaxis(o, 0, 1).reshape(q.shape)
    return _dilated_block(q, pos, qidx, k_ext, v_ext)


def _memory_kv(mem, norm_src, w_k, w_v, k_norm):
    b, n = mem.shape[:2]
    m = _rms_norm(mem, norm_src)
    k = _rms_norm((m @ w_k).reshape(b, n, MEM_HEADS, MEM_HEAD_DIM), k_norm)
    v = (m @ w_v).reshape(b, n, MEM_HEADS, MEM_HEAD_DIM)
    return k, v


def _memory_attention(h, mem_k, mem_v, w_q, q_norm, w_o):
    b, t = h.shape[:2]
    q = _rms_norm((h @ w_q).reshape(b, t, MEM_HEADS, MEM_HEAD_DIM), q_norm)
    s = jnp.einsum("bthd,bmhd->bhtm", q, mem_k, preferred_element_type=jnp.float32) * (MEM_HEAD_DIM ** -0.5)
    p = jax.nn.softmax(s, axis=-1)
    o = jnp.einsum("bhtm,bmhd->bthd", p, mem_v.astype(jnp.float32))
    return o.reshape(b, t, MEM_WIDTH).astype(h.dtype) @ w_o


def _conv_ffn(h, conv_prev, w_gate, w_up, conv_w, conv_b, w_down):
    t = h.shape[1]
    g = h @ w_gate
    up = h @ w_up
    g_ext = jnp.concatenate([conv_prev, g], axis=1)
    c = conv_b + sum(g_ext[:, j:j + t] * conv_w[j] for j in range(CONV_WIDTH))
    y = (jax.nn.silu(c) * up) @ w_down
    return y, g_ext[:, -(CONV_WIDTH - 1):]


def _layer(x, pos, pool_prev, k_prev, v_prev, mem_k, mem_v, conv_prev, p):
    b, t = x.shape[:2]
    h = _rms_norm(x, p["norm_mix"])
    proj = h @ p["w_in"]
    u = proj[..., :POOL_WIDTH]
    q = proj[..., POOL_WIDTH:POOL_WIDTH + ATTN_WIDTH].reshape(b, t, N_HEADS, HEAD_DIM)
    k = proj[..., POOL_WIDTH + ATTN_WIDTH:POOL_WIDTH + 2 * ATTN_WIDTH].reshape(b, t, N_HEADS, HEAD_DIM)
    v = proj[..., POOL_WIDTH + 2 * ATTN_WIDTH:].reshape(b, t, N_HEADS, HEAD_DIM)
    q = _rope(_rms_norm(q, p["q_norm"]), pos)
    k = _rope(_rms_norm(k, p["k_norm"]), pos)
    u_ext = jnp.concatenate([pool_prev, u], axis=1)
    pool_out = _multi_scale_pool(u_ext, pos, p["w_pool"], p["pool_scale"])
    attn_out = _dilated_attention(q, pos, jnp.concatenate([k_prev, k], axis=1),
                                  jnp.concatenate([v_prev, v], axis=1))
    mixed = jnp.concatenate([pool_out, attn_out.reshape(b, t, ATTN_WIDTH)], axis=-1)
    x = x + mixed @ p["w_out"]
    x = x + _memory_attention(_rms_norm(x, p["norm_mem"]), mem_k, mem_v,
                              p["w_mem_q"], p["mem_q_norm"], p["w_mem_o"])
    f, conv_state = _conv_ffn(_rms_norm(x, p["norm_ffn"]), conv_prev, p["w_gate"], p["w_up"],
                              p["conv_w"], p["conv_b"], p["w_down"])
    x = x + f
    return x, u_ext[:, -POOL_STATE:], k, v, conv_state


def setup_inputs(seed: int = 0) -> dict:
    key = jax.random.key(seed)
    ks = iter(jax.random.split(key, 40))
    f32 = jnp.float32

    def nrm(shape, scale):
        return jax.random.normal(next(ks), shape, f32) * scale

    def gain(shape):
        return 1.0 + nrm(shape, 0.02)

    win_buf = min(MAX_WINDOW, PAST_LEN)
    return {
        "x_prompt": nrm((BATCH, SEQ, D_MODEL), 1.0),
        "x_sample": nrm((DEC_BATCH, DEC_SEQ, D_MODEL), 1.0),
        "state_pool": nrm((DEPTH, DEC_BATCH, POOL_STATE, POOL_WIDTH), 1.0),
        "cache_win_k": nrm((DEPTH, DEC_BATCH, win_buf, N_HEADS, HEAD_DIM), 1.0),
        "cache_win_v": nrm((DEPTH, DEC_BATCH, win_buf, N_HEADS, HEAD_DIM), 1.0),
        "cache_mem_k": nrm((DEPTH, DEC_BATCH, N_MEM, MEM_HEADS, MEM_HEAD_DIM), 1.0),
        "cache_mem_v": nrm((DEPTH, DEC_BATCH, N_MEM, MEM_HEADS, MEM_HEAD_DIM), 1.0),
        "state_conv": nrm((DEPTH, DEC_BATCH, CONV_WIDTH - 1, D_FF), 1.0),
        "mem_prompt": nrm((BATCH, N_MEM, D_MODEL), 1.0),
        "norm_mix": gain((DEPTH, D_MODEL)),
        "w_in": nrm((DEPTH, D_MODEL, IN_WIDTH), D_MODEL ** -0.5),
        "q_norm": gain((DEPTH, HEAD_DIM)),
        "k_norm": gain((DEPTH, HEAD_DIM)),
        "w_pool": nrm((DEPTH, len(POOL_WINDOWS), POOL_GROUP, POOL_GROUP), POOL_GROUP ** -0.5),
        "pool_scale": gain((DEPTH, POOL_WIDTH)),
        "w_out": nrm((DEPTH, D_MODEL, D_MODEL), D_MODEL ** -0.5),
        "norm_mem": gain((DEPTH, D_MODEL)),
        "norm_mem_src": gain((DEPTH, D_MODEL)),
        "w_mem_q": nrm((DEPTH, D_MODEL, MEM_WIDTH), D_MODEL ** -0.5),
        "w_mem_k": nrm((DEPTH, D_MODEL, MEM_WIDTH), D_MODEL ** -0.5),
        "w_mem_v": nrm((DEPTH, D_MODEL, MEM_WIDTH), D_MODEL ** -0.5),
        "mem_q_norm": gain((DEPTH, MEM_HEAD_DIM)),
        "mem_k_norm": gain((DEPTH, MEM_HEAD_DIM)),
        "w_mem_o": nrm((DEPTH, MEM_WIDTH, D_MODEL), MEM_WIDTH ** -0.5),
        "norm_ffn": gain((DEPTH, D_MODEL)),
        "w_gate": nrm((DEPTH, D_MODEL, D_FF), D_MODEL ** -0.5),
        "w_up": nrm((DEPTH, D_MODEL, D_FF), D_MODEL ** -0.5),
        "conv_w": nrm((DEPTH, CONV_WIDTH, D_FF), CONV_WIDTH ** -0.5),
        "conv_b": nrm((DEPTH, D_FF), 0.02),
        "w_down": nrm((DEPTH, D_FF, D_MODEL), D_FF ** -0.5),
    }


def reference(x_prompt, x_sample, state_pool, cache_win_k, cache_win_v, cache_mem_k, cache_mem_v,
              state_conv, mem_prompt, norm_mix, w_in, q_norm, k_norm, w_pool, pool_scale, w_out,
              norm_mem, norm_mem_src, w_mem_q, w_mem_k, w_mem_v, mem_q_norm, mem_k_norm, w_mem_o,
              norm_ffn, w_gate, w_up, conv_w, conv_b, w_down):
    bp, tp = x_prompt.shape[:2]
    ts = x_sample.shape[1]
    pos_p = jnp.arange(tp)
    pos_s = PAST_LEN + jnp.arange(ts)
    keep_p = min(MAX_WINDOW, tp)
    dt = x_prompt.dtype
    xp, xs = x_prompt, x_sample
    p_pool, p_k, p_v, p_mk, p_mv, p_conv = [], [], [], [], [], []
    s_pool, s_k, s_v, s_conv = [], [], [], []
    for l in range(DEPTH):
        prm = {
            "norm_mix": norm_mix[l], "w_in": w_in[l], "q_norm": q_norm[l], "k_norm": k_norm[l],
            "w_pool": w_pool[l], "pool_scale": pool_scale[l], "w_out": w_out[l],
            "norm_mem": norm_mem[l], "w_mem_q": w_mem_q[l], "mem_q_norm": mem_q_norm[l],
            "w_mem_o": w_mem_o[l], "norm_ffn": norm_ffn[l], "w_gate": w_gate[l], "w_up": w_up[l],
            "conv_w": conv_w[l], "conv_b": conv_b[l], "w_down": w_down[l],
        }
        mk, mv = _memory_kv(mem_prompt, norm_mem_src[l], w_mem_k[l], w_mem_v[l], mem_k_norm[l])
        xp, pool_st, kp, vp, conv_st = _layer(
            xp, pos_p,
            jnp.zeros((bp, POOL_STATE, POOL_WIDTH), dt),
            jnp.zeros((bp, MAX_WINDOW, N_HEADS, HEAD_DIM), dt),
            jnp.zeros((bp, MAX_WINDOW, N_HEADS, HEAD_DIM), dt),
            mk, mv,
            jnp.zeros((bp, CONV_WIDTH - 1, D_FF), dt), prm)
        p_pool.append(pool_st); p_k.append(kp[:, -keep_p:]); p_v.append(vp[:, -keep_p:])
        p_mk.append(mk); p_mv.append(mv); p_conv.append(conv_st)
        xs, pool_st_s, ks_new, vs_new, conv_st_s = _layer(
            xs, pos_s, state_pool[l], cache_win_k[l], cache_win_v[l],
            cache_mem_k[l], cache_mem_v[l], state_conv[l], prm)
        s_pool.append(pool_st_s); s_k.append(ks_new); s_v.append(vs_new); s_conv.append(conv_st_s)
    return (xp, xs,
            jnp.stack(p_pool), jnp.stack(p_k), jnp.stack(p_v), jnp.stack(p_mk), jnp.stack(p_mv), jnp.stack(p_conv),
            jnp.stack(s_pool), jnp.stack(s_k), jnp.stack(s_v), jnp.stack(s_conv))
```

```python
import functools
import math

import jax
import jax.numpy as jnp
from jax import lax
from jax.experimental import pallas as pl
from jax.experimental.pallas import tpu as pltpu

PAST_LEN = 16384
POOL_WINDOWS = (2, 4, 8, 16)
HEAD_DIM = 128
DILATIONS = ((128, 1), (512, 4), (2048, 16))
MAX_WINDOW = 2048
ROPE_THETA = 10000.0
MEM_HEAD_DIM = 128
CONV_WIDTH = 3
EPS = 1e-6
NEG_INF = -1e30

POOL_STATE = max(POOL_WINDOWS) - 1
POOL_TAIL = max(POOL_WINDOWS)
CONV_TAIL = CONV_WIDTH - 1
BAND = DILATIONS[0][0]
SUPER_BLOCK = MAX_WINDOW

V7X_SCOPED_VMEM_LIMIT = 60000 * 1024
LANES = 128
SUBLANES = 8

F32 = jnp.float32
BF16 = jnp.bfloat16


def _const_spec(shape):
    return pl.BlockSpec(shape, lambda *_: (0,) * len(shape), pipeline_mode=pl.Buffered(1))


def _params(n_axes, vmem_bytes):
    return pltpu.CompilerParams(
        dimension_semantics=("arbitrary",) * n_axes,
        vmem_limit_bytes=min(int(vmem_bytes), V7X_SCOPED_VMEM_LIMIT),
    )


def _rms(x, gain):
    return x * lax.rsqrt(jnp.mean(x * x, axis=-1, keepdims=True) + EPS) * gain


def _dot(a, b):
    return jnp.dot(a, b, preferred_element_type=F32)


def _dot_nt(a, b):
    return lax.dot_general(a, b, (((1,), (1,)), ((), ())), preferred_element_type=F32)


def _in_proj_kernel(x_ref, cos_ref, sin_ref, prev_ref, norm_ref, win_ref, qn_ref, kn_ref, wpool_ref, pscale_ref,
                    pool_ref, q_ref, k_ref, v_ref, kkeep_ref, vkeep_ref, pstate_ref, u_scr,
                    *, tm, nb, pos0, n_tiles, n_heads):
    i = pl.program_id(0)
    tail = POOL_TAIL * nb
    pool_width = pool_ref.shape[1]
    attn_width = n_heads * HEAD_DIM
    group = pool_width // len(POOL_WINDOWS)

    h = _rms(x_ref[...], norm_ref[...]).astype(BF16)

    @pl.when(i == 0)
    def _():
        u_scr[0:tail, :] = prev_ref[...]

    u_scr[tail:tail + tm, :] = _dot(h, win_ref[:, 0:pool_width])
    row = i * tm + lax.broadcasted_iota(jnp.int32, (tm, 1), 0)
    pos = pos0 + (row >> int(math.log2(nb)))
    for g, w in enumerate(POOL_WINDOWS):
        cols = slice(g * group, (g + 1) * group)
        u_t = u_scr[tail:tail + tm, cols]
        acc = u_t
        for j in range(1, w):
            acc = acc + u_scr[tail - j * nb:tail - j * nb + tm, cols]
        cnt = jnp.minimum(pos + 1, w).astype(F32)
        diff = acc / cnt - u_t
        y = _dot(diff.astype(BF16), wpool_ref[g]) * pscale_ref[:, cols]
        pool_ref[:, cols] = y.astype(pool_ref.dtype)

    @pl.when(i == n_tiles - 1)
    def _():
        pstate_ref[...] = u_scr[tm + nb:tm + tail, :]

    if n_tiles > 1:
        u_scr[0:tail, :] = u_scr[tm:tm + tail, :]

    cosv = cos_ref[...]
    sinv = sin_ref[...]

    def normed_rope(col0, gain_ref, hm_ref, keep_ref):
        proj = _dot(h, win_ref[:, col0:col0 + attn_width])
        for hh in range(n_heads):
            cols = slice(hh * HEAD_DIM, (hh + 1) * HEAD_DIM)
            xn = _rms(proj[:, cols], gain_ref[...])
            xr = xn * cosv + pltpu.roll(xn, HEAD_DIM // 2, 1) * sinv
            hm_ref[hh] = xr.astype(hm_ref.dtype)
            if keep_ref is not None:
                keep_ref[:, cols] = xr

    normed_rope(pool_width, qn_ref, q_ref, None)
    normed_rope(pool_width + attn_width, kn_ref, k_ref, kkeep_ref)
    vproj = _dot(h, win_ref[:, pool_width + 2 * attn_width:pool_width + 3 * attn_width])
    vkeep_ref[...] = vproj
    for hh in range(n_heads):
        v_ref[hh] = vproj[:, hh * HEAD_DIM:(hh + 1) * HEAD_DIM].astype(v_ref.dtype)


def _in_proj(x, cos2, sin2, pool_prev, norm, w_in, q_norm, k_norm, w_pool, pool_scale, *, tm, nb, pos0, keep):
    rows, d_model = x.shape
    pool_width = pool_scale.shape[-1]
    attn_width = (w_in.shape[1] - pool_width) // 3
    n_heads = attn_width // HEAD_DIM
    n_tiles = rows // tm
    assert rows % tm == 0 and keep % tm == 0 and nb & (nb - 1) == 0 and (tm >= POOL_TAIL * nb or n_tiles == 1)
    first_keep = n_tiles - keep // tm
    tile = lambda width: pl.BlockSpec((tm, width), lambda i: (i, 0))
    heads = pl.BlockSpec((n_heads, tm, HEAD_DIM), lambda i: (0, i, 0))
    keep_spec = pl.BlockSpec((tm, attn_width), lambda i: (jnp.maximum(i - first_keep, 0), 0))
    vmem = (2 * tm * d_model * 4 + w_in.size * 2 + 2 * 2 * tm * LANES * 4 + 2 * tm * pool_width * 2
            + 3 * 2 * tm * attn_width * 2 + 2 * 2 * tm * attn_width * 4 + (POOL_TAIL * nb + tm) * pool_width * 4
            + 2 * POOL_TAIL * nb * pool_width * 4 + 6 * tm * attn_width * 4 + tm * d_model * 6 + (4 << 20))
    return pl.pallas_call(
        functools.partial(_in_proj_kernel, tm=tm, nb=nb, pos0=pos0, n_tiles=n_tiles, n_heads=n_heads),
        grid=(n_tiles,),
        in_specs=[tile(d_model), tile(HEAD_DIM), tile(HEAD_DIM), _const_spec(pool_prev.shape),
                  _const_spec(norm.shape), _const_spec(w_in.shape), _const_spec(q_norm.shape),
                  _const_spec(k_norm.shape), _const_spec(w_pool.shape), _const_spec(pool_scale.shape)],
        out_specs=[tile(pool_width), heads, heads, heads, keep_spec, keep_spec,
                   pl.BlockSpec((POOL_STATE * nb, pool_width), lambda i: (0, 0))],
        out_shape=[jax.ShapeDtypeStruct((rows, pool_width), BF16),
                   jax.ShapeDtypeStruct((n_heads, rows, HEAD_DIM), BF16),
                   jax.ShapeDtypeStruct((n_heads, rows, HEAD_DIM), BF16),
                   jax.ShapeDtypeStruct((n_heads, rows, HEAD_DIM), BF16),
                   jax.ShapeDtypeStruct((keep, attn_width), F32),
                   jax.ShapeDtypeStruct((keep, attn_width), F32),
                   jax.ShapeDtypeStruct((POOL_STATE * nb, pool_width), F32)],
        scratch_shapes=[pltpu.VMEM((POOL_TAIL * nb + tm, pool_width), F32)],
        compiler_params=_params(1, vmem),
        name="in_proj",
    )(x, cos2, sin2, pool_prev, norm, w_in, q_norm, k_norm, w_pool, pool_scale)


def _dilated_attn_kernel(q_ref, kp_ref, kc_ref, vp_ref, vc_ref, o_ref,
                         qf, kf, vf, k1, v1, q4, k4, v4, q16, k16, v16,
                         o1, o2, o3, m1, m2, m3, l1, l2, l3, bias):
    i = pl.program_id(0)
    sb = q_ref.shape[0]
    scale = HEAD_DIM ** -0.5

    k1[0:sb, :] = kp_ref[...]
    k1[sb:2 * sb, :] = kc_ref[...]
    v1[0:sb, :] = vp_ref[...]
    v1[sb:2 * sb, :] = vc_ref[...]
    qf[...] = q_ref[...].astype(F32)
    kf[...] = k1[...].astype(F32)
    vf[...] = v1[...].astype(F32)
    for d, qd, kd, vd in ((4, q4, k4, v4), (16, q16, k16, v16)):
        nq, nk = sb // d, 2 * sb // d
        for r in range(d):
            qd[r * nq:(r + 1) * nq, :] = qf[pl.ds(r, nq, stride=d), :].astype(BF16)
            kd[r * nk:(r + 1) * nk, :] = kf[pl.ds(r, nk, stride=d), :].astype(BF16)
            vd[r * nk:(r + 1) * nk, :] = vf[pl.ds(r, nk, stride=d), :].astype(BF16)

    qa = lax.broadcasted_iota(jnp.int32, (BAND, 2 * BAND), 0)
    kc = lax.broadcasted_iota(jnp.int32, (BAND, 2 * BAND), 1)
    in_band = (kc >= qa) & (kc <= qa + BAND)
    bias[0] = jnp.where(in_band, 0.0, NEG_INF)
    bias[1] = jnp.where(in_band & (kc >= BAND), 0.0, NEG_INF)

    def unit(q, k, v, before_start):
        s = _dot_nt(q, k) * scale + bias[before_start.astype(jnp.int32)]
        m = jnp.max(s, axis=-1, keepdims=True)
        p = jnp.exp(s - m)
        l = jnp.sum(p, axis=-1, keepdims=True)
        o = _dot(p.astype(BF16), v)
        return o, jnp.broadcast_to(m, o.shape), jnp.broadcast_to(l, o.shape)

    def branch(d, qd, kd, vd, od, md, ld):
        nq, nk = sb // d, 2 * sb // d
        blocks = nq // BAND

        def body(u, carry):
            r = u // blocks
            mb = u % blocks
            q0 = pl.multiple_of(r * nq + mb * BAND, BAND)
            k0 = pl.multiple_of(r * nk + nk // 2 + (mb - 1) * BAND, BAND)
            o, m, l = unit(qd[pl.ds(q0, BAND), :], kd[pl.ds(k0, 2 * BAND), :], vd[pl.ds(k0, 2 * BAND), :],
                           (i == 0) & (mb == 0))
            rows = pl.ds(r + d * BAND * mb, BAND, stride=d) if d > 1 else pl.ds(q0, BAND)
            od[rows, :] = o
            md[rows, :] = m
            ld[rows, :] = l
            return carry

        lax.fori_loop(0, d * blocks, body, 0, unroll=2)

    branch(1, q_ref, k1, v1, o1, m1, l1)
    branch(4, q4, k4, v4, o2, m2, l2)
    branch(16, q16, k16, v16, o3, m3, l3)

    def merge(c, carry):
        rows = pl.ds(pl.multiple_of(c * BAND, BAND), BAND)
        ma, mb_, mc = m1[rows, :], m2[rows, :], m3[rows, :]
        m = jnp.maximum(jnp.maximum(ma, mb_), mc)
        wa, wb, wc = jnp.exp(ma - m), jnp.exp(mb_ - m), jnp.exp(mc - m)
        num = wa * o1[rows, :] + wb * o2[rows, :] + wc * o3[rows, :]
        den = wa * l1[rows, :] + wb * l2[rows, :] + wc * l3[rows, :]
        o_ref[rows, :] = (num / den).astype(o_ref.dtype)
        return carry

    lax.fori_loop(0, sb // BAND, merge, 0)


def _dilated_attn(q_hm, k_hm, v_hm):
    n_heads, rows, _ = q_hm.shape
    sb = SUPER_BLOCK
    assert rows % sb == 0 and [d for _, d in DILATIONS] == [1, 4, 16] and all(w == BAND * d for w, d in DILATIONS)
    cur = pl.BlockSpec((None, sb, HEAD_DIM), lambda i, h: (h, i, 0))
    prev = pl.BlockSpec((None, sb, HEAD_DIM), lambda i, h: (h, jnp.maximum(i - 1, 0), 0))
    blk = lambda n, dt: pltpu.VMEM((n, HEAD_DIM), dt)
    scratch = ([blk(sb, F32), blk(2 * sb, F32), blk(2 * sb, F32), blk(2 * sb, BF16), blk(2 * sb, BF16)]
               + [blk(sb, BF16), blk(2 * sb, BF16), blk(2 * sb, BF16)] * 2
               + [blk(sb, F32)] * 9
               + [pltpu.VMEM((2, BAND, 2 * BAND), F32)])
    vmem = 2 * 6 * sb * HEAD_DIM * 2 + (5 * 4 + 2 * 2 + 2 * 5 * 2 + 9 * 4) * sb * HEAD_DIM + (8 << 20)
    return pl.pallas_call(
        _dilated_attn_kernel,
        grid=(rows // sb, n_heads),
        in_specs=[cur, prev, cur, prev, cur],
        out_specs=pl.BlockSpec((sb, HEAD_DIM), lambda i, h: (i, h)),
        out_shape=jax.ShapeDtypeStruct((rows, n_heads * HEAD_DIM), BF16),
        scratch_shapes=scratch,
        compiler_params=_params(2, vmem),
        name="dilated_attn",
    )(q_hm, k_hm, k_hm, v_hm, v_hm)


def _sample_attn_kernel(q_ref, kn_ref, vn_ref, ck_ref, cv_ref, o_ref, *, n_heads, past_len):
    tq = q_ref.shape[0]
    cache = ck_ref.shape[0]
    scale = HEAD_DIM ** -0.5

    def multiplicity(n_keys, key0):
        t = lax.broadcasted_iota(jnp.int32, (tq, n_keys), 0)
        key = key0 + lax.broadcasted_iota(jnp.int32, (tq, n_keys), 1)
        dist = cache + t - key
        seen = (dist >= 0) & (past_len + t - dist >= 0)
        n = jnp.zeros((tq, n_keys), F32)
        for window, dil in DILATIONS:
            hit = seen & (dist <= window) & ((dist & (dil - 1)) == 0)
            n = n + jnp.where(hit, 1.0, 0.0)
        return n

    n_c = multiplicity(cache, 0)
    n_n = multiplicity(tq, cache)
    for hh in range(n_heads):
        cols = slice(hh * HEAD_DIM, (hh + 1) * HEAD_DIM)
        q = q_ref[:, cols]
        s_c = jnp.where(n_c > 0, _dot_nt(q, ck_ref[:, cols].astype(BF16)) * scale, NEG_INF)
        s_n = jnp.where(n_n > 0, _dot_nt(q, kn_ref[:, cols].astype(BF16)) * scale, NEG_INF)
        m = jnp.maximum(jnp.max(s_c, axis=-1, keepdims=True), jnp.max(s_n, axis=-1, keepdims=True))
        p_c = n_c * jnp.exp(s_c - m)
        p_n = n_n * jnp.exp(s_n - m)
        den = jnp.sum(p_c, axis=-1, keepdims=True) + jnp.sum(p_n, axis=-1, keepdims=True)
        num = _dot(p_c.astype(BF16), cv_ref[:, cols].astype(BF16)) + _dot(p_n.astype(BF16), vn_ref[:, cols].astype(BF16))
        o_ref[:, cols] = (num / den).astype(o_ref.dtype)


def _sample_attn(q, k_new, v_new, cache_k, cache_v, *, past_len):
    batch, tq, width = q.shape
    cache = cache_k.shape[1]
    assert all(d & (d - 1) == 0 for _, d in DILATIONS)
    new = pl.BlockSpec((None, tq, width), lambda b: (b, 0, 0))
    old = pl.BlockSpec((None, cache, width), lambda b: (b, 0, 0))
    vmem = 2 * 2 * cache * width * 4 + 4 * cache * HEAD_DIM * 4 + (8 << 20)
    return pl.pallas_call(
        functools.partial(_sample_attn_kernel, n_heads=width // HEAD_DIM, past_len=past_len),
        grid=(batch,),
        in_specs=[new, new, new, old, old],
        out_specs=new,
        out_shape=jax.ShapeDtypeStruct((batch, tq, width), BF16),
        compiler_params=_params(1, vmem),
        name="sample_attn",
    )(q, k_new, v_new, cache_k, cache_v)


def _memory_kv_kernel(mem_ref, norm_ref, wk_ref, wv_ref, kn_ref, k_ref, v_ref):
    m = _rms(mem_ref[...], norm_ref[...]).astype(BF16)
    k = _dot(m, wk_ref[...])
    for hh in range(k.shape[1] // MEM_HEAD_DIM):
        cols = slice(hh * MEM_HEAD_DIM, (hh + 1) * MEM_HEAD_DIM)
        k_ref[:, cols] = _rms(k[:, cols], kn_ref[...])
    v_ref[...] = _dot(m, wv_ref[...])


def _memory_kv(mem, norm_src, w_k, w_v, k_norm):
    n_mem = mem.shape[0]
    width = w_k.shape[1]
    args = (mem, norm_src, w_k, w_v, k_norm)
    out = jax.ShapeDtypeStruct((n_mem, width), F32)
    return pl.pallas_call(
        _memory_kv_kernel,
        grid=(1,),
        in_specs=[_const_spec(a.shape) for a in args],
        out_specs=[pl.BlockSpec((n_mem, width), lambda i: (0, 0))] * 2,
        out_shape=[out, out],
        compiler_params=_params(1, 32 << 20),
        name="memory_kv",
    )(*args)


def _mem_mix_kernel(x_ref, pool_ref, attn_ref, wout_ref, nmem_ref, wq_ref, qn_ref, mk_ref, mv_ref, wo_ref, nffn_ref,
                    x2_ref, h3_ref, *, nb, n_mem):
    tm = x_ref.shape[0]
    pool_width = pool_ref.shape[1]
    x1 = x_ref[...] + _dot(pool_ref[...], wout_ref[0:pool_width, :]) + _dot(attn_ref[...], wout_ref[pool_width:, :])
    h2 = _rms(x1, nmem_ref[...]).astype(BF16)
    qm = _dot(h2, wq_ref[...])
    scale = MEM_HEAD_DIM ** -0.5
    n_keys = mk_ref.shape[0]
    if nb > 1:
        seq_q = (pl.program_id(0) * tm + lax.broadcasted_iota(jnp.int32, (tm, n_keys), 0)) & (nb - 1)
        seq_k = lax.broadcasted_iota(jnp.int32, (tm, n_keys), 1) >> int(math.log2(n_mem))
        own = seq_q == seq_k
    heads = []
    for hh in range(qm.shape[1] // MEM_HEAD_DIM):
        cols = slice(hh * MEM_HEAD_DIM, (hh + 1) * MEM_HEAD_DIM)
        q = _rms(qm[:, cols], qn_ref[...]).astype(BF16)
        s = _dot_nt(q, mk_ref[:, cols].astype(BF16)) * scale
        if nb > 1:
            s = jnp.where(own, s, NEG_INF)
        p = jnp.exp(s - jnp.max(s, axis=-1, keepdims=True))
        o = _dot(p.astype(BF16), mv_ref[:, cols].astype(BF16)) / jnp.sum(p, axis=-1, keepdims=True)
        heads.append(o.astype(BF16))
    x2 = x1 + _dot(jnp.concatenate(heads, axis=-1), wo_ref[...])
    x2_ref[...] = x2
    h3_ref[...] = _rms(x2, nffn_ref[...]).astype(h3_ref.dtype)


def _mem_mix(x, pool_out, attn_out, w_out, norm_mem, w_q, q_norm, mem_k, mem_v, w_o, norm_ffn, *, tm, nb, n_mem):
    rows, d_model = x.shape
    assert rows % tm == 0 and nb & (nb - 1) == 0 and n_mem & (n_mem - 1) == 0 and mem_k.shape[0] == nb * n_mem
    tile = lambda a: pl.BlockSpec((tm, a.shape[1]), lambda i: (i, 0))
    consts = (w_out, norm_mem, w_q, q_norm, mem_k, mem_v, w_o, norm_ffn)
    vmem = (sum(c.size * c.dtype.itemsize for c in consts) + 2 * tm * d_model * (4 + 2 + 4 + 2)
            + tm * d_model * 16 + 4 * tm * mem_k.shape[0] * 4 + (6 << 20))
    return pl.pallas_call(
        functools.partial(_mem_mix_kernel, nb=nb, n_mem=n_mem),
        grid=(rows // tm,),
        in_specs=[tile(x), tile(pool_out), tile(attn_out)] + [_const_spec(c.shape) for c in consts],
        out_specs=[pl.BlockSpec((tm, d_model), lambda i: (i, 0))] * 2,
        out_shape=[jax.ShapeDtypeStruct((rows, d_model), F32), jax.ShapeDtypeStruct((rows, d_model), BF16)],
        compiler_params=_params(1, vmem),
        name="mem_mix",
    )(x, pool_out, attn_out, *consts)


def _conv_ffn_kernel(h_ref, x_ref, wg_ref, wu_ref, cw_ref, cb_ref, wd_ref, prev_ref, y_ref, state_ref,
                     g_ext, g_scr, *, nb):
    i = pl.program_id(0)
    j = pl.program_id(1)
    tm = h_ref.shape[0]
    tail = CONV_TAIL * nb
    h = h_ref[...]

    @pl.when(i == 0)
    def _():
        g_ext[0:tail, :] = prev_ref[...]

    @pl.when(i > 0)
    def _():
        g_ext[0:tail, :] = g_scr[j]

    g_ext[tail:tail + tm, :] = _dot(h, wg_ref[...])
    up = _dot(h, wu_ref[...])
    c = cb_ref[...]
    for tap in range(CONV_WIDTH):
        c = c + g_ext[tap * nb:tap * nb + tm, :] * cw_ref[tap:tap + 1, :]
    act = (c * jax.nn.sigmoid(c) * up).astype(BF16)
    y = _dot(act, wd_ref[...])

    @pl.when(j == 0)
    def _():
        y_ref[...] = x_ref[...] + y

    @pl.when(j > 0)
    def _():
        y_ref[...] += y

    last = g_ext[tm:tm + tail, :]
    g_scr[j] = last
    state_ref[:, pl.ds(pl.multiple_of(j * last.shape[1], LANES), last.shape[1])] = last


def _conv_ffn(h3, x2, w_gate, w_up, conv_w, conv_b, w_down, conv_prev, *, tm, tf, nb):
    rows, d_model = x2.shape
    d_ff = w_gate.shape[1]
    assert rows % tm == 0 and d_ff % tf == 0 and tm >= CONV_TAIL * nb
    tail = CONV_TAIL * nb
    row_tile = pl.BlockSpec((tm, d_model), lambda i, j: (i, 0))
    chunk = lambda n: pl.BlockSpec((n, tf), lambda i, j: (0, j))
    vmem = (2 * tm * d_model * (2 + 4 + 4) + 2 * 3 * d_model * tf * 2 + (tm + 3 * tail) * tf * 4
            + d_ff // tf * tail * tf * 4 + 5 * tm * tf * 4 + tm * d_model * 4 + (6 << 20))
    return pl.pallas_call(
        functools.partial(_conv_ffn_kernel, nb=nb),
        grid=(rows // tm, d_ff // tf),
        in_specs=[row_tile, row_tile, chunk(d_model), chunk(d_model), chunk(CONV_WIDTH), chunk(1),
                  pl.BlockSpec((tf, d_model), lambda i, j: (j, 0)), chunk(tail)],
        out_specs=[row_tile, pl.BlockSpec((tail, d_ff), lambda i, j: (0, 0))],
        out_shape=[jax.ShapeDtypeStruct((rows, d_model), F32), jax.ShapeDtypeStruct((tail, d_ff), F32)],
        scratch_shapes=[pltpu.VMEM((tail + tm, tf), F32), pltpu.VMEM((d_ff // tf, tail, tf), F32)],
        compiler_params=_params(2, vmem),
        name="conv_ffn",
    )(h3, x2, w_gate, w_up, conv_w, conv_b, w_down, conv_prev)


def _rope_tables(pos):
    half = HEAD_DIM // 2
    inv = 1.0 / (ROPE_THETA ** (jnp.arange(half, dtype=F32) * (2.0 / HEAD_DIM)))
    ang = pos.astype(F32)[:, None] * inv[None, :]
    cos, sin = jnp.cos(ang), jnp.sin(ang)
    return jnp.concatenate([cos, cos], axis=-1), jnp.concatenate([-sin, sin], axis=-1)


def _row_tile(rows, want):
    return want if rows % want == 0 else rows


def _layer(x, pos0, nb, pool_prev, conv_prev, mem_k, mem_v, n_mem, attend, p, keep):
    rows = x.shape[0]
    cos2, sin2 = _rope_tables(pos0 + jnp.arange(rows) // nb)
    pool_out, q_hm, k_hm, v_hm, k_keep, v_keep, pool_state = _in_proj(
        x, cos2, sin2, pool_prev, p["norm_mix"], p["w_in"], p["q_norm"], p["k_norm"], p["w_pool"], p["pool_scale"],
        tm=_row_tile(rows, 256), nb=nb, pos0=pos0, keep=keep)
    attn_out = attend(q_hm, k_hm, v_hm, k_keep, v_keep)
    x2, h3 = _mem_mix(x, pool_out, attn_out, p["w_out"], p["norm_mem"], p["w_mem_q"], p["mem_q_norm"], mem_k, mem_v,
                      p["w_mem_o"], p["norm_ffn"], tm=_row_tile(rows, 256), nb=nb, n_mem=n_mem)
    y, conv_state = _conv_ffn(h3, x2, p["w_gate"], p["w_up"], p["conv_w"], p["conv_b"], p["w_down"], conv_prev,
                              tm=_row_tile(rows, 512), tf=512, nb=nb)
    return y, pool_state, k_keep, v_keep, conv_state


def kernel(x_prompt, x_sample, state_pool, cache_win_k, cache_win_v, cache_mem_k, cache_mem_v, state_conv, mem_prompt, norm_mix, w_in, q_norm, k_norm, w_pool, pool_scale, w_out, norm_mem, norm_mem_src, w_mem_q, w_mem_k, w_mem_v, mem_q_norm, mem_k_norm, w_mem_o, norm_ffn, w_gate, w_up, conv_w, conv_b, w_down):
    bp, tp, d_model = x_prompt.shape
    bs, ts, _ = x_sample.shape
    depth = w_in.shape[0]
    assert bp == 1
    pool_width = pool_scale.shape[-1]
    d_ff = w_gate.shape[-1]
    n_mem = mem_prompt.shape[1]
    n_heads = cache_win_k.shape[3]
    mem_heads = cache_mem_k.shape[3]
    attn_width = n_heads * HEAD_DIM
    mem_width = mem_heads * MEM_HEAD_DIM
    keep_p = min(MAX_WINDOW, tp)
    row = lambda a: a.reshape(1, -1)

    xp = x_prompt.reshape(tp, d_model)
    xs = x_sample.transpose(1, 0, 2).reshape(ts * bs, d_model)
    outs = {k: [] for k in ("p_pool", "p_k", "p_v", "p_mk", "p_mv", "p_conv", "s_pool", "s_k", "s_v", "s_conv")}
    for l in range(depth):
        prm = {
            "norm_mix": row(norm_mix[l]), "w_in": w_in[l].astype(BF16), "q_norm": row(q_norm[l]),
            "k_norm": row(k_norm[l]), "w_pool": w_pool[l].astype(BF16), "pool_scale": row(pool_scale[l]),
            "w_out": w_out[l].astype(BF16), "norm_mem": row(norm_mem[l]), "w_mem_q": w_mem_q[l].astype(BF16),
            "mem_q_norm": row(mem_q_norm[l]), "w_mem_o": w_mem_o[l].astype(BF16), "norm_ffn": row(norm_ffn[l]),
            "w_gate": w_gate[l].astype(BF16), "w_up": w_up[l].astype(BF16), "conv_w": conv_w[l],
            "conv_b": row(conv_b[l]), "w_down": w_down[l].astype(BF16),
        }
        mk, mv = _memory_kv(mem_prompt[0], row(norm_mem_src[l]), w_mem_k[l].astype(BF16), w_mem_v[l].astype(BF16),
                            row(mem_k_norm[l]))
        xp, pool_st, kp, vp, conv_st = _layer(
            xp, 0, 1, jnp.zeros((POOL_TAIL, pool_width), F32), jnp.zeros((CONV_TAIL, d_ff), F32), mk, mv, n_mem,
            lambda q, k, v, kk, vk: _dilated_attn(q, k, v), prm, keep_p)
        outs["p_pool"].append(pool_st[None])
        outs["p_k"].append(kp.reshape(1, keep_p, n_heads, HEAD_DIM))
        outs["p_v"].append(vp.reshape(1, keep_p, n_heads, HEAD_DIM))
        outs["p_mk"].append(mk.reshape(1, n_mem, mem_heads, MEM_HEAD_DIM))
        outs["p_mv"].append(mv.reshape(1, n_mem, mem_heads, MEM_HEAD_DIM))
        outs["p_conv"].append(conv_st[None])

        pool_prev = jnp.pad(state_pool[l].transpose(1, 0, 2), ((POOL_TAIL - POOL_STATE, 0), (0, 0), (0, 0)))
        conv_prev = state_conv[l].transpose(1, 0, 2).reshape(CONV_TAIL * bs, d_ff)
        cache = cache_win_k.shape[2]

        def attend_samples(q_hm, k_hm, v_hm, k_new, v_new, l=l, cache=cache):
            pad = lambda a: jnp.pad(a, ((0, 0), (0, -ts % SUBLANES), (0, 0)))
            q = q_hm.reshape(n_heads, ts, bs, HEAD_DIM).transpose(2, 1, 0, 3).reshape(bs, ts, attn_width)
            to_seq = lambda a: a.reshape(ts, bs, attn_width).transpose(1, 0, 2)
            o = _sample_attn(pad(q), pad(to_seq(k_new)), pad(to_seq(v_new)),
                             cache_win_k[l].reshape(bs, cache, attn_width), cache_win_v[l].reshape(bs, cache, attn_width),
                             past_len=PAST_LEN)
            return o[:, :ts].transpose(1, 0, 2).reshape(ts * bs, attn_width)

        xs, pool_st, ks, vs, conv_st = _layer(
            xs, PAST_LEN, bs, pool_prev.reshape(POOL_TAIL * bs, pool_width), conv_prev,
            cache_mem_k[l].reshape(bs * n_mem, mem_width), cache_mem_v[l].reshape(bs * n_mem, mem_width), n_mem,
            attend_samples, prm, ts * bs)
        seq_major = lambda a, t: a.reshape(t, bs, -1).transpose(1, 0, 2)
        outs["s_pool"].append(seq_major(pool_st, POOL_STATE))
        outs["s_k"].append(seq_major(ks, ts).reshape(bs, ts, n_heads, HEAD_DIM))
        outs["s_v"].append(seq_major(vs, ts).reshape(bs, ts, n_heads, HEAD_DIM))
        outs["s_conv"].append(seq_major(conv_st, CONV_TAIL))
    stack = lambda k: jnp.stack(outs[k])
    return (xp.reshape(bp, tp, d_model), xs.reshape(ts, bs, d_model).transpose(1, 0, 2),
            stack("p_pool"), stack("p_k"), stack("p_v"), stack("p_mk"), stack("p_mv"), stack("p_conv"),
            stack("s_pool"), stack("s_k"), stack("s_v"), stack("s_conv"))
```

```python
import functools
import math

import jax
import jax.numpy as jnp
from jax import lax
from jax.experimental import pallas as pl
from jax.experimental.pallas import tpu as pltpu

PAST_LEN = 16384
POOL_WINDOWS = (2, 4, 8, 16)
HEAD_DIM = 128
DILATIONS = ((128, 1), (512, 4), (2048, 16))
MAX_WINDOW = 2048
ROPE_THETA = 10000.0
MEM_HEAD_DIM = 128
CONV_WIDTH = 3
EPS = 1e-6
NEG_INF = -1e30

POOL_STATE = max(POOL_WINDOWS) - 1
POOL_TAIL = max(POOL_WINDOWS)
CONV_TAIL = CONV_WIDTH - 1
BAND = DILATIONS[0][0]
SUPER_BLOCK = MAX_WINDOW

V7X_SCOPED_VMEM_LIMIT = 60000 * 1024
LANES = 128
SUBLANES = 8

IN_PROJ_ROWS = 256
MEM_MIX_ROWS = 256
FFN_ROWS = 512
FFN_COLS = 512

F32 = jnp.float32
BF16 = jnp.bfloat16


def _const_spec(shape):
    return pl.BlockSpec(shape, lambda *_: (0,) * len(shape), pipeline_mode=pl.Buffered(1))


def _params(n_axes, vmem_bytes):
    return pltpu.CompilerParams(
        dimension_semantics=("arbitrary",) * n_axes,
        vmem_limit_bytes=min(int(vmem_bytes), V7X_SCOPED_VMEM_LIMIT),
    )


def _rms(x, gain):
    return x * lax.rsqrt(jnp.mean(x * x, axis=-1, keepdims=True) + EPS) * gain


def _dot(a, b):
    return jnp.dot(a, b, preferred_element_type=F32)


def _dot_nt(a, b):
    return lax.dot_general(a, b, (((1,), (1,)), ((), ())), preferred_element_type=F32)


def _in_proj_kernel(x_ref, cos_ref, sin_ref, prev_ref, norm_ref, win_ref, qn_ref, kn_ref, wpool_ref, pscale_ref,
                    pool_ref, q_ref, k_ref, v_ref, kkeep_ref, vkeep_ref, pstate_ref, u_scr,
                    *, tm, nb, pos0, n_tiles, n_heads):
    i = pl.program_id(0)
    tail = POOL_TAIL * nb
    pool_width = pool_ref.shape[1]
    attn_width = n_heads * HEAD_DIM
    group = pool_width // len(POOL_WINDOWS)

    h = _rms(x_ref[...], norm_ref[...]).astype(BF16)

    @pl.when(i == 0)
    def _():
        u_scr[0:tail, :] = prev_ref[...]

    u_scr[tail:tail + tm, :] = _dot(h, win_ref[:, 0:pool_width])
    row = i * tm + lax.broadcasted_iota(jnp.int32, (tm, 1), 0)
    pos = pos0 + (row >> int(math.log2(nb)))
    for g, w in enumerate(POOL_WINDOWS):
        cols = slice(g * group, (g + 1) * group)
        u_t = u_scr[tail:tail + tm, cols]
        acc = u_t
        for j in range(1, w):
            acc = acc + u_scr[tail - j * nb:tail - j * nb + tm, cols]
        cnt = jnp.minimum(pos + 1, w).astype(F32)
        diff = acc / cnt - u_t
        y = _dot(diff.astype(BF16), wpool_ref[g]) * pscale_ref[:, cols]
        pool_ref[:, cols] = y.astype(pool_ref.dtype)

    @pl.when(i == n_tiles - 1)
    def _():
        pstate_ref[...] = u_scr[tm + nb:tm + tail, :]

    if n_tiles > 1:
        u_scr[0:tail, :] = u_scr[tm:tm + tail, :]

    cosv = cos_ref[...]
    sinv = sin_ref[...]

    def normed_rope(col0, gain_ref, hm_ref, keep_ref):
        proj = _dot(h, win_ref[:, col0:col0 + attn_width])
        for hh in range(n_heads):
            cols = slice(hh * HEAD_DIM, (hh + 1) * HEAD_DIM)
            xn = _rms(proj[:, cols], gain_ref[...])
            xr = xn * cosv + pltpu.roll(xn, HEAD_DIM // 2, 1) * sinv
            hm_ref[hh] = xr
            if keep_ref is not None:
                keep_ref[:, cols] = xr

    normed_rope(pool_width, qn_ref, q_ref, None)
    normed_rope(pool_width + attn_width, kn_ref, k_ref, kkeep_ref)
    vproj = _dot(h, win_ref[:, pool_width + 2 * attn_width:pool_width + 3 * attn_width])
    vkeep_ref[...] = vproj
    for hh in range(n_heads):
        v_ref[hh] = vproj[:, hh * HEAD_DIM:(hh + 1) * HEAD_DIM]


def _in_proj(x, cos2, sin2, pool_prev, norm, w_in, q_norm, k_norm, w_pool, pool_scale, *, tm, nb, pos0, keep):
    rows, d_model = x.shape
    pool_width = pool_scale.shape[-1]
    attn_width = (w_in.shape[1] - pool_width) // 3
    n_heads = attn_width // HEAD_DIM
    n_tiles = rows // tm
    assert rows % tm == 0 and keep % tm == 0 and nb & (nb - 1) == 0 and (tm >= POOL_TAIL * nb or n_tiles == 1)
    first_keep = n_tiles - keep // tm
    tile = lambda width: pl.BlockSpec((tm, width), lambda i: (i, 0))
    heads = pl.BlockSpec((n_heads, tm, HEAD_DIM), lambda i: (0, i, 0))
    keep_spec = pl.BlockSpec((tm, attn_width), lambda i: (jnp.maximum(i - first_keep, 0), 0))
    vmem = (2 * tm * d_model * 4 + w_in.size * 2 + 2 * 2 * tm * LANES * 4 + 2 * tm * pool_width * 2
            + 3 * 2 * tm * attn_width * 4 + 2 * 2 * tm * attn_width * 4 + (POOL_TAIL * nb + tm) * pool_width * 4
            + 2 * POOL_TAIL * nb * pool_width * 4 + 6 * tm * attn_width * 4 + tm * d_model * 6 + (4 << 20))
    hm_shape = jax.ShapeDtypeStruct((n_heads, rows, HEAD_DIM), F32)
    return pl.pallas_call(
        functools.partial(_in_proj_kernel, tm=tm, nb=nb, pos0=pos0, n_tiles=n_tiles, n_heads=n_heads),
        grid=(n_tiles,),
        in_specs=[tile(d_model), tile(HEAD_DIM), tile(HEAD_DIM), _const_spec(pool_prev.shape),
                  _const_spec(norm.shape), _const_spec(w_in.shape), _const_spec(q_norm.shape),
                  _const_spec(k_norm.shape), _const_spec(w_pool.shape), _const_spec(pool_scale.shape)],
        out_specs=[tile(pool_width), heads, heads, heads, keep_spec, keep_spec,
                   pl.BlockSpec((POOL_STATE * nb, pool_width), lambda i: (0, 0))],
        out_shape=[jax.ShapeDtypeStruct((rows, pool_width), BF16), hm_shape, hm_shape, hm_shape,
                   jax.ShapeDtypeStruct((keep, attn_width), F32),
                   jax.ShapeDtypeStruct((keep, attn_width), F32),
                   jax.ShapeDtypeStruct((POOL_STATE * nb, pool_width), F32)],
        scratch_shapes=[pltpu.VMEM((POOL_TAIL * nb + tm, pool_width), F32)],
        compiler_params=_params(1, vmem),
        name="in_proj",
    )(x, cos2, sin2, pool_prev, norm, w_in, q_norm, k_norm, w_pool, pool_scale)


CLASSES = 4
UNITS_PER_CHUNK = 4


def _dilated_attn_kernel(q_ref, k_ref, v_ref, o_ref,
                         yq, yk, yv, q1, k1, v1, q4, k4, v4, q16, k16, v16,
                         o1, l1, o4, l4, o16, l16, out_f, bias):
    i = pl.program_id(1)
    sb = q_ref.shape[0]
    n4 = sb // CLASSES
    nk4 = BAND + n4
    nk16 = 2 * BAND
    slots = sb // BAND
    chunk = UNITS_PER_CHUNK * BAND
    scale = HEAD_DIM ** -0.5
    keys = ((k1, v1), (k4, v4), (k16, v16))
    prev_rows = ([(0, sb)], [(r * nk4, r * nk4 + n4) for r in range(CLASSES)],
                 [(s * nk16, s * nk16 + BAND) for s in range(slots)])

    @pl.when(i == 0)
    def _():
        for (kd, vd), rows in zip(keys, prev_rows):
            for dst, _ in rows:
                kd[dst:dst + BAND, :] = jnp.zeros((BAND, HEAD_DIM), BF16)
                vd[dst:dst + BAND, :] = jnp.zeros((BAND, HEAD_DIM), BF16)
        row = lax.broadcasted_iota(jnp.int32, (chunk, 2 * BAND), 0)
        col = lax.broadcasted_iota(jnp.int32, (chunk, 2 * BAND), 1)
        qa = row & (BAND - 1)
        in_band = (col >= qa) & (col <= qa + BAND)
        started = col >= BAND
        bias[0] = jnp.where(in_band, 0.0, NEG_INF)
        bias[1] = jnp.where(in_band & (started | (row >= BAND)), 0.0, NEG_INF)
        bias[2] = jnp.where(in_band & started, 0.0, NEG_INF)

    @pl.when(i > 0)
    def _():
        for (kd, vd), rows in zip(keys, prev_rows):
            for dst, src in rows:
                kd[dst:dst + BAND, :] = kd[src:src + BAND, :]
                vd[dst:dst + BAND, :] = vd[src:src + BAND, :]

    q1[...] = q_ref[...].astype(BF16)
    k1[BAND:BAND + sb, :] = k_ref[...].astype(BF16)
    v1[BAND:BAND + sb, :] = v_ref[...].astype(BF16)
    for src, y, d4, d16, lead in ((q_ref, yq, q4, q16, 0), (k_ref, yk, k4, k16, BAND), (v_ref, yv, v4, v16, BAND)):
        for r1 in range(CLASSES):
            rows = src[pl.ds(r1, n4, stride=CLASSES), :]
            y[r1 * n4:(r1 + 1) * n4, :] = rows
            d4[r1 * (lead + n4) + lead:(r1 + 1) * (lead + n4), :] = rows.astype(BF16)
        for s in range(slots):
            r1, r2 = divmod(s, CLASSES)
            rows = y[pl.ds(r1 * n4 + r2, BAND, stride=CLASSES), :]
            d16[s * (lead + BAND) + lead:(s + 1) * (lead + BAND), :] = rows.astype(BF16)

    first = (i == 0).astype(jnp.int32)

    def branch(qd, kd, vd, units, pattern, store):
        for c in range(len(units) // UNITS_PER_CHUNK):
            todo = units[c * UNITS_PER_CHUNK:(c + 1) * UNITS_PER_CHUNK]
            s = jnp.concatenate([_dot_nt(qd[q0:q0 + BAND, :], kd[k0:k0 + 2 * BAND, :]) for q0, k0 in todo], axis=0)
            s = s * scale + bias[pattern(c)]
            m = jnp.max(s, axis=-1, keepdims=True)
            p = jnp.exp(s - m)
            l = jnp.sum(p, axis=-1, keepdims=True)
            p = p.astype(BF16)
            lse = m + jnp.log(l)
            inv = 1.0 / l
            for uu, (q0, k0) in enumerate(todo):
                rows = slice(uu * BAND, (uu + 1) * BAND)
                o = _dot(p[rows, :], vd[k0:k0 + 2 * BAND, :]) * inv[rows, :]
                store(c * UNITS_PER_CHUNK + uu, o, jnp.broadcast_to(lse[rows, :], o.shape))

    def store_rows(od, ld, rows_of):
        def store(u, o, lse):
            od[rows_of(u), :] = o
            ld[rows_of(u), :] = lse
        return store

    blocks4 = n4 // BAND
    branch(q1, k1, v1, [(u * BAND, u * BAND) for u in range(slots)],
           lambda c: first if c == 0 else 0,
           store_rows(o1, l1, lambda u: slice(u * BAND, (u + 1) * BAND)))
    branch(q4, k4, v4, [(r * n4 + b * BAND, r * nk4 + b * BAND) for r in range(CLASSES) for b in range(blocks4)],
           lambda c: first,
           store_rows(o4, l4, lambda u: slice(u * BAND, (u + 1) * BAND)))
    branch(q16, k16, v16, [(s * BAND, s * nk16) for s in range(slots)],
           lambda c: 2 * first,
           store_rows(o16, l16, lambda s: pl.ds((s // CLASSES) * n4 + s % CLASSES, BAND, stride=CLASSES)))

    for r1 in range(CLASSES):
        rows = slice(r1 * n4, (r1 + 1) * n4)
        nat = pl.ds(r1, n4, stride=CLASSES)
        la, lb, lc = l1[nat, :], l4[rows, :], l16[rows, :]
        m = jnp.maximum(jnp.maximum(la, lb), lc)
        wa, wb, wc = jnp.exp(la - m), jnp.exp(lb - m), jnp.exp(lc - m)
        num = wa * o1[nat, :] + wb * o4[rows, :] + wc * o16[rows, :]
        out_f[nat, :] = num / (wa + wb + wc)
    o_ref[...] = out_f[...].astype(o_ref.dtype)


def _dilated_attn(q_hm, k_hm, v_hm):
    n_heads, rows, _ = q_hm.shape
    sb = SUPER_BLOCK
    assert rows % sb == 0 and [d for _, d in DILATIONS] == [1, CLASSES, CLASSES ** 2]
    assert all(w == BAND * d for w, d in DILATIONS) and sb == BAND * CLASSES ** 2
    assert (sb // BAND) % UNITS_PER_CHUNK == 0 and sb // CLASSES == UNITS_PER_CHUNK * BAND
    block = pl.BlockSpec((None, sb, HEAD_DIM), lambda h, i: (h, i, 0))
    blk = lambda n, dt: pltpu.VMEM((n, HEAD_DIM), dt)
    lead = (BAND, CLASSES * BAND, sb)
    scratch = ([blk(sb, F32)] * 3
               + [s for n in lead for s in (blk(sb, BF16), blk(n + sb, BF16), blk(n + sb, BF16))]
               + [blk(sb, F32)] * 7
               + [pltpu.VMEM((3, UNITS_PER_CHUNK * BAND, 2 * BAND), F32)])
    vmem = (2 * 3 * sb * HEAD_DIM * 4 + 2 * sb * HEAD_DIM * 2 + 10 * sb * HEAD_DIM * 4
            + (9 * sb + 2 * sum(lead)) * HEAD_DIM * 2 + 3 * UNITS_PER_CHUNK * BAND * 2 * BAND * 4 + (12 << 20))
    return pl.pallas_call(
        _dilated_attn_kernel,
        grid=(n_heads, rows // sb),
        in_specs=[block, block, block],
        out_specs=pl.BlockSpec((sb, HEAD_DIM), lambda h, i: (i, h)),
        out_shape=jax.ShapeDtypeStruct((rows, n_heads * HEAD_DIM), BF16),
        scratch_shapes=scratch,
        compiler_params=_params(2, vmem),
        name="dilated_attn",
    )(q_hm, k_hm, v_hm)


def _sample_attn_kernel(q_ref, kn_ref, vn_ref, ck_ref, cv_ref, o_ref, *, n_heads, past_len):
    tq = q_ref.shape[0]
    cache = ck_ref.shape[0] // n_heads
    scale = HEAD_DIM ** -0.5

    def multiplicity(n_keys, key0):
        t = lax.broadcasted_iota(jnp.int32, (tq, n_keys), 0)
        key = key0 + lax.broadcasted_iota(jnp.int32, (tq, n_keys), 1)
        dist = cache + t - key
        seen = (dist >= 0) & (past_len + t - dist >= 0)
        n = jnp.zeros((tq, n_keys), F32)
        for window, dil in DILATIONS:
            hit = seen & (dist <= window) & ((dist & (dil - 1)) == 0)
            n = n + jnp.where(hit, 1.0, 0.0)
        return n

    n_c = multiplicity(cache, 0)
    n_n = multiplicity(tq, cache)
    for hh in range(n_heads):
        cols = slice(hh * HEAD_DIM, (hh + 1) * HEAD_DIM)
        head_rows = pl.ds(hh, cache, stride=n_heads)
        q = q_ref[:, cols].astype(BF16)
        s_c = jnp.where(n_c > 0, _dot_nt(q, ck_ref[head_rows, :].astype(BF16)) * scale, NEG_INF)
        s_n = jnp.where(n_n > 0, _dot_nt(q, kn_ref[:, cols].astype(BF16)) * scale, NEG_INF)
        m = jnp.maximum(jnp.max(s_c, axis=-1, keepdims=True), jnp.max(s_n, axis=-1, keepdims=True))
        p_c = n_c * jnp.exp(s_c - m)
        p_n = n_n * jnp.exp(s_n - m)
        den = jnp.sum(p_c, axis=-1, keepdims=True) + jnp.sum(p_n, axis=-1, keepdims=True)
        num = (_dot(p_c.astype(BF16), cv_ref[head_rows, :].astype(BF16))
               + _dot(p_n.astype(BF16), vn_ref[:, cols].astype(BF16)))
        o_ref[:, cols] = (num / den).astype(o_ref.dtype)


def _sample_attn(q, k_new, v_new, cache_k, cache_v, *, past_len):
    batch, tq, width = q.shape
    n_heads = width // HEAD_DIM
    cache_rows = cache_k.shape[1]
    assert all(d & (d - 1) == 0 for _, d in DILATIONS) and cache_k.shape[2] == HEAD_DIM
    new = pl.BlockSpec((None, tq, width), lambda b: (b, 0, 0))
    old = pl.BlockSpec((None, cache_rows, HEAD_DIM), lambda b: (b, 0, 0))
    vmem = 2 * 2 * cache_rows * HEAD_DIM * 4 + 8 * cache_rows // n_heads * HEAD_DIM * 4 + (8 << 20)
    return pl.pallas_call(
        functools.partial(_sample_attn_kernel, n_heads=n_heads, past_len=past_len),
        grid=(batch,),
        in_specs=[new, new, new, old, old],
        out_specs=new,
        out_shape=jax.ShapeDtypeStruct((batch, tq, width), BF16),
        compiler_params=_params(1, vmem),
        name="sample_attn",
    )(q, k_new, v_new, cache_k, cache_v)


def _memory_kv_kernel(mem_ref, norm_ref, wk_ref, wv_ref, kn_ref, k_ref, v_ref):
    m = _rms(mem_ref[...], norm_ref[...]).astype(BF16)
    k = _dot(m, wk_ref[...])
    for hh in range(k.shape[1] // MEM_HEAD_DIM):
        cols = slice(hh * MEM_HEAD_DIM, (hh + 1) * MEM_HEAD_DIM)
        k_ref[:, cols] = _rms(k[:, cols], kn_ref[...])
    v_ref[...] = _dot(m, wv_ref[...])


def _memory_kv(mem, norm_src, w_k, w_v, k_norm):
    n_mem = mem.shape[0]
    width = w_k.shape[1]
    args = (mem, norm_src, w_k, w_v, k_norm)
    out = jax.ShapeDtypeStruct((n_mem, width), F32)
    return pl.pallas_call(
        _memory_kv_kernel,
        grid=(1,),
        in_specs=[_const_spec(a.shape) for a in args],
        out_specs=[pl.BlockSpec((n_mem, width), lambda i: (0, 0))] * 2,
        out_shape=[out, out],
        compiler_params=_params(1, 32 << 20),
        name="memory_kv",
    )(*args)


def _mem_mix_kernel(x_ref, pool_ref, attn_ref, wout_ref, nmem_ref, wq_ref, qn_ref, mk_ref, mv_ref, wo_ref, nffn_ref,
                    x2_ref, h3_ref, *, nb, n_mem):
    tm = x_ref.shape[0]
    pool_width = pool_ref.shape[1]
    x1 = x_ref[...] + _dot(pool_ref[...], wout_ref[0:pool_width, :]) + _dot(attn_ref[...], wout_ref[pool_width:, :])
    h2 = _rms(x1, nmem_ref[...]).astype(BF16)
    qm = _dot(h2, wq_ref[...])
    scale = MEM_HEAD_DIM ** -0.5
    n_keys = mk_ref.shape[0]
    if nb > 1:
        seq_q = (pl.program_id(0) * tm + lax.broadcasted_iota(jnp.int32, (tm, n_keys), 0)) & (nb - 1)
        seq_k = lax.broadcasted_iota(jnp.int32, (tm, n_keys), 1) >> int(math.log2(n_mem))
        own = seq_q == seq_k
    heads = []
    for hh in range(qm.shape[1] // MEM_HEAD_DIM):
        cols = slice(hh * MEM_HEAD_DIM, (hh + 1) * MEM_HEAD_DIM)
        q = _rms(qm[:, cols], qn_ref[...]).astype(BF16)
        s = _dot_nt(q, mk_ref[:, cols].astype(BF16)) * scale
        if nb > 1:
            s = jnp.where(own, s, NEG_INF)
        p = jnp.exp(s - jnp.max(s, axis=-1, keepdims=True))
        o = _dot(p.astype(BF16), mv_ref[:, cols].astype(BF16)) / jnp.sum(p, axis=-1, keepdims=True)
        heads.append(o.astype(BF16))
    x2 = x1 + _dot(jnp.concatenate(heads, axis=-1), wo_ref[...])
    x2_ref[...] = x2
    h3_ref[...] = _rms(x2, nffn_ref[...]).astype(h3_ref.dtype)


def _mem_mix(x, pool_out, attn_out, w_out, norm_mem, w_q, q_norm, mem_k, mem_v, w_o, norm_ffn, *, tm, nb, n_mem):
    rows, d_model = x.shape
    assert rows % tm == 0 and nb & (nb - 1) == 0 and n_mem & (n_mem - 1) == 0 and mem_k.shape[0] == nb * n_mem
    tile = lambda a: pl.BlockSpec((tm, a.shape[1]), lambda i: (i, 0))
    consts = (w_out, norm_mem, w_q, q_norm, mem_k, mem_v, w_o, norm_ffn)
    vmem = (sum(c.size * c.dtype.itemsize for c in consts) + 2 * tm * d_model * (4 + 2 + 4 + 2)
            + tm * d_model * 16 + 4 * tm * mem_k.shape[0] * 4 + (6 << 20))
    return pl.pallas_call(
        functools.partial(_mem_mix_kernel, nb=nb, n_mem=n_mem),
        grid=(rows // tm,),
        in_specs=[tile(x), tile(pool_out), tile(attn_out)] + [_const_spec(c.shape) for c in consts],
        out_specs=[pl.BlockSpec((tm, d_model), lambda i: (i, 0))] * 2,
        out_shape=[jax.ShapeDtypeStruct((rows, d_model), F32), jax.ShapeDtypeStruct((rows, d_model), BF16)],
        compiler_params=_params(1, vmem),
        name="mem_mix",
    )(x, pool_out, attn_out, *consts)


def _conv_ffn_kernel(h_ref, x_ref, wg_ref, wu_ref, cw_ref, cb_ref, wd_ref, prev_ref, y_ref, state_ref,
                     g_ext, g_scr, *, nb):
    i = pl.program_id(0)
    j = pl.program_id(1)
    tm = h_ref.shape[0]
    tail = CONV_TAIL * nb
    h = h_ref[...]

    @pl.when(i == 0)
    def _():
        g_ext[0:tail, :] = prev_ref[...]

    @pl.when(i > 0)
    def _():
        g_ext[0:tail, :] = g_scr[j]

    g_ext[tail:tail + tm, :] = _dot(h, wg_ref[...])
    up = _dot(h, wu_ref[...])
    c = cb_ref[...]
    for tap in range(CONV_WIDTH):
        c = c + g_ext[tap * nb:tap * nb + tm, :] * cw_ref[tap:tap + 1, :]
    act = (c * jax.nn.sigmoid(c) * up).astype(BF16)
    y = _dot(act, wd_ref[...])

    @pl.when(j == 0)
    def _():
        y_ref[...] = x_ref[...] + y

    @pl.when(j > 0)
    def _():
        y_ref[...] += y

    last = g_ext[tm:tm + tail, :]
    g_scr[j] = last
    state_ref[:, pl.ds(pl.multiple_of(j * last.shape[1], LANES), last.shape[1])] = last


def _conv_ffn(h3, x2, w_gate, w_up, conv_w, conv_b, w_down, conv_prev, *, tm, tf, nb):
    rows, d_model = x2.shape
    d_ff = w_gate.shape[1]
    assert rows % tm == 0 and d_ff % tf == 0 and tm >= CONV_TAIL * nb
    tail = CONV_TAIL * nb
    row_tile = pl.BlockSpec((tm, d_model), lambda i, j: (i, 0))
    chunk = lambda n: pl.BlockSpec((n, tf), lambda i, j: (0, j))
    vmem = (2 * tm * d_model * (2 + 4 + 4) + 2 * 3 * d_model * tf * 2 + (tm + 3 * tail) * tf * 4
            + d_ff // tf * tail * tf * 4 + 5 * tm * tf * 4 + tm * d_model * 4 + (6 << 20))
    return pl.pallas_call(
        functools.partial(_conv_ffn_kernel, nb=nb),
        grid=(rows // tm, d_ff // tf),
        in_specs=[row_tile, row_tile, chunk(d_model), chunk(d_model), chunk(CONV_WIDTH), chunk(1),
                  pl.BlockSpec((tf, d_model), lambda i, j: (j, 0)), chunk(tail)],
        out_specs=[row_tile, pl.BlockSpec((tail, d_ff), lambda i, j: (0, 0))],
        out_shape=[jax.ShapeDtypeStruct((rows, d_model), F32), jax.ShapeDtypeStruct((tail, d_ff), F32)],
        scratch_shapes=[pltpu.VMEM((tail + tm, tf), F32), pltpu.VMEM((d_ff // tf, tail, tf), F32)],
        compiler_params=_params(2, vmem),
        name="conv_ffn",
    )(h3, x2, w_gate, w_up, conv_w, conv_b, w_down, conv_prev)


def _rope_tables(pos):
    half = HEAD_DIM // 2
    inv = 1.0 / (ROPE_THETA ** (jnp.arange(half, dtype=F32) * (2.0 / HEAD_DIM)))
    ang = pos.astype(F32)[:, None] * inv[None, :]
    cos, sin = jnp.cos(ang), jnp.sin(ang)
    return jnp.concatenate([cos, cos], axis=-1), jnp.concatenate([-sin, sin], axis=-1)


def _row_tile(rows, want):
    return want if rows % want == 0 else rows


def _layer(x, pos0, nb, pool_prev, conv_prev, mem_k, mem_v, n_mem, attend, p, keep):
    rows = x.shape[0]
    cos2, sin2 = _rope_tables(pos0 + jnp.arange(rows) // nb)
    pool_out, q_hm, k_hm, v_hm, k_keep, v_keep, pool_state = _in_proj(
        x, cos2, sin2, pool_prev, p["norm_mix"], p["w_in"], p["q_norm"], p["k_norm"], p["w_pool"], p["pool_scale"],
        tm=_row_tile(rows, IN_PROJ_ROWS), nb=nb, pos0=pos0, keep=keep)
    attn_out = attend(q_hm, k_hm, v_hm, k_keep, v_keep)
    x2, h3 = _mem_mix(x, pool_out, attn_out, p["w_out"], p["norm_mem"], p["w_mem_q"], p["mem_q_norm"], mem_k, mem_v,
                      p["w_mem_o"], p["norm_ffn"], tm=_row_tile(rows, MEM_MIX_ROWS), nb=nb, n_mem=n_mem)
    y, conv_state = _conv_ffn(h3, x2, p["w_gate"], p["w_up"], p["conv_w"], p["conv_b"], p["w_down"], conv_prev,
                              tm=_row_tile(rows, FFN_ROWS), tf=FFN_COLS, nb=nb)
    return y, pool_state, k_keep, v_keep, conv_state


def kernel(x_prompt, x_sample, state_pool, cache_win_k, cache_win_v, cache_mem_k, cache_mem_v, state_conv, mem_prompt, norm_mix, w_in, q_norm, k_norm, w_pool, pool_scale, w_out, norm_mem, norm_mem_src, w_mem_q, w_mem_k, w_mem_v, mem_q_norm, mem_k_norm, w_mem_o, norm_ffn, w_gate, w_up, conv_w, conv_b, w_down):
    bp, tp, d_model = x_prompt.shape
    bs, ts, _ = x_sample.shape
    depth = w_in.shape[0]
    assert bp == 1
    pool_width = pool_scale.shape[-1]
    d_ff = w_gate.shape[-1]
    n_mem = mem_prompt.shape[1]
    n_heads = cache_win_k.shape[3]
    mem_heads = cache_mem_k.shape[3]
    attn_width = n_heads * HEAD_DIM
    mem_width = mem_heads * MEM_HEAD_DIM
    keep_p = min(MAX_WINDOW, tp)
    row = lambda a: a.reshape(1, -1)

    xp = x_prompt.reshape(tp, d_model)
    xs = x_sample.transpose(1, 0, 2).reshape(ts * bs, d_model)
    outs = {k: [] for k in ("p_pool", "p_k", "p_v", "p_mk", "p_mv", "p_conv", "s_pool", "s_k", "s_v", "s_conv")}
    for l in range(depth):
        prm = {
            "norm_mix": row(norm_mix[l]), "w_in": w_in[l].astype(BF16), "q_norm": row(q_norm[l]),
            "k_norm": row(k_norm[l]), "w_pool": w_pool[l].astype(BF16), "pool_scale": row(pool_scale[l]),
            "w_out": w_out[l].astype(BF16), "norm_mem": row(norm_mem[l]), "w_mem_q": w_mem_q[l].astype(BF16),
            "mem_q_norm": row(mem_q_norm[l]), "w_mem_o": w_mem_o[l].astype(BF16), "norm_ffn": row(norm_ffn[l]),
            "w_gate": w_gate[l].astype(BF16), "w_up": w_up[l].astype(BF16), "conv_w": conv_w[l],
            "conv_b": row(conv_b[l]), "w_down": w_down[l].astype(BF16),
        }
        mk, mv = _memory_kv(mem_prompt[0], row(norm_mem_src[l]), w_mem_k[l].astype(BF16), w_mem_v[l].astype(BF16),
                            row(mem_k_norm[l]))
        xp, pool_st, kp, vp, conv_st = _layer(
            xp, 0, 1, jnp.zeros((POOL_TAIL, pool_width), F32), jnp.zeros((CONV_TAIL, d_ff), F32), mk, mv, n_mem,
            lambda q, k, v, kk, vk: _dilated_attn(q, k, v), prm, keep_p)
        outs["p_pool"].append(pool_st[None])
        outs["p_k"].append(kp.reshape(1, keep_p, n_heads, HEAD_DIM))
        outs["p_v"].append(vp.reshape(1, keep_p, n_heads, HEAD_DIM))
        outs["p_mk"].append(mk.reshape(1, n_mem, mem_heads, MEM_HEAD_DIM))
        outs["p_mv"].append(mv.reshape(1, n_mem, mem_heads, MEM_HEAD_DIM))
        outs["p_conv"].append(conv_st[None])

        pool_prev = jnp.pad(state_pool[l].transpose(1, 0, 2), ((POOL_TAIL - POOL_STATE, 0), (0, 0), (0, 0)))
        conv_prev = state_conv[l].transpose(1, 0, 2).reshape(CONV_TAIL * bs, d_ff)
        cache = cache_win_k.shape[2]

        def attend_samples(q_hm, k_hm, v_hm, k_new, v_new, l=l, cache=cache):
            pad = lambda a: jnp.pad(a, ((0, 0), (0, -ts % SUBLANES), (0, 0)))
            q = q_hm.reshape(n_heads, ts, bs, HEAD_DIM).transpose(2, 1, 0, 3).reshape(bs, ts, attn_width)
            to_seq = lambda a: a.reshape(ts, bs, attn_width).transpose(1, 0, 2)
            o = _sample_attn(pad(q), pad(to_seq(k_new)), pad(to_seq(v_new)),
                             cache_win_k[l].reshape(bs, cache * n_heads, HEAD_DIM),
                             cache_win_v[l].reshape(bs, cache * n_heads, HEAD_DIM), past_len=PAST_LEN)
            return o[:, :ts].transpose(1, 0, 2).reshape(ts * bs, attn_width)

        xs, pool_st, ks, vs, conv_st = _layer(
            xs, PAST_LEN, bs, pool_prev.reshape(POOL_TAIL * bs, pool_width), conv_prev,
            cache_mem_k[l].reshape(bs * n_mem, mem_width), cache_mem_v[l].reshape(bs * n_mem, mem_width), n_mem,
            attend_samples, prm, ts * bs)
        seq_major = lambda a, t: a.reshape(t, bs, -1).transpose(1, 0, 2)
        outs["s_pool"].append(seq_major(pool_st, POOL_STATE))
        outs["s_k"].append(seq_major(ks, ts).reshape(bs, ts, n_heads, HEAD_DIM))
        outs["s_v"].append(seq_major(vs, ts).reshape(bs, ts, n_heads, HEAD_DIM))
        outs["s_conv"].append(seq_major(conv_st, CONV_TAIL))
    stack = lambda k: jnp.stack(outs[k])
    return (xp.reshape(bp, tp, d_model), xs.reshape(ts, bs, d_model).transpose(1, 0, 2),
            stack("p_pool"), stack("p_k"), stack("p_v"), stack("p_mk"), stack("p_mv"), stack("p_conv"),
            stack("s_pool"), stack("s_k"), stack("s_v"), stack("s_conv"))
```

```python
import functools
import math

import jax
import jax.numpy as jnp
from jax import lax
from jax.experimental import pallas as pl
from jax.experimental.pallas import tpu as pltpu

PAST_LEN = 16384
POOL_WINDOWS = (2, 4, 8, 16)
HEAD_DIM = 128
DILATIONS = ((128, 1), (512, 4), (2048, 16))
MAX_WINDOW = 2048
ROPE_THETA = 10000.0
MEM_HEAD_DIM = 128
CONV_WIDTH = 3
EPS = 1e-6
NEG_INF = -1e30

POOL_STATE = max(POOL_WINDOWS) - 1
POOL_TAIL = max(POOL_WINDOWS)
CONV_TAIL = CONV_WIDTH - 1
BAND = DILATIONS[0][0]
SUPER_BLOCK = MAX_WINDOW

V7X_SCOPED_VMEM_LIMIT = 60000 * 1024
LANES = 128
SUBLANES = 8

IN_PROJ_ROWS = 512
MEM_MIX_ROWS = 512
FFN_ROWS = 512
FFN_COLS = 512

F32 = jnp.float32
BF16 = jnp.bfloat16


def _const_spec(shape):
    return pl.BlockSpec(shape, lambda *_: (0,) * len(shape), pipeline_mode=pl.Buffered(1))


def _params(n_axes, vmem_bytes):
    return pltpu.CompilerParams(
        dimension_semantics=("arbitrary",) * n_axes,
        vmem_limit_bytes=min(int(vmem_bytes), V7X_SCOPED_VMEM_LIMIT),
    )


def _rms(x, gain):
    return x * lax.rsqrt(jnp.mean(x * x, axis=-1, keepdims=True) + EPS) * gain


def _dot(a, b):
    return jnp.dot(a, b, preferred_element_type=F32)


def _dot_nt(a, b):
    return lax.dot_general(a, b, (((1,), (1,)), ((), ())), preferred_element_type=F32)


def _in_proj_kernel(x_ref, cos_ref, sin_ref, prev_ref, norm_ref, win_ref, qn_ref, kn_ref, wpool_ref, pscale_ref,
                    pool_ref, q_ref, k_ref, v_ref, pstate_ref, u_scr,
                    *, tm, nb, pos0, n_tiles, n_heads):
    i = pl.program_id(0)
    tail = POOL_TAIL * nb
    pool_width = pool_ref.shape[1]
    attn_width = n_heads * HEAD_DIM
    group = pool_width // len(POOL_WINDOWS)

    h = _rms(x_ref[...], norm_ref[...]).astype(BF16)

    @pl.when(i == 0)
    def _():
        u_scr[0:tail, :] = prev_ref[...]

    u_scr[tail:tail + tm, :] = _dot(h, win_ref[:, 0:pool_width])
    row = i * tm + lax.broadcasted_iota(jnp.int32, (tm, 1), 0)
    pos = pos0 + (row >> int(math.log2(nb)))
    for g, w in enumerate(POOL_WINDOWS):
        cols = slice(g * group, (g + 1) * group)
        u_t = u_scr[tail:tail + tm, cols]
        acc = u_t
        for j in range(1, w):
            acc = acc + u_scr[tail - j * nb:tail - j * nb + tm, cols]
        cnt = jnp.minimum(pos + 1, w).astype(F32)
        diff = acc / cnt - u_t
        y = _dot(diff.astype(BF16), wpool_ref[g]) * pscale_ref[:, cols]
        pool_ref[:, cols] = y.astype(pool_ref.dtype)

    @pl.when(i == n_tiles - 1)
    def _():
        pstate_ref[...] = u_scr[tm + nb:tm + tail, :]

    if n_tiles > 1:
        u_scr[0:tail, :] = u_scr[tm:tm + tail, :]

    cosv = cos_ref[...]
    sinv = sin_ref[...]

    def normed_rope(col0, gain_ref, hm_ref):
        proj = _dot(h, win_ref[:, col0:col0 + attn_width])
        for hh in range(n_heads):
            xn = _rms(proj[:, hh * HEAD_DIM:(hh + 1) * HEAD_DIM], gain_ref[...])
            hm_ref[hh] = xn * cosv + pltpu.roll(xn, HEAD_DIM // 2, 1) * sinv

    normed_rope(pool_width, qn_ref, q_ref)
    normed_rope(pool_width + attn_width, kn_ref, k_ref)
    vproj = _dot(h, win_ref[:, pool_width + 2 * attn_width:pool_width + 3 * attn_width])
    for hh in range(n_heads):
        v_ref[hh] = vproj[:, hh * HEAD_DIM:(hh + 1) * HEAD_DIM]


def _in_proj(x, cos2, sin2, pool_prev, norm, w_in, q_norm, k_norm, w_pool, pool_scale, *, tm, nb, pos0):
    rows, d_model = x.shape
    pool_width = pool_scale.shape[-1]
    attn_width = (w_in.shape[1] - pool_width) // 3
    n_heads = attn_width // HEAD_DIM
    n_tiles = rows // tm
    assert rows % tm == 0 and nb & (nb - 1) == 0 and (tm >= POOL_TAIL * nb or n_tiles == 1)
    tile = lambda width: pl.BlockSpec((tm, width), lambda i: (i, 0))
    heads = pl.BlockSpec((n_heads, tm, HEAD_DIM), lambda i: (0, i, 0))
    vmem = (2 * tm * d_model * 4 + w_in.size * 2 + 2 * 2 * tm * LANES * 4 + 2 * tm * pool_width * 2
            + 3 * 2 * tm * attn_width * 4 + (POOL_TAIL * nb + tm) * pool_width * 4
            + 2 * POOL_TAIL * nb * pool_width * 4 + 6 * tm * attn_width * 4 + tm * d_model * 6 + (4 << 20))
    hm_shape = jax.ShapeDtypeStruct((n_heads, rows, HEAD_DIM), F32)
    return pl.pallas_call(
        functools.partial(_in_proj_kernel, tm=tm, nb=nb, pos0=pos0, n_tiles=n_tiles, n_heads=n_heads),
        grid=(n_tiles,),
        in_specs=[tile(d_model), tile(HEAD_DIM), tile(HEAD_DIM), _const_spec(pool_prev.shape),
                  _const_spec(norm.shape), _const_spec(w_in.shape), _const_spec(q_norm.shape),
                  _const_spec(k_norm.shape), _const_spec(w_pool.shape), _const_spec(pool_scale.shape)],
        out_specs=[tile(pool_width), heads, heads, heads,
                   pl.BlockSpec((POOL_STATE * nb, pool_width), lambda i: (0, 0))],
        out_shape=[jax.ShapeDtypeStruct((rows, pool_width), BF16), hm_shape, hm_shape, hm_shape,
                   jax.ShapeDtypeStruct((POOL_STATE * nb, pool_width), F32)],
        scratch_shapes=[pltpu.VMEM((POOL_TAIL * nb + tm, pool_width), F32)],
        compiler_params=_params(1, vmem),
        name="in_proj",
    )(x, cos2, sin2, pool_prev, norm, w_in, q_norm, k_norm, w_pool, pool_scale)


CLASSES = 4
UNITS_PER_CHUNK = 4


def _dilated_attn_kernel(q_ref, k_ref, v_ref, o_ref, klast_ref, vlast_ref,
                         yq, yk, yv, q1, k1, v1, q4, k4, v4, q16, k16, v16,
                         o1, l1, o4, l4, o16, l16, out_f, bias):
    i = pl.program_id(1)
    sb = q_ref.shape[0]

    @pl.when(i == pl.num_programs(1) - 1)
    def _():
        klast_ref[...] = k_ref[...]
        vlast_ref[...] = v_ref[...]

    n4 = sb // CLASSES
    nk4 = BAND + n4
    nk16 = 2 * BAND
    slots = sb // BAND
    chunk = UNITS_PER_CHUNK * BAND
    scale = HEAD_DIM ** -0.5
    keys = ((k1, v1), (k4, v4), (k16, v16))
    prev_rows = ([(0, sb)], [(r * nk4, r * nk4 + n4) for r in range(CLASSES)],
                 [(s * nk16, s * nk16 + BAND) for s in range(slots)])

    @pl.when(i == 0)
    def _():
        for (kd, vd), rows in zip(keys, prev_rows):
            for dst, _ in rows:
                kd[dst:dst + BAND, :] = jnp.zeros((BAND, HEAD_DIM), BF16)
                vd[dst:dst + BAND, :] = jnp.zeros((BAND, HEAD_DIM), BF16)
        row = lax.broadcasted_iota(jnp.int32, (chunk, 2 * BAND), 0)
        col = lax.broadcasted_iota(jnp.int32, (chunk, 2 * BAND), 1)
        qa = row & (BAND - 1)
        in_band = (col >= qa) & (col <= qa + BAND)
        started = col >= BAND
        bias[0] = jnp.where(in_band, 0.0, NEG_INF)
        bias[1] = jnp.where(in_band & (started | (row >= BAND)), 0.0, NEG_INF)
        bias[2] = jnp.where(in_band & started, 0.0, NEG_INF)

    @pl.when(i > 0)
    def _():
        for (kd, vd), rows in zip(keys, prev_rows):
            for dst, src in rows:
                kd[dst:dst + BAND, :] = kd[src:src + BAND, :]
                vd[dst:dst + BAND, :] = vd[src:src + BAND, :]

    q1[...] = q_ref[...].astype(BF16)
    k1[BAND:BAND + sb, :] = k_ref[...].astype(BF16)
    v1[BAND:BAND + sb, :] = v_ref[...].astype(BF16)
    for src, y, d4, d16, lead in ((q_ref, yq, q4, q16, 0), (k_ref, yk, k4, k16, BAND), (v_ref, yv, v4, v16, BAND)):
        for r1 in range(CLASSES):
            rows = src[pl.ds(r1, n4, stride=CLASSES), :]
            y[r1 * n4:(r1 + 1) * n4, :] = rows
            d4[r1 * (lead + n4) + lead:(r1 + 1) * (lead + n4), :] = rows.astype(BF16)
        for s in range(slots):
            r1, r2 = divmod(s, CLASSES)
            rows = y[pl.ds(r1 * n4 + r2, BAND, stride=CLASSES), :]
            d16[s * (lead + BAND) + lead:(s + 1) * (lead + BAND), :] = rows.astype(BF16)

    first = (i == 0).astype(jnp.int32)

    def branch(qd, kd, vd, units, pattern, store):
        for c in range(len(units) // UNITS_PER_CHUNK):
            todo = units[c * UNITS_PER_CHUNK:(c + 1) * UNITS_PER_CHUNK]
            s = jnp.concatenate([_dot_nt(qd[q0:q0 + BAND, :], kd[k0:k0 + 2 * BAND, :]) for q0, k0 in todo], axis=0)
            s = s * scale + bias[pattern(c)]
            m = jnp.max(s, axis=-1, keepdims=True)
            p = jnp.exp(s - m)
            l = jnp.sum(p, axis=-1, keepdims=True)
            p = p.astype(BF16)
            lse = m + jnp.log(l)
            inv = 1.0 / l
            for uu, (q0, k0) in enumerate(todo):
                rows = slice(uu * BAND, (uu + 1) * BAND)
                o = _dot(p[rows, :], vd[k0:k0 + 2 * BAND, :]) * inv[rows, :]
                store(c * UNITS_PER_CHUNK + uu, o, jnp.broadcast_to(lse[rows, :], o.shape))

    def store_rows(od, ld, rows_of):
        def store(u, o, lse):
            od[rows_of(u), :] = o
            ld[rows_of(u), :] = lse
        return store

    blocks4 = n4 // BAND
    branch(q1, k1, v1, [(u * BAND, u * BAND) for u in range(slots)],
           lambda c: first if c == 0 else 0,
           store_rows(o1, l1, lambda u: slice(u * BAND, (u + 1) * BAND)))
    branch(q4, k4, v4, [(r * n4 + b * BAND, r * nk4 + b * BAND) for r in range(CLASSES) for b in range(blocks4)],
           lambda c: first,
           store_rows(o4, l4, lambda u: slice(u * BAND, (u + 1) * BAND)))
    branch(q16, k16, v16, [(s * BAND, s * nk16) for s in range(slots)],
           lambda c: 2 * first,
           store_rows(o16, l16, lambda s: pl.ds((s // CLASSES) * n4 + s % CLASSES, BAND, stride=CLASSES)))

    for r1 in range(CLASSES):
        rows = slice(r1 * n4, (r1 + 1) * n4)
        nat = pl.ds(r1, n4, stride=CLASSES)
        la, lb, lc = l1[nat, :], l4[rows, :], l16[rows, :]
        m = jnp.maximum(jnp.maximum(la, lb), lc)
        wa, wb, wc = jnp.exp(la - m), jnp.exp(lb - m), jnp.exp(lc - m)
        num = wa * o1[nat, :] + wb * o4[rows, :] + wc * o16[rows, :]
        out_f[nat, :] = num / (wa + wb + wc)
    o_ref[...] = out_f[...].astype(o_ref.dtype)


def _dilated_attn(q_hm, k_hm, v_hm):
    n_heads, rows, _ = q_hm.shape
    sb = SUPER_BLOCK
    assert rows % sb == 0 and [d for _, d in DILATIONS] == [1, CLASSES, CLASSES ** 2]
    assert all(w == BAND * d for w, d in DILATIONS) and sb == BAND * CLASSES ** 2
    assert (sb // BAND) % UNITS_PER_CHUNK == 0 and sb // CLASSES == UNITS_PER_CHUNK * BAND
    block = pl.BlockSpec((None, sb, HEAD_DIM), lambda h, i: (h, i, 0))
    blk = lambda n, dt: pltpu.VMEM((n, HEAD_DIM), dt)
    lead = (BAND, CLASSES * BAND, sb)
    scratch = ([blk(sb, F32)] * 3
               + [s for n in lead for s in (blk(sb, BF16), blk(n + sb, BF16), blk(n + sb, BF16))]
               + [blk(sb, F32)] * 7
               + [pltpu.VMEM((3, UNITS_PER_CHUNK * BAND, 2 * BAND), F32)])
    vmem = (2 * 3 * sb * HEAD_DIM * 4 + 2 * sb * HEAD_DIM * 2 + 2 * 2 * sb * HEAD_DIM * 4 + 10 * sb * HEAD_DIM * 4
            + (9 * sb + 2 * sum(lead)) * HEAD_DIM * 2 + 3 * UNITS_PER_CHUNK * BAND * 2 * BAND * 4 + (12 << 20))
    last = pl.BlockSpec((sb, HEAD_DIM), lambda h, i: (0, h))
    last_shape = jax.ShapeDtypeStruct((sb, n_heads * HEAD_DIM), F32)
    return pl.pallas_call(
        _dilated_attn_kernel,
        grid=(n_heads, rows // sb),
        in_specs=[block, block, block],
        out_specs=[pl.BlockSpec((sb, HEAD_DIM), lambda h, i: (i, h)), last, last],
        out_shape=[jax.ShapeDtypeStruct((rows, n_heads * HEAD_DIM), BF16), last_shape, last_shape],
        scratch_shapes=scratch,
        compiler_params=_params(2, vmem),
        name="dilated_attn",
    )(q_hm, k_hm, v_hm)


def _sample_attn_kernel(q_ref, kn_ref, vn_ref, ck_ref, cv_ref, o_ref, *, n_heads, past_len):
    tq = q_ref.shape[0]
    cache = ck_ref.shape[0] // n_heads
    scale = HEAD_DIM ** -0.5

    def multiplicity(n_keys, key0):
        t = lax.broadcasted_iota(jnp.int32, (tq, n_keys), 0)
        key = key0 + lax.broadcasted_iota(jnp.int32, (tq, n_keys), 1)
        dist = cache + t - key
        seen = (dist >= 0) & (past_len + t - dist >= 0)
        n = jnp.zeros((tq, n_keys), F32)
        for window, dil in DILATIONS:
            hit = seen & (dist <= window) & ((dist & (dil - 1)) == 0)
            n = n + jnp.where(hit, 1.0, 0.0)
        return n

    n_c = multiplicity(cache, 0)
    n_n = multiplicity(tq, cache)
    for hh in range(n_heads):
        cols = slice(hh * HEAD_DIM, (hh + 1) * HEAD_DIM)
        head_rows = pl.ds(hh, cache, stride=n_heads)
        q = q_ref[:, cols].astype(BF16)
        s_c = jnp.where(n_c > 0, _dot_nt(q, ck_ref[head_rows, :].astype(BF16)) * scale, NEG_INF)
        s_n = jnp.where(n_n > 0, _dot_nt(q, kn_ref[:, cols].astype(BF16)) * scale, NEG_INF)
        m = jnp.maximum(jnp.max(s_c, axis=-1, keepdims=True), jnp.max(s_n, axis=-1, keepdims=True))
        p_c = n_c * jnp.exp(s_c - m)
        p_n = n_n * jnp.exp(s_n - m)
        den = jnp.sum(p_c, axis=-1, keepdims=True) + jnp.sum(p_n, axis=-1, keepdims=True)
        num = (_dot(p_c.astype(BF16), cv_ref[head_rows, :].astype(BF16))
               + _dot(p_n.astype(BF16), vn_ref[:, cols].astype(BF16)))
        o_ref[:, cols] = (num / den).astype(o_ref.dtype)


def _sample_attn(q, k_new, v_new, cache_k, cache_v, *, past_len):
    batch, tq, width = q.shape
    n_heads = width // HEAD_DIM
    cache_rows = cache_k.shape[1]
    assert all(d & (d - 1) == 0 for _, d in DILATIONS) and cache_k.shape[2] == HEAD_DIM
    new = pl.BlockSpec((None, tq, width), lambda b: (b, 0, 0))
    old = pl.BlockSpec((None, cache_rows, HEAD_DIM), lambda b: (b, 0, 0))
    vmem = 2 * 2 * cache_rows * HEAD_DIM * 4 + 8 * cache_rows // n_heads * HEAD_DIM * 4 + (8 << 20)
    return pl.pallas_call(
        functools.partial(_sample_attn_kernel, n_heads=n_heads, past_len=past_len),
        grid=(batch,),
        in_specs=[new, new, new, old, old],
        out_specs=new,
        out_shape=jax.ShapeDtypeStruct((batch, tq, width), BF16),
        compiler_params=_params(1, vmem),
        name="sample_attn",
    )(q, k_new, v_new, cache_k, cache_v)


def _memory_kv_kernel(mem_ref, norm_ref, wk_ref, wv_ref, kn_ref, k_ref, v_ref):
    m = _rms(mem_ref[...], norm_ref[...]).astype(BF16)
    n_mem = m.shape[0]
    k = _dot(m, wk_ref[...])
    v = _dot(m, wv_ref[...])
    heads = k.shape[1] // MEM_HEAD_DIM
    for hh in range(heads):
        cols = slice(hh * MEM_HEAD_DIM, (hh + 1) * MEM_HEAD_DIM)
        k_ref[pl.ds(hh, n_mem, stride=heads), :] = _rms(k[:, cols], kn_ref[...])
        v_ref[pl.ds(hh, n_mem, stride=heads), :] = v[:, cols]


def _memory_kv(mem, norm_src, w_k, w_v, k_norm):
    n_mem = mem.shape[0]
    heads = w_k.shape[1] // MEM_HEAD_DIM
    args = (mem, norm_src, w_k, w_v, k_norm)
    out = jax.ShapeDtypeStruct((n_mem * heads, MEM_HEAD_DIM), F32)
    return pl.pallas_call(
        _memory_kv_kernel,
        grid=(1,),
        in_specs=[_const_spec(a.shape) for a in args],
        out_specs=[pl.BlockSpec((n_mem * heads, MEM_HEAD_DIM), lambda i: (0, 0))] * 2,
        out_shape=[out, out],
        compiler_params=_params(1, 32 << 20),
        name="memory_kv",
    )(*args)


def _mem_mix_kernel(x_ref, pool_ref, attn_ref, wout_ref, nmem_ref, wq_ref, qn_ref, mk_ref, mv_ref, wo_ref, nffn_ref,
                    x2_ref, h3_ref, *, nb, n_mem):
    tm = x_ref.shape[0]
    pool_width = pool_ref.shape[1]
    x1 = x_ref[...] + _dot(pool_ref[...], wout_ref[0:pool_width, :]) + _dot(attn_ref[...], wout_ref[pool_width:, :])
    h2 = _rms(x1, nmem_ref[...]).astype(BF16)
    qm = _dot(h2, wq_ref[...])
    scale = MEM_HEAD_DIM ** -0.5
    n_heads = qm.shape[1] // MEM_HEAD_DIM
    n_keys = mk_ref.shape[0] // n_heads
    if nb > 1:
        seq_q = (pl.program_id(0) * tm + lax.broadcasted_iota(jnp.int32, (tm, n_keys), 0)) & (nb - 1)
        seq_k = lax.broadcasted_iota(jnp.int32, (tm, n_keys), 1) >> int(math.log2(n_mem))
        own = seq_q == seq_k
    heads = []
    for hh in range(n_heads):
        cols = slice(hh * MEM_HEAD_DIM, (hh + 1) * MEM_HEAD_DIM)
        head_rows = pl.ds(hh, n_keys, stride=n_heads)
        q = _rms(qm[:, cols], qn_ref[...]).astype(BF16)
        s = _dot_nt(q, mk_ref[head_rows, :].astype(BF16)) * scale
        if nb > 1:
            s = jnp.where(own, s, NEG_INF)
        p = jnp.exp(s - jnp.max(s, axis=-1, keepdims=True))
        o = _dot(p.astype(BF16), mv_ref[head_rows, :].astype(BF16)) / jnp.sum(p, axis=-1, keepdims=True)
        heads.append(o.astype(BF16))
    x2 = x1 + _dot(jnp.concatenate(heads, axis=-1), wo_ref[...])
    x2_ref[...] = x2
    h3_ref[...] = _rms(x2, nffn_ref[...]).astype(h3_ref.dtype)


def _mem_mix(x, pool_out, attn_out, w_out, norm_mem, w_q, q_norm, mem_k, mem_v, w_o, norm_ffn, *, tm, nb, n_mem):
    rows, d_model = x.shape
    n_keys = mem_k.shape[0] * MEM_HEAD_DIM // w_q.shape[1]
    assert rows % tm == 0 and nb & (nb - 1) == 0 and n_mem & (n_mem - 1) == 0 and n_keys == nb * n_mem
    tile = lambda a: pl.BlockSpec((tm, a.shape[1]), lambda i: (i, 0))
    consts = (w_out, norm_mem, w_q, q_norm, mem_k, mem_v, w_o, norm_ffn)
    vmem = (sum(c.size * c.dtype.itemsize for c in consts) + 2 * tm * d_model * (4 + 2 + 4 + 2)
            + tm * d_model * 16 + 4 * tm * n_keys * 4 + (6 << 20))
    return pl.pallas_call(
        functools.partial(_mem_mix_kernel, nb=nb, n_mem=n_mem),
        grid=(rows // tm,),
        in_specs=[tile(x), tile(pool_out), tile(attn_out)] + [_const_spec(c.shape) for c in consts],
        out_specs=[pl.BlockSpec((tm, d_model), lambda i: (i, 0))] * 2,
        out_shape=[jax.ShapeDtypeStruct((rows, d_model), F32), jax.ShapeDtypeStruct((rows, d_model), BF16)],
        compiler_params=_params(1, vmem),
        name="mem_mix",
    )(x, pool_out, attn_out, *consts)


def _conv_ffn_kernel(h_ref, x_ref, wg_ref, wu_ref, cw_ref, cb_ref, wd_ref, prev_ref, y_ref, state_ref,
                     g_ext, g_scr, *, nb):
    i = pl.program_id(0)
    j = pl.program_id(1)
    tm = h_ref.shape[0]
    tail = CONV_TAIL * nb

    @pl.when(j == 0)
    def _():
        y_ref[...] = x_ref[...]

    @pl.when(i == 0)
    def _():
        g_ext[0:tail, :] = prev_ref[...]

    @pl.when(i > 0)
    def _():
        g_ext[0:tail, :] = g_scr[j]

    g_ext[tail:tail + tm, :] = _dot(h_ref[...], wg_ref[...])
    up = _dot(h_ref[...], wu_ref[...])
    c = cb_ref[...]
    for tap in range(CONV_WIDTH):
        c = c + g_ext[tap * nb:tap * nb + tm, :] * cw_ref[tap:tap + 1, :]
    act = (c * jax.nn.sigmoid(c) * up).astype(BF16)
    y_ref[...] += _dot(act, wd_ref[...])

    last = g_ext[tm:tm + tail, :]
    g_scr[j] = last
    state_ref[:, pl.ds(pl.multiple_of(j * last.shape[1], LANES), last.shape[1])] = last


def _conv_ffn(h3, x2, w_gate, w_up, conv_w, conv_b, w_down, conv_prev, *, tm, tf, nb):
    rows, d_model = x2.shape
    d_ff = w_gate.shape[1]
    assert rows % tm == 0 and d_ff % tf == 0 and tm >= CONV_TAIL * nb
    tail = CONV_TAIL * nb
    row_tile = pl.BlockSpec((tm, d_model), lambda i, j: (i, 0))
    chunk = lambda n: pl.BlockSpec((n, tf), lambda i, j: (0, j))
    vmem = (2 * tm * d_model * (2 + 4 + 4) + 2 * 3 * d_model * tf * 2 + (tm + 3 * tail) * tf * 4
            + d_ff // tf * tail * tf * 4 + 5 * tm * tf * 4 + tm * d_model * 4 + (6 << 20))
    return pl.pallas_call(
        functools.partial(_conv_ffn_kernel, nb=nb),
        grid=(rows // tm, d_ff // tf),
        in_specs=[row_tile, row_tile, chunk(d_model), chunk(d_model), chunk(CONV_WIDTH), chunk(1),
                  pl.BlockSpec((tf, d_model), lambda i, j: (j, 0)), chunk(tail)],
        out_specs=[row_tile, pl.BlockSpec((tail, d_ff), lambda i, j: (0, 0))],
        out_shape=[jax.ShapeDtypeStruct((rows, d_model), F32), jax.ShapeDtypeStruct((tail, d_ff), F32)],
        scratch_shapes=[pltpu.VMEM((tail + tm, tf), F32), pltpu.VMEM((d_ff // tf, tail, tf), F32)],
        compiler_params=_params(2, vmem),
        name="conv_ffn",
    )(h3, x2, w_gate, w_up, conv_w, conv_b, w_down, conv_prev)


def _rope_tables(pos):
    half = HEAD_DIM // 2
    inv = 1.0 / (ROPE_THETA ** (jnp.arange(half, dtype=F32) * (2.0 / HEAD_DIM)))
    ang = pos.astype(F32)[:, None] * inv[None, :]
    cos, sin = jnp.cos(ang), jnp.sin(ang)
    return jnp.concatenate([cos, cos], axis=-1), jnp.concatenate([-sin, sin], axis=-1)


def _row_tile(rows, want):
    return want if rows % want == 0 else rows


def _layer(x, pos0, nb, pool_prev, conv_prev, mem_k, mem_v, n_mem, attend, p):
    rows = x.shape[0]
    cos2, sin2 = _rope_tables(pos0 + jnp.arange(rows) // nb)
    pool_out, q_hm, k_hm, v_hm, pool_state = _in_proj(
        x, cos2, sin2, pool_prev, p["norm_mix"], p["w_in"], p["q_norm"], p["k_norm"], p["w_pool"], p["pool_scale"],
        tm=_row_tile(rows, IN_PROJ_ROWS), nb=nb, pos0=pos0)
    attn_out, k_keep, v_keep = attend(q_hm, k_hm, v_hm)
    x2, h3 = _mem_mix(x, pool_out, attn_out, p["w_out"], p["norm_mem"], p["w_mem_q"], p["mem_q_norm"], mem_k, mem_v,
                      p["w_mem_o"], p["norm_ffn"], tm=_row_tile(rows, MEM_MIX_ROWS), nb=nb, n_mem=n_mem)
    y, conv_state = _conv_ffn(h3, x2, p["w_gate"], p["w_up"], p["conv_w"], p["conv_b"], p["w_down"], conv_prev,
                              tm=_row_tile(rows, FFN_ROWS), tf=FFN_COLS, nb=nb)
    return y, pool_state, k_keep, v_keep, conv_state


def kernel(x_prompt, x_sample, state_pool, cache_win_k, cache_win_v, cache_mem_k, cache_mem_v, state_conv, mem_prompt, norm_mix, w_in, q_norm, k_norm, w_pool, pool_scale, w_out, norm_mem, norm_mem_src, w_mem_q, w_mem_k, w_mem_v, mem_q_norm, mem_k_norm, w_mem_o, norm_ffn, w_gate, w_up, conv_w, conv_b, w_down):
    bp, tp, d_model = x_prompt.shape
    bs, ts, _ = x_sample.shape
    depth = w_in.shape[0]
    assert bp == 1
    pool_width = pool_scale.shape[-1]
    d_ff = w_gate.shape[-1]
    n_mem = mem_prompt.shape[1]
    n_heads = cache_win_k.shape[3]
    mem_heads = cache_mem_k.shape[3]
    attn_width = n_heads * HEAD_DIM
    keep_p = min(MAX_WINDOW, tp)
    assert keep_p == SUPER_BLOCK
    row = lambda a: a.reshape(1, -1)

    xp = x_prompt.reshape(tp, d_model)
    xs = x_sample.transpose(1, 0, 2).reshape(ts * bs, d_model)
    outs = {k: [] for k in ("p_pool", "p_k", "p_v", "p_mk", "p_mv", "p_conv", "s_pool", "s_k", "s_v", "s_conv")}
    for l in range(depth):
        prm = {
            "norm_mix": row(norm_mix[l]), "w_in": w_in[l].astype(BF16), "q_norm": row(q_norm[l]),
            "k_norm": row(k_norm[l]), "w_pool": w_pool[l].astype(BF16), "pool_scale": row(pool_scale[l]),
            "w_out": w_out[l].astype(BF16), "norm_mem": row(norm_mem[l]), "w_mem_q": w_mem_q[l].astype(BF16),
            "mem_q_norm": row(mem_q_norm[l]), "w_mem_o": w_mem_o[l].astype(BF16), "norm_ffn": row(norm_ffn[l]),
            "w_gate": w_gate[l].astype(BF16), "w_up": w_up[l].astype(BF16), "conv_w": conv_w[l],
            "conv_b": row(conv_b[l]), "w_down": w_down[l].astype(BF16),
        }
        mk, mv = _memory_kv(mem_prompt[0], row(norm_mem_src[l]), w_mem_k[l].astype(BF16), w_mem_v[l].astype(BF16),
                            row(mem_k_norm[l]))
        xp, pool_st, kp, vp, conv_st = _layer(
            xp, 0, 1, jnp.zeros((POOL_TAIL, pool_width), F32), jnp.zeros((CONV_TAIL, d_ff), F32), mk, mv, n_mem,
            _dilated_attn, prm)
        outs["p_pool"].append(pool_st[None])
        outs["p_k"].append(kp.reshape(1, keep_p, n_heads, HEAD_DIM))
        outs["p_v"].append(vp.reshape(1, keep_p, n_heads, HEAD_DIM))
        outs["p_mk"].append(mk.reshape(1, n_mem, mem_heads, MEM_HEAD_DIM))
        outs["p_mv"].append(mv.reshape(1, n_mem, mem_heads, MEM_HEAD_DIM))
        outs["p_conv"].append(conv_st[None])

        pool_prev = jnp.pad(state_pool[l].transpose(1, 0, 2), ((POOL_TAIL - POOL_STATE, 0), (0, 0), (0, 0)))
        conv_prev = state_conv[l].transpose(1, 0, 2).reshape(CONV_TAIL * bs, d_ff)
        cache = cache_win_k.shape[2]

        def attend_samples(q_hm, k_hm, v_hm, l=l, cache=cache):
            pad = lambda a: jnp.pad(a, ((0, 0), (0, -ts % SUBLANES), (0, 0)))
            to_seq = lambda a: a.reshape(n_heads, ts, bs, HEAD_DIM).transpose(2, 1, 0, 3).reshape(bs, ts, attn_width)
            k_new, v_new = to_seq(k_hm), to_seq(v_hm)
            o = _sample_attn(pad(to_seq(q_hm)), pad(k_new), pad(v_new),
                             cache_win_k[l].reshape(bs, cache * n_heads, HEAD_DIM),
                             cache_win_v[l].reshape(bs, cache * n_heads, HEAD_DIM), past_len=PAST_LEN)
            return o[:, :ts].transpose(1, 0, 2).reshape(ts * bs, attn_width), k_new, v_new

        xs, pool_st, ks, vs, conv_st = _layer(
            xs, PAST_LEN, bs, pool_prev.reshape(POOL_TAIL * bs, pool_width), conv_prev,
            cache_mem_k[l].reshape(bs * n_mem * mem_heads, MEM_HEAD_DIM),
            cache_mem_v[l].reshape(bs * n_mem * mem_heads, MEM_HEAD_DIM), n_mem, attend_samples, prm)
        seq_major = lambda a, t: a.reshape(t, bs, -1).transpose(1, 0, 2)
        outs["s_pool"].append(seq_major(pool_st, POOL_STATE))
        outs["s_k"].append(ks.reshape(bs, ts, n_heads, HEAD_DIM))
        outs["s_v"].append(vs.reshape(bs, ts, n_heads, HEAD_DIM))
        outs["s_conv"].append(seq_major(conv_st, CONV_TAIL))
    stack = lambda k: jnp.stack(outs[k])
    return (xp.reshape(bp, tp, d_model), xs.reshape(ts, bs, d_model).transpose(1, 0, 2),
            stack("p_pool"), stack("p_k"), stack("p_v"), stack("p_mk"), stack("p_mv"), stack("p_conv"),
            stack("s_pool"), stack("s_k"), stack("s_v"), stack("s_conv"))
```

```python
import functools
import math

import jax
import jax.numpy as jnp
from jax import lax
from jax.experimental import pallas as pl
from jax.experimental.pallas import tpu as pltpu

PAST_LEN = 16384
POOL_WINDOWS = (2, 4, 8, 16)
HEAD_DIM = 128
DILATIONS = ((128, 1), (512, 4), (2048, 16))
MAX_WINDOW = 2048
ROPE_THETA = 10000.0
MEM_HEAD_DIM = 128
CONV_WIDTH = 3
EPS = 1e-6
NEG_INF = -1e30

POOL_STATE = max(POOL_WINDOWS) - 1
POOL_TAIL = max(POOL_WINDOWS)
CONV_TAIL = CONV_WIDTH - 1
BAND = DILATIONS[0][0]
SUPER_BLOCK = MAX_WINDOW

V7X_SCOPED_VMEM_LIMIT = 60000 * 1024
LANES = 128
SUBLANES = 8

IN_PROJ_ROWS = 512
MEM_MIX_ROWS = 512
FFN_ROWS = 512
FFN_COLS = 512

F32 = jnp.float32
BF16 = jnp.bfloat16


def _const_spec(shape):
    return pl.BlockSpec(shape, lambda *_: (0,) * len(shape), pipeline_mode=pl.Buffered(1))


def _params(n_axes, vmem_bytes):
    return pltpu.CompilerParams(
        dimension_semantics=("arbitrary",) * n_axes,
        vmem_limit_bytes=min(int(vmem_bytes), V7X_SCOPED_VMEM_LIMIT),
    )


def _rms(x, gain):
    return x * lax.rsqrt(jnp.mean(x * x, axis=-1, keepdims=True) + EPS) * gain


def _dot(a, b):
    return jnp.dot(a, b, preferred_element_type=F32)


def _dot_nt(a, b):
    return lax.dot_general(a, b, (((1,), (1,)), ((), ())), preferred_element_type=F32)


def _in_proj_kernel(x_ref, cos_ref, sin_ref, prev_ref, norm_ref, win_ref, qn_ref, kn_ref, wpool_ref, pscale_ref,
                    pool_ref, q_ref, k_ref, v_ref, pstate_ref, u_scr,
                    *, tm, nb, pos0, n_tiles, n_heads):
    i = pl.program_id(0)
    tail = POOL_TAIL * nb
    pool_width = pool_ref.shape[1]
    attn_width = n_heads * HEAD_DIM
    group = pool_width // len(POOL_WINDOWS)

    h = _rms(x_ref[...], norm_ref[...]).astype(BF16)

    @pl.when(i == 0)
    def _():
        u_scr[0:tail, :] = prev_ref[...]

    u_scr[tail:tail + tm, :] = _dot(h, win_ref[:, 0:pool_width])
    row = i * tm + lax.broadcasted_iota(jnp.int32, (tm, 1), 0)
    pos = pos0 + (row >> int(math.log2(nb)))
    for g, w in enumerate(POOL_WINDOWS):
        cols = slice(g * group, (g + 1) * group)
        u_t = u_scr[tail:tail + tm, cols]
        acc = u_t
        for j in range(1, w):
            acc = acc + u_scr[tail - j * nb:tail - j * nb + tm, cols]
        cnt = jnp.minimum(pos + 1, w).astype(F32)
        diff = acc / cnt - u_t
        y = _dot(diff.astype(BF16), wpool_ref[g]) * pscale_ref[:, cols]
        pool_ref[:, cols] = y.astype(pool_ref.dtype)

    @pl.when(i == n_tiles - 1)
    def _():
        pstate_ref[...] = u_scr[tm + nb:tm + tail, :]

    if n_tiles > 1:
        u_scr[0:tail, :] = u_scr[tm:tm + tail, :]

    cosv = cos_ref[...]
    sinv = sin_ref[...]

    def normed_rope(col0, gain_ref, hm_ref):
        proj = _dot(h, win_ref[:, col0:col0 + attn_width])
        for hh in range(n_heads):
            xn = _rms(proj[:, hh * HEAD_DIM:(hh + 1) * HEAD_DIM], gain_ref[...])
            hm_ref[hh] = xn * cosv + pltpu.roll(xn, HEAD_DIM // 2, 1) * sinv

    normed_rope(pool_width, qn_ref, q_ref)
    normed_rope(pool_width + attn_width, kn_ref, k_ref)
    vproj = _dot(h, win_ref[:, pool_width + 2 * attn_width:pool_width + 3 * attn_width])
    for hh in range(n_heads):
        v_ref[hh] = vproj[:, hh * HEAD_DIM:(hh + 1) * HEAD_DIM]


def _in_proj(x, cos2, sin2, pool_prev, norm, w_in, q_norm, k_norm, w_pool, pool_scale, *, tm, nb, pos0):
    rows, d_model = x.shape
    pool_width = pool_scale.shape[-1]
    attn_width = (w_in.shape[1] - pool_width) // 3
    n_heads = attn_width // HEAD_DIM
    n_tiles = rows // tm
    assert rows % tm == 0 and nb & (nb - 1) == 0 and (tm >= POOL_TAIL * nb or n_tiles == 1)
    tile = lambda width: pl.BlockSpec((tm, width), lambda i: (i, 0))
    heads = pl.BlockSpec((n_heads, tm, HEAD_DIM), lambda i: (0, i, 0))
    vmem = (2 * tm * d_model * 4 + w_in.size * 2 + 2 * 2 * tm * LANES * 4 + 2 * tm * pool_width * 2
            + 3 * 2 * tm * attn_width * 4 + (POOL_TAIL * nb + tm) * pool_width * 4
            + 2 * POOL_TAIL * nb * pool_width * 4 + 6 * tm * attn_width * 4 + tm * d_model * 6 + (4 << 20))
    hm_shape = jax.ShapeDtypeStruct((n_heads, rows, HEAD_DIM), F32)
    return pl.pallas_call(
        functools.partial(_in_proj_kernel, tm=tm, nb=nb, pos0=pos0, n_tiles=n_tiles, n_heads=n_heads),
        grid=(n_tiles,),
        in_specs=[tile(d_model), tile(HEAD_DIM), tile(HEAD_DIM), _const_spec(pool_prev.shape),
                  _const_spec(norm.shape), _const_spec(w_in.shape), _const_spec(q_norm.shape),
                  _const_spec(k_norm.shape), _const_spec(w_pool.shape), _const_spec(pool_scale.shape)],
        out_specs=[tile(pool_width), heads, heads, heads,
                   pl.BlockSpec((POOL_STATE * nb, pool_width), lambda i: (0, 0))],
        out_shape=[jax.ShapeDtypeStruct((rows, pool_width), BF16), hm_shape, hm_shape, hm_shape,
                   jax.ShapeDtypeStruct((POOL_STATE * nb, pool_width), F32)],
        scratch_shapes=[pltpu.VMEM((POOL_TAIL * nb + tm, pool_width), F32)],
        compiler_params=_params(1, vmem),
        name="in_proj",
    )(x, cos2, sin2, pool_prev, norm, w_in, q_norm, k_norm, w_pool, pool_scale)


CLASSES = 4
LOG2E = 1.4426950408889634


def _dilated_attn_kernel(q_ref, k_ref, v_ref, o_ref, klast_ref, vlast_ref,
                         yq, yk, yv, q1, k1, v1, q4, k4, v4, q16, k16, v16,
                         o1, m1, l1, o4, m4, l4, o16, m16, l16, out_f, bias):
    i = pl.program_id(1)
    sb = q_ref.shape[0]

    @pl.when(i == pl.num_programs(1) - 1)
    def _():
        klast_ref[...] = k_ref[...]
        vlast_ref[...] = v_ref[...]

    n4 = sb // CLASSES
    nk4 = BAND + n4
    nk16 = 2 * BAND
    slots = sb // BAND
    scale = HEAD_DIM ** -0.5 * LOG2E
    keys = ((k1, v1), (k4, v4), (k16, v16))
    prev_rows = ([(0, sb)], [(r * nk4, r * nk4 + n4) for r in range(CLASSES)],
                 [(s * nk16, s * nk16 + BAND) for s in range(slots)])
    val = slice(0, HEAD_DIM)

    @pl.when(i == 0)
    def _():
        for (kd, vd), rows in zip(keys, prev_rows):
            vd[:, HEAD_DIM:] = jnp.ones((vd.shape[0], HEAD_DIM), BF16)
            for dst, _ in rows:
                kd[dst:dst + BAND, :] = jnp.zeros((BAND, HEAD_DIM), BF16)
                vd[dst:dst + BAND, val] = jnp.zeros((BAND, HEAD_DIM), BF16)
        qa = lax.broadcasted_iota(jnp.int32, (BAND, 2 * BAND), 0)
        col = lax.broadcasted_iota(jnp.int32, (BAND, 2 * BAND), 1)
        in_band = (col >= qa) & (col <= qa + BAND)
        bias[0] = jnp.where(in_band, 0.0, NEG_INF)
        bias[1] = jnp.where(in_band & (col >= BAND), 0.0, NEG_INF)

    @pl.when(i > 0)
    def _():
        for (kd, vd), rows in zip(keys, prev_rows):
            for dst, src in rows:
                kd[dst:dst + BAND, :] = kd[src:src + BAND, :]
                vd[dst:dst + BAND, val] = vd[src:src + BAND, val]

    q1[...] = q_ref[...].astype(BF16)
    k1[BAND:BAND + sb, :] = k_ref[...].astype(BF16)
    v1[BAND:BAND + sb, val] = v_ref[...].astype(BF16)
    for src, y, d4, d16, lead, lanes in ((q_ref, yq, q4, q16, 0, slice(None)), (k_ref, yk, k4, k16, BAND, slice(None)),
                                        (v_ref, yv, v4, v16, BAND, val)):
        for r1 in range(CLASSES):
            rows = src[pl.ds(r1, n4, stride=CLASSES), :]
            y[r1 * n4:(r1 + 1) * n4, :] = rows
            d4[r1 * (lead + n4) + lead:(r1 + 1) * (lead + n4), lanes] = rows.astype(BF16)
        for s in range(slots):
            r1, r2 = divmod(s, CLASSES)
            rows = y[pl.ds(r1 * n4 + r2, BAND, stride=CLASSES), :]
            d16[s * (lead + BAND) + lead:(s + 1) * (lead + BAND), lanes] = rows.astype(BF16)

    first = (i == 0).astype(jnp.int32)

    def branch(qd, kd, vd, units, od, md, ld, rows_of):
        for u, (q0, k0, leads) in enumerate(units):
            s = _dot_nt(qd[q0:q0 + BAND, :], kd[k0:k0 + 2 * BAND, :]) * scale + bias[first if leads else 0]
            m = jnp.max(s, axis=-1, keepdims=True)
            ol = _dot(jnp.exp2(s - m).astype(BF16), vd[k0:k0 + 2 * BAND, :])
            od[rows_of(u), :] = ol[:, val]
            md[rows_of(u), :] = jnp.broadcast_to(m, (BAND, HEAD_DIM))
            ld[rows_of(u), :] = ol[:, HEAD_DIM:]

    blocks4 = n4 // BAND
    in_order = lambda u: slice(u * BAND, (u + 1) * BAND)
    branch(q1, k1, v1, [(u * BAND, u * BAND, u == 0) for u in range(slots)], o1, m1, l1, in_order)
    branch(q4, k4, v4, [(r * n4 + b * BAND, r * nk4 + b * BAND, b == 0)
                        for r in range(CLASSES) for b in range(blocks4)], o4, m4, l4, in_order)
    branch(q16, k16, v16, [(s * BAND, s * nk16, True) for s in range(slots)], o16, m16, l16,
           lambda s: pl.ds((s // CLASSES) * n4 + s % CLASSES, BAND, stride=CLASSES))

    for piece in range(slots):
        r1, j = divmod(piece, blocks4)
        rows = in_order(piece)
        nat = pl.ds(r1 + CLASSES * BAND * j, BAND, stride=CLASSES)
        ma, mb, mc = m1[nat, :], m4[rows, :], m16[rows, :]
        m = jnp.maximum(jnp.maximum(ma, mb), mc)
        wa, wb, wc = jnp.exp2(ma - m), jnp.exp2(mb - m), jnp.exp2(mc - m)
        num = wa * o1[nat, :] + wb * o4[rows, :] + wc * o16[rows, :]
        den = wa * l1[nat, :] + wb * l4[rows, :] + wc * l16[rows, :]
        out_f[nat, :] = num / den
    o_ref[...] = out_f[...].astype(o_ref.dtype)


def _dilated_attn(q_hm, k_hm, v_hm):
    n_heads, rows, _ = q_hm.shape
    sb = SUPER_BLOCK
    assert rows % sb == 0 and [d for _, d in DILATIONS] == [1, CLASSES, CLASSES ** 2]
    assert all(w == BAND * d for w, d in DILATIONS) and sb == BAND * CLASSES ** 2
    block = pl.BlockSpec((None, sb, HEAD_DIM), lambda h, i: (h, i, 0))
    blk = lambda n, dt, width=HEAD_DIM: pltpu.VMEM((n, width), dt)
    lead = (BAND, CLASSES * BAND, sb)
    scratch = ([blk(sb, F32)] * 3
               + [s for n in lead for s in (blk(sb, BF16), blk(n + sb, BF16), blk(n + sb, BF16, 2 * HEAD_DIM))]
               + [blk(sb, F32)] * 10
               + [pltpu.VMEM((2, BAND, 2 * BAND), F32)])
    vmem = (2 * 3 * sb * HEAD_DIM * 4 + 2 * sb * HEAD_DIM * 2 + 2 * 2 * sb * HEAD_DIM * 4 + 13 * sb * HEAD_DIM * 4
            + (12 * sb + 3 * sum(lead)) * HEAD_DIM * 2 + 2 * BAND * 2 * BAND * 4 + (12 << 20))
    last = pl.BlockSpec((sb, HEAD_DIM), lambda h, i: (0, h))
    last_shape = jax.ShapeDtypeStruct((sb, n_heads * HEAD_DIM), F32)
    return pl.pallas_call(
        _dilated_attn_kernel,
        grid=(n_heads, rows // sb),
        in_specs=[block, block, block],
        out_specs=[pl.BlockSpec((sb, HEAD_DIM), lambda h, i: (i, h)), last, last],
        out_shape=[jax.ShapeDtypeStruct((rows, n_heads * HEAD_DIM), BF16), last_shape, last_shape],
        scratch_shapes=scratch,
        compiler_params=_params(2, vmem),
        name="dilated_attn",
    )(q_hm, k_hm, v_hm)


FAR_STEP = DILATIONS[2][1]
NEAR_ROWS = DILATIONS[1][0]


def _sample_attn_kernel(q_ref, kn_ref, vn_ref, *refs, n_heads, past_len, cache, n_far):
    far_k, far_v = refs[0:n_far], refs[n_far:2 * n_far]
    near_k, near_v, o_ref = refs[2 * n_far:]
    tq = q_ref.shape[0]
    far_m = cache // FAR_STEP
    scale = HEAD_DIM ** -0.5

    def multiplicity(key, ok):
        t = lax.broadcasted_iota(jnp.int32, key.shape, 0)
        dist = cache + t - key
        seen = ok & (dist >= 0) & (past_len + t - dist >= 0)
        n = jnp.zeros(key.shape, F32)
        for window, dil in DILATIONS:
            hit = seen & (dist <= window) & ((dist & (dil - 1)) == 0)
            n = n + jnp.where(hit, 1.0, 0.0)
        return n

    lane = lax.broadcasted_iota(jnp.int32, (tq, far_m), 1)
    near_key = cache - NEAR_ROWS + lax.broadcasted_iota(jnp.int32, (tq, NEAR_ROWS), 1)
    old_key = jnp.concatenate([FAR_STEP * lane + r for r in range(n_far)] + [near_key], axis=1)
    far_col = lax.broadcasted_iota(jnp.int32, old_key.shape, 1) < n_far * far_m
    n_old = multiplicity(old_key, ~(far_col & (old_key >= cache - NEAR_ROWS)))
    n_new = multiplicity(cache + lax.broadcasted_iota(jnp.int32, (tq, tq), 1), True)

    def head_rows(far, near, hh):
        parts = [f[:, hh, :] for f in far]
        parts.append(near[pl.ds(hh, NEAR_ROWS, stride=n_heads), :])
        return jnp.concatenate(parts, axis=0).astype(BF16)

    for hh in range(n_heads):
        cols = slice(hh * HEAD_DIM, (hh + 1) * HEAD_DIM)
        q = q_ref[:, cols].astype(BF16)
        s_o = jnp.where(n_old > 0, _dot_nt(q, head_rows(far_k, near_k, hh)) * scale, NEG_INF)
        s_n = jnp.where(n_new > 0, _dot_nt(q, kn_ref[:, cols].astype(BF16)) * scale, NEG_INF)
        m = jnp.maximum(jnp.max(s_o, axis=-1, keepdims=True), jnp.max(s_n, axis=-1, keepdims=True))
        p_o = n_old * jnp.exp(s_o - m)
        p_n = n_new * jnp.exp(s_n - m)
        den = jnp.sum(p_o, axis=-1, keepdims=True) + jnp.sum(p_n, axis=-1, keepdims=True)
        num = (_dot(p_o.astype(BF16), head_rows(far_v, near_v, hh))
               + _dot(p_n.astype(BF16), vn_ref[:, cols].astype(BF16)))
        o_ref[:, cols] = (num / den).astype(o_ref.dtype)


def _sample_attn(q, k_new, v_new, cache_k, cache_v, *, n_queries, past_len):
    batch, tq, width = q.shape
    _, cache, n_heads, _ = cache_k.shape
    n_far = min(n_queries, FAR_STEP)
    assert all(d & (d - 1) == 0 for _, d in DILATIONS) and cache % NEAR_ROWS == 0 and cache % FAR_STEP == 0
    assert DILATIONS[0][0] <= NEAR_ROWS and tq % SUBLANES == 0
    new = pl.BlockSpec((None, tq, width), lambda b: (b, 0, 0))
    far_view = lambda c: c.reshape(batch, cache // FAR_STEP, FAR_STEP, n_heads, HEAD_DIM)
    near_view = lambda c: c.reshape(batch, cache * n_heads, HEAD_DIM)
    far = [pl.BlockSpec((None, cache // FAR_STEP, None, n_heads, HEAD_DIM),
                        functools.partial(lambda b, r: (b, 0, r, 0, 0), r=r)) for r in range(n_far)]
    near = pl.BlockSpec((None, NEAR_ROWS * n_heads, HEAD_DIM), lambda b: (b, cache // NEAR_ROWS - 1, 0))
    vmem = 2 * 2 * (n_far * cache // FAR_STEP + NEAR_ROWS) * width * 4 + (16 << 20)
    return pl.pallas_call(
        functools.partial(_sample_attn_kernel, n_heads=n_heads, past_len=past_len, cache=cache, n_far=n_far),
        grid=(batch,),
        in_specs=[new, new, new] + far + far + [near, near],
        out_specs=new,
        out_shape=jax.ShapeDtypeStruct((batch, tq, width), BF16),
        compiler_params=_params(1, vmem),
        name="sample_attn",
    )(q, k_new, v_new, *([far_view(cache_k)] * n_far), *([far_view(cache_v)] * n_far),
      near_view(cache_k), near_view(cache_v))


def _memory_kv_kernel(mem_ref, norm_ref, wk_ref, wv_ref, kn_ref, k_ref, v_ref):
    m = _rms(mem_ref[...], norm_ref[...]).astype(BF16)
    n_mem = m.shape[0]
    k = _dot(m, wk_ref[...])
    v = _dot(m, wv_ref[...])
    heads = k.shape[1] // MEM_HEAD_DIM
    for hh in range(heads):
        cols = slice(hh * MEM_HEAD_DIM, (hh + 1) * MEM_HEAD_DIM)
        k_ref[pl.ds(hh, n_mem, stride=heads), :] = _rms(k[:, cols], kn_ref[...])
        v_ref[pl.ds(hh, n_mem, stride=heads), :] = v[:, cols]


def _memory_kv(mem, norm_src, w_k, w_v, k_norm):
    n_mem = mem.shape[0]
    heads = w_k.shape[1] // MEM_HEAD_DIM
    args = (mem, norm_src, w_k, w_v, k_norm)
    out = jax.ShapeDtypeStruct((n_mem * heads, MEM_HEAD_DIM), F32)
    return pl.pallas_call(
        _memory_kv_kernel,
        grid=(1,),
        in_specs=[_const_spec(a.shape) for a in args],
        out_specs=[pl.BlockSpec((n_mem * heads, MEM_HEAD_DIM), lambda i: (0, 0))] * 2,
        out_shape=[out, out],
        compiler_params=_params(1, 32 << 20),
        name="memory_kv",
    )(*args)


def _mem_mix_kernel(x_ref, pool_ref, attn_ref, wout_ref, nmem_ref, wq_ref, qn_ref, mk_ref, mv_ref, wo_ref, nffn_ref,
                    x2_ref, h3_ref, *, nb, n_mem):
    tm = x_ref.shape[0]
    pool_width = pool_ref.shape[1]
    x1 = x_ref[...] + _dot(pool_ref[...], wout_ref[0:pool_width, :]) + _dot(attn_ref[...], wout_ref[pool_width:, :])
    h2 = _rms(x1, nmem_ref[...]).astype(BF16)
    qm = _dot(h2, wq_ref[...])
    scale = MEM_HEAD_DIM ** -0.5
    n_heads = qm.shape[1] // MEM_HEAD_DIM
    n_keys = mk_ref.shape[0] // n_heads
    if nb > 1:
        seq_q = (pl.program_id(0) * tm + lax.broadcasted_iota(jnp.int32, (tm, n_keys), 0)) & (nb - 1)
        seq_k = lax.broadcasted_iota(jnp.int32, (tm, n_keys), 1) >> int(math.log2(n_mem))
        own = seq_q == seq_k
    heads = []
    for hh in range(n_heads):
        cols = slice(hh * MEM_HEAD_DIM, (hh + 1) * MEM_HEAD_DIM)
        head_rows = pl.ds(hh, n_keys, stride=n_heads)
        q = _rms(qm[:, cols], qn_ref[...]).astype(BF16)
        s = _dot_nt(q, mk_ref[head_rows, :].astype(BF16)) * scale
        if nb > 1:
            s = jnp.where(own, s, NEG_INF)
        p = jnp.exp(s - jnp.max(s, axis=-1, keepdims=True))
        o = _dot(p.astype(BF16), mv_ref[head_rows, :].astype(BF16)) / jnp.sum(p, axis=-1, keepdims=True)
        heads.append(o.astype(BF16))
    x2 = x1 + _dot(jnp.concatenate(heads, axis=-1), wo_ref[...])
    x2_ref[...] = x2
    h3_ref[...] = _rms(x2, nffn_ref[...]).astype(h3_ref.dtype)


def _mem_mix(x, pool_out, attn_out, w_out, norm_mem, w_q, q_norm, mem_k, mem_v, w_o, norm_ffn, *, tm, nb, n_mem):
    rows, d_model = x.shape
    n_keys = mem_k.shape[0] * MEM_HEAD_DIM // w_q.shape[1]
    assert rows % tm == 0 and nb & (nb - 1) == 0 and n_mem & (n_mem - 1) == 0 and n_keys == nb * n_mem
    tile = lambda a: pl.BlockSpec((tm, a.shape[1]), lambda i: (i, 0))
    consts = (w_out, norm_mem, w_q, q_norm, mem_k, mem_v, w_o, norm_ffn)
    vmem = (sum(c.size * c.dtype.itemsize for c in consts) + 2 * tm * d_model * (4 + 2 + 4 + 2)
            + tm * d_model * 16 + 4 * tm * n_keys * 4 + (6 << 20))
    return pl.pallas_call(
        functools.partial(_mem_mix_kernel, nb=nb, n_mem=n_mem),
        grid=(rows // tm,),
        in_specs=[tile(x), tile(pool_out), tile(attn_out)] + [_const_spec(c.shape) for c in consts],
        out_specs=[pl.BlockSpec((tm, d_model), lambda i: (i, 0))] * 2,
        out_shape=[jax.ShapeDtypeStruct((rows, d_model), F32), jax.ShapeDtypeStruct((rows, d_model), BF16)],
        compiler_params=_params(1, vmem),
        name="mem_mix",
    )(x, pool_out, attn_out, *consts)


def _conv_ffn_kernel(h_ref, x_ref, wg_ref, wu_ref, cw_ref, cb_ref, wd_ref, prev_ref, y_ref, state_ref,
                     g_ext, g_scr, *, nb):
    i = pl.program_id(0)
    j = pl.program_id(1)
    tm = h_ref.shape[0]
    tail = CONV_TAIL * nb

    @pl.when(j == 0)
    def _():
        y_ref[...] = x_ref[...]

    @pl.when(i == 0)
    def _():
        g_ext[0:tail, :] = prev_ref[...]

    @pl.when(i > 0)
    def _():
        g_ext[0:tail, :] = g_scr[j]

    g_ext[tail:tail + tm, :] = _dot(h_ref[...], wg_ref[...])
    up = _dot(h_ref[...], wu_ref[...])
    c = cb_ref[...]
    for tap in range(CONV_WIDTH):
        c = c + g_ext[tap * nb:tap * nb + tm, :] * cw_ref[tap:tap + 1, :]
    act = (c * jax.nn.sigmoid(c) * up).astype(BF16)
    y_ref[...] += _dot(act, wd_ref[...])

    last = g_ext[tm:tm + tail, :]
    g_scr[j] = last
    state_ref[:, pl.ds(pl.multiple_of(j * last.shape[1], LANES), last.shape[1])] = last


def _conv_ffn(h3, x2, w_gate, w_up, conv_w, conv_b, w_down, conv_prev, *, tm, tf, nb):
    rows, d_model = x2.shape
    d_ff = w_gate.shape[1]
    assert rows % tm == 0 and d_ff % tf == 0 and tm >= CONV_TAIL * nb
    tail = CONV_TAIL * nb
    row_tile = pl.BlockSpec((tm, d_model), lambda i, j: (i, 0))
    chunk = lambda n: pl.BlockSpec((n, tf), lambda i, j: (0, j))
    vmem = (2 * tm * d_model * (2 + 4 + 4) + 2 * 3 * d_model * tf * 2 + (tm + 3 * tail) * tf * 4
            + d_ff // tf * tail * tf * 4 + 5 * tm * tf * 4 + tm * d_model * 4 + (6 << 20))
    return pl.pallas_call(
        functools.partial(_conv_ffn_kernel, nb=nb),
        grid=(rows // tm, d_ff // tf),
        in_specs=[row_tile, row_tile, chunk(d_model), chunk(d_model), chunk(CONV_WIDTH), chunk(1),
                  pl.BlockSpec((tf, d_model), lambda i, j: (j, 0)), chunk(tail)],
        out_specs=[row_tile, pl.BlockSpec((tail, d_ff), lambda i, j: (0, 0))],
        out_shape=[jax.ShapeDtypeStruct((rows, d_model), F32), jax.ShapeDtypeStruct((tail, d_ff), F32)],
        scratch_shapes=[pltpu.VMEM((tail + tm, tf), F32), pltpu.VMEM((d_ff // tf, tail, tf), F32)],
        compiler_params=_params(2, vmem),
        name="conv_ffn",
    )(h3, x2, w_gate, w_up, conv_w, conv_b, w_down, conv_prev)


def _rope_tables(pos):
    half = HEAD_DIM // 2
    inv = 1.0 / (ROPE_THETA ** (jnp.arange(half, dtype=F32) * (2.0 / HEAD_DIM)))
    ang = pos.astype(F32)[:, None] * inv[None, :]
    cos, sin = jnp.cos(ang), jnp.sin(ang)
    return jnp.concatenate([cos, cos], axis=-1), jnp.concatenate([-sin, sin], axis=-1)


def _row_tile(rows, want):
    return want if rows % want == 0 else rows


def _layer(x, pos0, nb, pool_prev, conv_prev, mem_k, mem_v, n_mem, attend, p):
    rows = x.shape[0]
    cos2, sin2 = _rope_tables(pos0 + jnp.arange(rows) // nb)
    pool_out, q_hm, k_hm, v_hm, pool_state = _in_proj(
        x, cos2, sin2, pool_prev, p["norm_mix"], p["w_in"], p["q_norm"], p["k_norm"], p["w_pool"], p["pool_scale"],
        tm=_row_tile(rows, IN_PROJ_ROWS), nb=nb, pos0=pos0)
    attn_out, k_keep, v_keep = attend(q_hm, k_hm, v_hm)
    x2, h3 = _mem_mix(x, pool_out, attn_out, p["w_out"], p["norm_mem"], p["w_mem_q"], p["mem_q_norm"], mem_k, mem_v,
                      p["w_mem_o"], p["norm_ffn"], tm=_row_tile(rows, MEM_MIX_ROWS), nb=nb, n_mem=n_mem)
    y, conv_state = _conv_ffn(h3, x2, p["w_gate"], p["w_up"], p["conv_w"], p["conv_b"], p["w_down"], conv_prev,
                              tm=_row_tile(rows, FFN_ROWS), tf=FFN_COLS, nb=nb)
    return y, pool_state, k_keep, v_keep, conv_state


def kernel(x_prompt, x_sample, state_pool, cache_win_k, cache_win_v, cache_mem_k, cache_mem_v, state_conv, mem_prompt, norm_mix, w_in, q_norm, k_norm, w_pool, pool_scale, w_out, norm_mem, norm_mem_src, w_mem_q, w_mem_k, w_mem_v, mem_q_norm, mem_k_norm, w_mem_o, norm_ffn, w_gate, w_up, conv_w, conv_b, w_down):
    bp, tp, d_model = x_prompt.shape
    bs, ts, _ = x_sample.shape
    depth = w_in.shape[0]
    assert bp == 1
    pool_width = pool_scale.shape[-1]
    d_ff = w_gate.shape[-1]
    n_mem = mem_prompt.shape[1]
    n_heads = cache_win_k.shape[3]
    mem_heads = cache_mem_k.shape[3]
    attn_width = n_heads * HEAD_DIM
    keep_p = min(MAX_WINDOW, tp)
    assert keep_p == SUPER_BLOCK
    row = lambda a: a.reshape(1, -1)

    xp = x_prompt.reshape(tp, d_model)
    xs = x_sample.transpose(1, 0, 2).reshape(ts * bs, d_model)
    outs = {k: [] for k in ("p_pool", "p_k", "p_v", "p_mk", "p_mv", "p_conv", "s_pool", "s_k", "s_v", "s_conv")}
    for l in range(depth):
        prm = {
            "norm_mix": row(norm_mix[l]), "w_in": w_in[l].astype(BF16), "q_norm": row(q_norm[l]),
            "k_norm": row(k_norm[l]), "w_pool": w_pool[l].astype(BF16), "pool_scale": row(pool_scale[l]),
            "w_out": w_out[l].astype(BF16), "norm_mem": row(norm_mem[l]), "w_mem_q": w_mem_q[l].astype(BF16),
            "mem_q_norm": row(mem_q_norm[l]), "w_mem_o": w_mem_o[l].astype(BF16), "norm_ffn": row(norm_ffn[l]),
            "w_gate": w_gate[l].astype(BF16), "w_up": w_up[l].astype(BF16), "conv_w": conv_w[l],
            "conv_b": row(conv_b[l]), "w_down": w_down[l].astype(BF16),
        }
        mk, mv = _memory_kv(mem_prompt[0], row(norm_mem_src[l]), w_mem_k[l].astype(BF16), w_mem_v[l].astype(BF16),
                            row(mem_k_norm[l]))
        xp, pool_st, kp, vp, conv_st = _layer(
            xp, 0, 1, jnp.zeros((POOL_TAIL, pool_width), F32), jnp.zeros((CONV_TAIL, d_ff), F32), mk, mv, n_mem,
            _dilated_attn, prm)
        outs["p_pool"].append(pool_st[None])
        outs["p_k"].append(kp.reshape(1, keep_p, n_heads, HEAD_DIM))
        outs["p_v"].append(vp.reshape(1, keep_p, n_heads, HEAD_DIM))
        outs["p_mk"].append(mk.reshape(1, n_mem, mem_heads, MEM_HEAD_DIM))
        outs["p_mv"].append(mv.reshape(1, n_mem, mem_heads, MEM_HEAD_DIM))
        outs["p_conv"].append(conv_st[None])

        pool_prev = jnp.pad(state_pool[l].transpose(1, 0, 2), ((POOL_TAIL - POOL_STATE, 0), (0, 0), (0, 0)))
        conv_prev = state_conv[l].transpose(1, 0, 2).reshape(CONV_TAIL * bs, d_ff)

        def attend_samples(q_hm, k_hm, v_hm, l=l):
            pad = lambda a: jnp.pad(a, ((0, 0), (0, -ts % SUBLANES), (0, 0)))
            to_seq = lambda a: a.reshape(n_heads, ts, bs, HEAD_DIM).transpose(2, 1, 0, 3).reshape(bs, ts, attn_width)
            k_new, v_new = to_seq(k_hm), to_seq(v_hm)
            o = _sample_attn(pad(to_seq(q_hm)), pad(k_new), pad(v_new), cache_win_k[l], cache_win_v[l],
                             n_queries=ts, past_len=PAST_LEN)
            return o[:, :ts].transpose(1, 0, 2).reshape(ts * bs, attn_width), k_new, v_new

        xs, pool_st, ks, vs, conv_st = _layer(
            xs, PAST_LEN, bs, pool_prev.reshape(POOL_TAIL * bs, pool_width), conv_prev,
            cache_mem_k[l].reshape(bs * n_mem * mem_heads, MEM_HEAD_DIM),
            cache_mem_v[l].reshape(bs * n_mem * mem_heads, MEM_HEAD_DIM), n_mem, attend_samples, prm)
        seq_major = lambda a, t: a.reshape(t, bs, -1).transpose(1, 0, 2)
        outs["s_pool"].append(seq_major(pool_st, POOL_STATE))
        outs["s_k"].append(ks.reshape(bs, ts, n_heads, HEAD_DIM))
        outs["s_v"].append(vs.reshape(bs, ts, n_heads, HEAD_DIM))
        outs["s_conv"].append(seq_major(conv_st, CONV_TAIL))
    stack = lambda k: jnp.stack(outs[k])
    return (xp.reshape(bp, tp, d_model), xs.reshape(ts, bs, d_model).transpose(1, 0, 2),
            stack("p_pool"), stack("p_k"), stack("p_v"), stack("p_mk"), stack("p_mv"), stack("p_conv"),
            stack("s_pool"), stack("s_k"), stack("s_v"), stack("s_conv"))
```

```python
import functools
import math

import jax
import jax.numpy as jnp
from jax import lax
from jax.experimental import pallas as pl
from jax.experimental.pallas import tpu as pltpu

PAST_LEN = 16384
POOL_WINDOWS = (2, 4, 8, 16)
HEAD_DIM = 128
DILATIONS = ((128, 1), (512, 4), (2048, 16))
MAX_WINDOW = 2048
ROPE_THETA = 10000.0
MEM_HEAD_DIM = 128
CONV_WIDTH = 3
EPS = 1e-6
NEG_INF = -1e30

POOL_STATE = max(POOL_WINDOWS) - 1
POOL_TAIL = max(POOL_WINDOWS)
CONV_TAIL = CONV_WIDTH - 1
BAND = DILATIONS[0][0]
SUPER_BLOCK = MAX_WINDOW

V7X_SCOPED_VMEM_LIMIT = 60000 * 1024
LANES = 128
SUBLANES = 8

IN_PROJ_ROWS = 512
IN_PROJ_GROUP_ROWS = 256
MEM_MIX_ROWS = 512
FFN_ROWS = 512
FFN_COLS = 512

F32 = jnp.float32
BF16 = jnp.bfloat16


def _const_spec(shape):
    return pl.BlockSpec(shape, lambda *_: (0,) * len(shape), pipeline_mode=pl.Buffered(1))


def _params(n_axes, vmem_bytes):
    return pltpu.CompilerParams(
        dimension_semantics=("arbitrary",) * n_axes,
        vmem_limit_bytes=min(int(vmem_bytes), V7X_SCOPED_VMEM_LIMIT),
    )


def _rms(x, gain):
    return x * lax.rsqrt(jnp.mean(x * x, axis=-1, keepdims=True) + EPS) * gain


def _dot(a, b):
    return jnp.dot(a, b, preferred_element_type=F32)


def _dot_nt(a, b):
    return lax.dot_general(a, b, (((1,), (1,)), ((), ())), preferred_element_type=F32)


def _in_proj_kernel(x_ref, cos_ref, sin_ref, prev_ref, norm_ref, win_ref, qn_ref, kn_ref, wpool_ref, pscale_ref,
                    pool_ref, q_ref, k_ref, v_ref, pstate_ref, u_scr,
                    *, tm, nb, pos0, n_tiles, n_heads):
    i = pl.program_id(0)
    tail = POOL_TAIL * nb
    pool_width = pool_ref.shape[1]
    attn_width = n_heads * HEAD_DIM
    group = pool_width // len(POOL_WINDOWS)
    sr = min(tm, IN_PROJ_GROUP_ROWS)

    @pl.when(i == 0)
    def _():
        u_scr[0:tail, :] = prev_ref[...]

    for r0 in range(0, tm, sr):
        rows = slice(r0, r0 + sr)
        h = _rms(x_ref[rows, :], norm_ref[...]).astype(BF16)
        u_scr[tail + r0:tail + r0 + sr, :] = _dot(h, win_ref[:, 0:pool_width])
        row = i * tm + r0 + lax.broadcasted_iota(jnp.int32, (sr, 1), 0)
        pos = pos0 + (row >> int(math.log2(nb)))
        for g, w in enumerate(POOL_WINDOWS):
            cols = slice(g * group, (g + 1) * group)
            u_t = u_scr[tail + r0:tail + r0 + sr, cols]
            acc = u_t
            for j in range(1, w):
                acc = acc + u_scr[tail + r0 - j * nb:tail + r0 - j * nb + sr, cols]
            cnt = jnp.minimum(pos + 1, w).astype(F32)
            diff = acc / cnt - u_t
            y = _dot(diff.astype(BF16), wpool_ref[g]) * pscale_ref[:, cols]
            pool_ref[rows, cols] = y.astype(pool_ref.dtype)

        cosv = cos_ref[rows, :]
        sinv = sin_ref[rows, :]

        def normed_rope(col0, gain_ref, hm_ref):
            proj = _dot(h, win_ref[:, col0:col0 + attn_width])
            for hh in range(n_heads):
                xn = _rms(proj[:, hh * HEAD_DIM:(hh + 1) * HEAD_DIM], gain_ref[...])
                hm_ref[hh, rows, :] = xn * cosv + pltpu.roll(xn, HEAD_DIM // 2, 1) * sinv

        normed_rope(pool_width, qn_ref, q_ref)
        normed_rope(pool_width + attn_width, kn_ref, k_ref)
        vproj = _dot(h, win_ref[:, pool_width + 2 * attn_width:pool_width + 3 * attn_width])
        for hh in range(n_heads):
            v_ref[hh, rows, :] = vproj[:, hh * HEAD_DIM:(hh + 1) * HEAD_DIM]

    @pl.when(i == n_tiles - 1)
    def _():
        pstate_ref[...] = u_scr[tm + nb:tm + tail, :]

    if n_tiles > 1:
        u_scr[0:tail, :] = u_scr[tm:tm + tail, :]


def _in_proj(x, cos2, sin2, pool_prev, norm, w_in, q_norm, k_norm, w_pool, pool_scale, *, tm, nb, pos0):
    rows, d_model = x.shape
    pool_width = pool_scale.shape[-1]
    attn_width = (w_in.shape[1] - pool_width) // 3
    n_heads = attn_width // HEAD_DIM
    n_tiles = rows // tm
    assert rows % tm == 0 and nb & (nb - 1) == 0 and (tm >= POOL_TAIL * nb or n_tiles == 1)
    tile = lambda width: pl.BlockSpec((tm, width), lambda i: (i, 0))
    heads = pl.BlockSpec((n_heads, tm, HEAD_DIM), lambda i: (0, i, 0))
    vmem = (2 * tm * d_model * 4 + w_in.size * 2 + 2 * 2 * tm * LANES * 4 + 2 * tm * pool_width * 2
            + 3 * 2 * tm * attn_width * 4 + (POOL_TAIL * nb + tm) * pool_width * 4
            + 2 * POOL_TAIL * nb * pool_width * 4 + 6 * tm * attn_width * 4 + tm * d_model * 6 + (4 << 20))
    hm_shape = jax.ShapeDtypeStruct((n_heads, rows, HEAD_DIM), F32)
    return pl.pallas_call(
        functools.partial(_in_proj_kernel, tm=tm, nb=nb, pos0=pos0, n_tiles=n_tiles, n_heads=n_heads),
        grid=(n_tiles,),
        in_specs=[tile(d_model), tile(HEAD_DIM), tile(HEAD_DIM), _const_spec(pool_prev.shape),
                  _const_spec(norm.shape), _const_spec(w_in.shape), _const_spec(q_norm.shape),
                  _const_spec(k_norm.shape), _const_spec(w_pool.shape), _const_spec(pool_scale.shape)],
        out_specs=[tile(pool_width), heads, heads, heads,
                   pl.BlockSpec((POOL_STATE * nb, pool_width), lambda i: (0, 0))],
        out_shape=[jax.ShapeDtypeStruct((rows, pool_width), BF16), hm_shape, hm_shape, hm_shape,
                   jax.ShapeDtypeStruct((POOL_STATE * nb, pool_width), F32)],
        scratch_shapes=[pltpu.VMEM((POOL_TAIL * nb + tm, pool_width), F32)],
        compiler_params=_params(1, vmem),
        name="in_proj",
    )(x, cos2, sin2, pool_prev, norm, w_in, q_norm, k_norm, w_pool, pool_scale)


CLASSES = 4
LOG2E = 1.4426950408889634


def _dilated_attn_kernel(q_ref, k_ref, v_ref, o_ref, klast_ref, vlast_ref,
                         yq, yk, yv, q1, k1, v1, q4, k4, v4, q16, k16, v16,
                         o1, m1, l1, o4, m4, l4, o16, m16, l16, out_f, bias):
    i = pl.program_id(1)
    sb = q_ref.shape[0]

    @pl.when(i == pl.num_programs(1) - 1)
    def _():
        klast_ref[...] = k_ref[...]
        vlast_ref[...] = v_ref[...]

    n4 = sb // CLASSES
    nk4 = BAND + n4
    nk16 = 2 * BAND
    slots = sb // BAND
    scale = HEAD_DIM ** -0.5 * LOG2E
    keys = ((k1, v1), (k4, v4), (k16, v16))
    prev_rows = ([(0, sb)], [(r * nk4, r * nk4 + n4) for r in range(CLASSES)],
                 [(s * nk16, s * nk16 + BAND) for s in range(slots)])
    val = slice(0, HEAD_DIM)

    @pl.when(i == 0)
    def _():
        for (kd, vd), rows in zip(keys, prev_rows):
            vd[:, HEAD_DIM:] = jnp.ones((vd.shape[0], HEAD_DIM), BF16)
            for dst, _ in rows:
                kd[dst:dst + BAND, :] = jnp.zeros((BAND, HEAD_DIM), BF16)
                vd[dst:dst + BAND, val] = jnp.zeros((BAND, HEAD_DIM), BF16)
        qa = lax.broadcasted_iota(jnp.int32, (BAND, 2 * BAND), 0)
        col = lax.broadcasted_iota(jnp.int32, (BAND, 2 * BAND), 1)
        in_band = (col >= qa) & (col <= qa + BAND)
        bias[0] = jnp.where(in_band, 0.0, NEG_INF)
        bias[1] = jnp.where(in_band & (col >= BAND), 0.0, NEG_INF)

    @pl.when(i > 0)
    def _():
        for (kd, vd), rows in zip(keys, prev_rows):
            for dst, src in rows:
                kd[dst:dst + BAND, :] = kd[src:src + BAND, :]
                vd[dst:dst + BAND, val] = vd[src:src + BAND, val]

    q1[...] = q_ref[...].astype(BF16)
    k1[BAND:BAND + sb, :] = k_ref[...].astype(BF16)
    v1[BAND:BAND + sb, val] = v_ref[...].astype(BF16)
    for src, y, d4, d16, lead, lanes in ((q_ref, yq, q4, q16, 0, slice(None)), (k_ref, yk, k4, k16, BAND, slice(None)),
                                        (v_ref, yv, v4, v16, BAND, val)):
        for r1 in range(CLASSES):
            rows = src[pl.ds(r1, n4, stride=CLASSES), :]
            y[r1 * n4:(r1 + 1) * n4, :] = rows
            d4[r1 * (lead + n4) + lead:(r1 + 1) * (lead + n4), lanes] = rows.astype(BF16)
        for s in range(slots):
            r1, r2 = divmod(s, CLASSES)
            rows = y[pl.ds(r1 * n4 + r2, BAND, stride=CLASSES), :]
            d16[s * (lead + BAND) + lead:(s + 1) * (lead + BAND), lanes] = rows.astype(BF16)

    first = (i == 0).astype(jnp.int32)

    def branch(qd, kd, vd, units, od, md, ld, rows_of):
        for u, (q0, k0, leads) in enumerate(units):
            s = _dot_nt(qd[q0:q0 + BAND, :], kd[k0:k0 + 2 * BAND, :]) * scale + bias[first if leads else 0]
            m = jnp.max(s, axis=-1, keepdims=True)
            ol = _dot(jnp.exp2(s - m).astype(BF16), vd[k0:k0 + 2 * BAND, :])
            od[rows_of(u), :] = ol[:, val]
            md[rows_of(u), :] = jnp.broadcast_to(m, (BAND, HEAD_DIM))
            ld[rows_of(u), :] = ol[:, HEAD_DIM:]

    blocks4 = n4 // BAND
    in_order = lambda u: slice(u * BAND, (u + 1) * BAND)
    branch(q1, k1, v1, [(u * BAND, u * BAND, u == 0) for u in range(slots)], o1, m1, l1, in_order)
    branch(q4, k4, v4, [(r * n4 + b * BAND, r * nk4 + b * BAND, b == 0)
                        for r in range(CLASSES) for b in range(blocks4)], o4, m4, l4, in_order)
    branch(q16, k16, v16, [(s * BAND, s * nk16, True) for s in range(slots)], o16, m16, l16,
           lambda s: pl.ds((s // CLASSES) * n4 + s % CLASSES, BAND, stride=CLASSES))

    for piece in range(slots):
        r1, j = divmod(piece, blocks4)
        rows = in_order(piece)
        nat = pl.ds(r1 + CLASSES * BAND * j, BAND, stride=CLASSES)
        ma, mb, mc = m1[nat, :], m4[rows, :], m16[rows, :]
        m = jnp.maximum(jnp.maximum(ma, mb), mc)
        wa, wb, wc = jnp.exp2(ma - m), jnp.exp2(mb - m), jnp.exp2(mc - m)
        num = wa * o1[nat, :] + wb * o4[rows, :] + wc * o16[rows, :]
        den = wa * l1[nat, :] + wb * l4[rows, :] + wc * l16[rows, :]
        out_f[nat, :] = num / den
    o_ref[...] = out_f[...].astype(o_ref.dtype)


def _dilated_attn(q_hm, k_hm, v_hm):
    n_heads, rows, _ = q_hm.shape
    sb = SUPER_BLOCK
    assert rows % sb == 0 and [d for _, d in DILATIONS] == [1, CLASSES, CLASSES ** 2]
    assert all(w == BAND * d for w, d in DILATIONS) and sb == BAND * CLASSES ** 2
    block = pl.BlockSpec((None, sb, HEAD_DIM), lambda h, i: (h, i, 0))
    blk = lambda n, dt, width=HEAD_DIM: pltpu.VMEM((n, width), dt)
    lead = (BAND, CLASSES * BAND, sb)
    scratch = ([blk(sb, F32)] * 3
               + [s for n in lead for s in (blk(sb, BF16), blk(n + sb, BF16), blk(n + sb, BF16, 2 * HEAD_DIM))]
               + [blk(sb, F32)] * 10
               + [pltpu.VMEM((2, BAND, 2 * BAND), F32)])
    vmem = (2 * 3 * sb * HEAD_DIM * 4 + 2 * sb * HEAD_DIM * 2 + 2 * 2 * sb * HEAD_DIM * 4 + 13 * sb * HEAD_DIM * 4
            + (12 * sb + 3 * sum(lead)) * HEAD_DIM * 2 + 2 * BAND * 2 * BAND * 4 + (12 << 20))
    last = pl.BlockSpec((sb, HEAD_DIM), lambda h, i: (0, h))
    last_shape = jax.ShapeDtypeStruct((sb, n_heads * HEAD_DIM), F32)
    return pl.pallas_call(
        _dilated_attn_kernel,
        grid=(n_heads, rows // sb),
        in_specs=[block, block, block],
        out_specs=[pl.BlockSpec((sb, HEAD_DIM), lambda h, i: (i, h)), last, last],
        out_shape=[jax.ShapeDtypeStruct((rows, n_heads * HEAD_DIM), BF16), last_shape, last_shape],
        scratch_shapes=scratch,
        compiler_params=_params(2, vmem),
        name="dilated_attn",
    )(q_hm, k_hm, v_hm)


FAR_STEP = DILATIONS[2][1]
NEAR_ROWS = DILATIONS[1][0]


def _sample_attn_kernel(q_ref, kn_ref, vn_ref, *refs, n_heads, past_len, cache, n_far):
    far_k, far_v = refs[0:n_far], refs[n_far:2 * n_far]
    near_k, near_v, o_ref = refs[2 * n_far:]
    tq = q_ref.shape[0]
    far_m = cache // FAR_STEP
    scale = HEAD_DIM ** -0.5

    def multiplicity(key, ok):
        t = lax.broadcasted_iota(jnp.int32, key.shape, 0)
        dist = cache + t - key
        seen = ok & (dist >= 0) & (past_len + t - dist >= 0)
        n = jnp.zeros(key.shape, F32)
        for window, dil in DILATIONS:
            hit = seen & (dist <= window) & ((dist & (dil - 1)) == 0)
            n = n + jnp.where(hit, 1.0, 0.0)
        return n

    lane = lax.broadcasted_iota(jnp.int32, (tq, far_m), 1)
    near_key = cache - NEAR_ROWS + lax.broadcasted_iota(jnp.int32, (tq, NEAR_ROWS), 1)
    old_key = jnp.concatenate([FAR_STEP * lane + r for r in range(n_far)] + [near_key], axis=1)
    far_col = lax.broadcasted_iota(jnp.int32, old_key.shape, 1) < n_far * far_m
    n_old = multiplicity(old_key, ~(far_col & (old_key >= cache - NEAR_ROWS)))
    n_new = multiplicity(cache + lax.broadcasted_iota(jnp.int32, (tq, tq), 1), True)

    def head_rows(far, near, hh):
        parts = [f.reshape(far_m * n_heads, HEAD_DIM)[pl.ds(hh, far_m, stride=n_heads), :] for f in far]
        parts.append(near[pl.ds(hh, NEAR_ROWS, stride=n_heads), :])
        return jnp.concatenate(parts, axis=0).astype(BF16)

    for hh in range(n_heads):
        cols = slice(hh * HEAD_DIM, (hh + 1) * HEAD_DIM)
        q = q_ref[:, cols].astype(BF16)
        s_o = jnp.where(n_old > 0, _dot_nt(q, head_rows(far_k, near_k, hh)) * scale, NEG_INF)
        s_n = jnp.where(n_new > 0, _dot_nt(q, kn_ref[:, cols].astype(BF16)) * scale, NEG_INF)
        m = jnp.maximum(jnp.max(s_o, axis=-1, keepdims=True), jnp.max(s_n, axis=-1, keepdims=True))
        p_o = n_old * jnp.exp(s_o - m)
        p_n = n_new * jnp.exp(s_n - m)
        den = jnp.sum(p_o, axis=-1, keepdims=True) + jnp.sum(p_n, axis=-1, keepdims=True)
        num = (_dot(p_o.astype(BF16), head_rows(far_v, near_v, hh))
               + _dot(p_n.astype(BF16), vn_ref[:, cols].astype(BF16)))
        o_ref[:, cols] = (num / den).astype(o_ref.dtype)


def _sample_attn(q, k_new, v_new, cache_k, cache_v, *, n_queries, past_len):
    batch, tq, width = q.shape
    _, cache, n_heads, _ = cache_k.shape
    n_far = min(n_queries, FAR_STEP)
    assert all(d & (d - 1) == 0 for _, d in DILATIONS) and cache % NEAR_ROWS == 0 and cache % FAR_STEP == 0
    assert DILATIONS[0][0] <= NEAR_ROWS and tq % SUBLANES == 0
    new = pl.BlockSpec((None, tq, width), lambda b: (b, 0, 0))
    far_view = lambda c: c.reshape(batch, cache // FAR_STEP, FAR_STEP, n_heads, HEAD_DIM)
    near_view = lambda c: c.reshape(batch, cache * n_heads, HEAD_DIM)
    far = [pl.BlockSpec((None, cache // FAR_STEP, None, n_heads, HEAD_DIM),
                        functools.partial(lambda b, r: (b, 0, r, 0, 0), r=r)) for r in range(n_far)]
    near = pl.BlockSpec((None, NEAR_ROWS * n_heads, HEAD_DIM), lambda b: (b, cache // NEAR_ROWS - 1, 0))
    vmem = 2 * 2 * (n_far * cache // FAR_STEP + NEAR_ROWS) * width * 4 + (16 << 20)
    return pl.pallas_call(
        functools.partial(_sample_attn_kernel, n_heads=n_heads, past_len=past_len, cache=cache, n_far=n_far),
        grid=(batch,),
        in_specs=[new, new, new] + far + far + [near, near],
        out_specs=new,
        out_shape=jax.ShapeDtypeStruct((batch, tq, width), BF16),
        compiler_params=_params(1, vmem),
        name="sample_attn",
    )(q, k_new, v_new, *([far_view(cache_k)] * n_far), *([far_view(cache_v)] * n_far),
      near_view(cache_k), near_view(cache_v))


def _memory_kv_kernel(mem_ref, norm_ref, wk_ref, wv_ref, kn_ref, k_ref, v_ref):
    m = _rms(mem_ref[...], norm_ref[...]).astype(BF16)
    n_mem = m.shape[0]
    k = _dot(m, wk_ref[...])
    v = _dot(m, wv_ref[...])
    heads = k.shape[1] // MEM_HEAD_DIM
    for hh in range(heads):
        cols = slice(hh * MEM_HEAD_DIM, (hh + 1) * MEM_HEAD_DIM)
        k_ref[pl.ds(hh, n_mem, stride=heads), :] = _rms(k[:, cols], kn_ref[...])
        v_ref[pl.ds(hh, n_mem, stride=heads), :] = v[:, cols]


def _memory_kv(mem, norm_src, w_k, w_v, k_norm):
    n_mem = mem.shape[0]
    heads = w_k.shape[1] // MEM_HEAD_DIM
    args = (mem, norm_src, w_k, w_v, k_norm)
    out = jax.ShapeDtypeStruct((n_mem * heads, MEM_HEAD_DIM), F32)
    return pl.pallas_call(
        _memory_kv_kernel,
        grid=(1,),
        in_specs=[_const_spec(a.shape) for a in args],
        out_specs=[pl.BlockSpec((n_mem * heads, MEM_HEAD_DIM), lambda i: (0, 0))] * 2,
        out_shape=[out, out],
        compiler_params=_params(1, 32 << 20),
        name="memory_kv",
    )(*args)


def _mem_mix_kernel(x_ref, pool_ref, attn_ref, wout_ref, nmem_ref, wq_ref, qn_ref, mk_ref, mv_ref, wo_ref, nffn_ref,
                    x2_ref, h3_ref, *, nb, n_mem):
    tm = x_ref.shape[0]
    pool_width = pool_ref.shape[1]
    x1 = x_ref[...] + _dot(pool_ref[...], wout_ref[0:pool_width, :]) + _dot(attn_ref[...], wout_ref[pool_width:, :])
    h2 = _rms(x1, nmem_ref[...]).astype(BF16)
    qm = _dot(h2, wq_ref[...])
    scale = MEM_HEAD_DIM ** -0.5
    n_heads = qm.shape[1] // MEM_HEAD_DIM
    n_keys = mk_ref.shape[0] // n_heads
    if nb > 1:
        seq_q = (pl.program_id(0) * tm + lax.broadcasted_iota(jnp.int32, (tm, n_keys), 0)) & (nb - 1)
        seq_k = lax.broadcasted_iota(jnp.int32, (tm, n_keys), 1) >> int(math.log2(n_mem))
        own = seq_q == seq_k
    heads = []
    for hh in range(n_heads):
        cols = slice(hh * MEM_HEAD_DIM, (hh + 1) * MEM_HEAD_DIM)
        head_rows = pl.ds(hh, n_keys, stride=n_heads)
        q = _rms(qm[:, cols], qn_ref[...]).astype(BF16)
        s = _dot_nt(q, mk_ref[head_rows, :].astype(BF16)) * scale
        if nb > 1:
            s = jnp.where(own, s, NEG_INF)
        p = jnp.exp(s - jnp.max(s, axis=-1, keepdims=True))
        o = _dot(p.astype(BF16), mv_ref[head_rows, :].astype(BF16)) / jnp.sum(p, axis=-1, keepdims=True)
        heads.append(o.astype(BF16))
    x2 = x1 + _dot(jnp.concatenate(heads, axis=-1), wo_ref[...])
    x2_ref[...] = x2
    h3_ref[...] = _rms(x2, nffn_ref[...]).astype(h3_ref.dtype)


def _mem_mix(x, pool_out, attn_out, w_out, norm_mem, w_q, q_norm, mem_k, mem_v, w_o, norm_ffn, *, tm, nb, n_mem):
    rows, d_model = x.shape
    n_keys = mem_k.shape[0] * MEM_HEAD_DIM // w_q.shape[1]
    assert rows % tm == 0 and nb & (nb - 1) == 0 and n_mem & (n_mem - 1) == 0 and n_keys == nb * n_mem
    tile = lambda a: pl.BlockSpec((tm, a.shape[1]), lambda i: (i, 0))
    consts = (w_out, norm_mem, w_q, q_norm, mem_k, mem_v, w_o, norm_ffn)
    vmem = (sum(c.size * c.dtype.itemsize for c in consts) + 2 * tm * d_model * (4 + 2 + 4 + 2)
            + tm * d_model * 16 + 4 * tm * n_keys * 4 + (6 << 20))
    return pl.pallas_call(
        functools.partial(_mem_mix_kernel, nb=nb, n_mem=n_mem),
        grid=(rows // tm,),
        in_specs=[tile(x), tile(pool_out), tile(attn_out)] + [_const_spec(c.shape) for c in consts],
        out_specs=[pl.BlockSpec((tm, d_model), lambda i: (i, 0))] * 2,
        out_shape=[jax.ShapeDtypeStruct((rows, d_model), F32), jax.ShapeDtypeStruct((rows, d_model), BF16)],
        compiler_params=_params(1, vmem),
        name="mem_mix",
    )(x, pool_out, attn_out, *consts)


def _conv_ffn_kernel(h_ref, x_ref, wg_ref, wu_ref, cw_ref, cb_ref, wd_ref, prev_ref, y_ref, state_ref,
                     g_ext, g_scr, *, nb):
    i = pl.program_id(0)
    j = pl.program_id(1)
    tm = h_ref.shape[0]
    tail = CONV_TAIL * nb

    @pl.when(j == 0)
    def _():
        y_ref[...] = x_ref[...]

    @pl.when(i == 0)
    def _():
        g_ext[0:tail, :] = prev_ref[...]

    @pl.when(i > 0)
    def _():
        g_ext[0:tail, :] = g_scr[j]

    g_ext[tail:tail + tm, :] = _dot(h_ref[...], wg_ref[...])
    up = _dot(h_ref[...], wu_ref[...])
    c = cb_ref[...]
    for tap in range(CONV_WIDTH):
        c = c + g_ext[tap * nb:tap * nb + tm, :] * cw_ref[tap:tap + 1, :]
    act = (c * jax.nn.sigmoid(c) * up).astype(BF16)
    y_ref[...] += _dot(act, wd_ref[...])

    last = g_ext[tm:tm + tail, :]
    g_scr[j] = last
    state_ref[:, pl.ds(pl.multiple_of(j * last.shape[1], LANES), last.shape[1])] = last


def _conv_ffn(h3, x2, w_gate, w_up, conv_w, conv_b, w_down, conv_prev, *, tm, tf, nb):
    rows, d_model = x2.shape
    d_ff = w_gate.shape[1]
    assert rows % tm == 0 and d_ff % tf == 0 and tm >= CONV_TAIL * nb
    tail = CONV_TAIL * nb
    row_tile = pl.BlockSpec((tm, d_model), lambda i, j: (i, 0))
    chunk = lambda n: pl.BlockSpec((n, tf), lambda i, j: (0, j))
    vmem = (2 * tm * d_model * (2 + 4 + 4) + 2 * 3 * d_model * tf * 2 + (tm + 3 * tail) * tf * 4
            + d_ff // tf * tail * tf * 4 + 5 * tm * tf * 4 + tm * d_model * 4 + (6 << 20))
    return pl.pallas_call(
        functools.partial(_conv_ffn_kernel, nb=nb),
        grid=(rows // tm, d_ff // tf),
        in_specs=[row_tile, row_tile, chunk(d_model), chunk(d_model), chunk(CONV_WIDTH), chunk(1),
                  pl.BlockSpec((tf, d_model), lambda i, j: (j, 0)), chunk(tail)],
        out_specs=[row_tile, pl.BlockSpec((tail, d_ff), lambda i, j: (0, 0))],
        out_shape=[jax.ShapeDtypeStruct((rows, d_model), F32), jax.ShapeDtypeStruct((tail, d_ff), F32)],
        scratch_shapes=[pltpu.VMEM((tail + tm, tf), F32), pltpu.VMEM((d_ff // tf, tail, tf), F32)],
        compiler_params=_params(2, vmem),
        name="conv_ffn",
    )(h3, x2, w_gate, w_up, conv_w, conv_b, w_down, conv_prev)


ROPE_SPLIT = 128


def _rope_tables(pos0, n_pos):
    half = HEAD_DIM // 2
    inv = 1.0 / (ROPE_THETA ** (jnp.arange(half, dtype=F32) * (2.0 / HEAD_DIM)))
    angle = lambda pos: pos.astype(F32)[:, None] * inv[None, :]
    if pos0 % ROPE_SPLIT == 0 and n_pos % ROPE_SPLIT == 0:
        a = angle(pos0 + ROPE_SPLIT * jnp.arange(n_pos // ROPE_SPLIT))[:, None, :]
        b = angle(jnp.arange(ROPE_SPLIT))[None, :, :]
        cos = (jnp.cos(a) * jnp.cos(b) - jnp.sin(a) * jnp.sin(b)).reshape(n_pos, half)
        sin = (jnp.sin(a) * jnp.cos(b) + jnp.cos(a) * jnp.sin(b)).reshape(n_pos, half)
    else:
        ang = angle(pos0 + jnp.arange(n_pos))
        cos, sin = jnp.cos(ang), jnp.sin(ang)
    return jnp.concatenate([cos, cos], axis=-1), jnp.concatenate([-sin, sin], axis=-1)


def _row_tile(rows, want):
    return want if rows % want == 0 else rows


def _layer(x, pos0, nb, pool_prev, conv_prev, mem_k, mem_v, n_mem, attend, p):
    rows = x.shape[0]
    cos2, sin2 = (jnp.repeat(t, nb, axis=0) for t in _rope_tables(pos0, rows // nb))
    pool_out, q_hm, k_hm, v_hm, pool_state = _in_proj(
        x, cos2, sin2, pool_prev, p["norm_mix"], p["w_in"], p["q_norm"], p["k_norm"], p["w_pool"], p["pool_scale"],
        tm=_row_tile(rows, IN_PROJ_ROWS), nb=nb, pos0=pos0)
    attn_out, k_keep, v_keep = attend(q_hm, k_hm, v_hm)
    x2, h3 = _mem_mix(x, pool_out, attn_out, p["w_out"], p["norm_mem"], p["w_mem_q"], p["mem_q_norm"], mem_k, mem_v,
                      p["w_mem_o"], p["norm_ffn"], tm=_row_tile(rows, MEM_MIX_ROWS), nb=nb, n_mem=n_mem)
    y, conv_state = _conv_ffn(h3, x2, p["w_gate"], p["w_up"], p["conv_w"], p["conv_b"], p["w_down"], conv_prev,
                              tm=_row_tile(rows, FFN_ROWS), tf=FFN_COLS, nb=nb)
    return y, pool_state, k_keep, v_keep, conv_state


def kernel(x_prompt, x_sample, state_pool, cache_win_k, cache_win_v, cache_mem_k, cache_mem_v, state_conv, mem_prompt, norm_mix, w_in, q_norm, k_norm, w_pool, pool_scale, w_out, norm_mem, norm_mem_src, w_mem_q, w_mem_k, w_mem_v, mem_q_norm, mem_k_norm, w_mem_o, norm_ffn, w_gate, w_up, conv_w, conv_b, w_down):
    bp, tp, d_model = x_prompt.shape
    bs, ts, _ = x_sample.shape
    depth = w_in.shape[0]
    assert bp == 1
    pool_width = pool_scale.shape[-1]
    d_ff = w_gate.shape[-1]
    n_mem = mem_prompt.shape[1]
    n_heads = cache_win_k.shape[3]
    mem_heads = cache_mem_k.shape[3]
    attn_width = n_heads * HEAD_DIM
    keep_p = min(MAX_WINDOW, tp)
    assert keep_p == SUPER_BLOCK
    row = lambda a: a.reshape(1, -1)

    xp = x_prompt.reshape(tp, d_model)
    xs = x_sample.transpose(1, 0, 2).reshape(ts * bs, d_model)
    outs = {k: [] for k in ("p_pool", "p_k", "p_v", "p_mk", "p_mv", "p_conv", "s_pool", "s_k", "s_v", "s_conv")}
    for l in range(depth):
        prm = {
            "norm_mix": row(norm_mix[l]), "w_in": w_in[l].astype(BF16), "q_norm": row(q_norm[l]),
            "k_norm": row(k_norm[l]), "w_pool": w_pool[l].astype(BF16), "pool_scale": row(pool_scale[l]),
            "w_out": w_out[l].astype(BF16), "norm_mem": row(norm_mem[l]), "w_mem_q": w_mem_q[l].astype(BF16),
            "mem_q_norm": row(mem_q_norm[l]), "w_mem_o": w_mem_o[l].astype(BF16), "norm_ffn": row(norm_ffn[l]),
            "w_gate": w_gate[l].astype(BF16), "w_up": w_up[l].astype(BF16), "conv_w": conv_w[l],
            "conv_b": row(conv_b[l]), "w_down": w_down[l].astype(BF16),
        }
        mk, mv = _memory_kv(mem_prompt[0], row(norm_mem_src[l]), w_mem_k[l].astype(BF16), w_mem_v[l].astype(BF16),
                            row(mem_k_norm[l]))
        xp, pool_st, kp, vp, conv_st = _layer(
            xp, 0, 1, jnp.zeros((POOL_TAIL, pool_width), F32), jnp.zeros((CONV_TAIL, d_ff), F32), mk, mv, n_mem,
            _dilated_attn, prm)
        outs["p_pool"].append(pool_st[None])
        outs["p_k"].append(kp.reshape(1, keep_p, n_heads, HEAD_DIM))
        outs["p_v"].append(vp.reshape(1, keep_p, n_heads, HEAD_DIM))
        outs["p_mk"].append(mk.reshape(1, n_mem, mem_heads, MEM_HEAD_DIM))
        outs["p_mv"].append(mv.reshape(1, n_mem, mem_heads, MEM_HEAD_DIM))
        outs["p_conv"].append(conv_st[None])

        pool_prev = jnp.pad(state_pool[l].transpose(1, 0, 2), ((POOL_TAIL - POOL_STATE, 0), (0, 0), (0, 0)))
        conv_prev = state_conv[l].transpose(1, 0, 2).reshape(CONV_TAIL * bs, d_ff)

        def attend_samples(q_hm, k_hm, v_hm, l=l):
            pad = lambda a: jnp.pad(a, ((0, 0), (0, -ts % SUBLANES), (0, 0)))
            to_seq = lambda a: a.reshape(n_heads, ts, bs, HEAD_DIM).transpose(2, 1, 0, 3).reshape(bs, ts, attn_width)
            k_new, v_new = to_seq(k_hm), to_seq(v_hm)
            o = _sample_attn(pad(to_seq(q_hm)), pad(k_new), pad(v_new), cache_win_k[l], cache_win_v[l],
                             n_queries=ts, past_len=PAST_LEN)
            return o[:, :ts].transpose(1, 0, 2).reshape(ts * bs, attn_width), k_new, v_new

        xs, pool_st, ks, vs, conv_st = _layer(
            xs, PAST_LEN, bs, pool_prev.reshape(POOL_TAIL * bs, pool_width), conv_prev,
            cache_mem_k[l].reshape(bs * n_mem * mem_heads, MEM_HEAD_DIM),
            cache_mem_v[l].reshape(bs * n_mem * mem_heads, MEM_HEAD_DIM), n_mem, attend_samples, prm)
        seq_major = lambda a, t: a.reshape(t, bs, -1).transpose(1, 0, 2)
        outs["s_pool"].append(seq_major(pool_st, POOL_STATE))
        outs["s_k"].append(ks.reshape(bs, ts, n_heads, HEAD_DIM))
        outs["s_v"].append(vs.reshape(bs, ts, n_heads, HEAD_DIM))
        outs["s_conv"].append(seq_major(conv_st, CONV_TAIL))
    stack = lambda k: jnp.stack(outs[k])
    return (xp.reshape(bp, tp, d_model), xs.reshape(ts, bs, d_model).transpose(1, 0, 2),
            stack("p_pool"), stack("p_k"), stack("p_v"), stack("p_mk"), stack("p_mv"), stack("p_conv"),
            stack("s_pool"), stack("s_k"), stack("s_v"), stack("s_conv"))
```

```python
import functools
import math

import jax
import jax.numpy as jnp
from jax import lax
from jax.experimental import pallas as pl
from jax.experimental.pallas import tpu as pltpu

PAST_LEN = 16384
POOL_WINDOWS = (2, 4, 8, 16)
HEAD_DIM = 128
DILATIONS = ((128, 1), (512, 4), (2048, 16))
MAX_WINDOW = 2048
ROPE_THETA = 10000.0
MEM_HEAD_DIM = 128
CONV_WIDTH = 3
EPS = 1e-6
NEG_INF = -1e30

POOL_STATE = max(POOL_WINDOWS) - 1
POOL_TAIL = max(POOL_WINDOWS)
CONV_TAIL = CONV_WIDTH - 1
BAND = DILATIONS[0][0]
SUPER_BLOCK = MAX_WINDOW

V7X_SCOPED_VMEM_LIMIT = 60000 * 1024
LANES = 128
SUBLANES = 8

IN_PROJ_ROWS = 512
IN_PROJ_GROUP_ROWS = 256
MEM_MIX_ROWS = 512
FFN_ROWS = 512
FFN_COLS = 512

F32 = jnp.float32
BF16 = jnp.bfloat16


def _const_spec(shape):
    return pl.BlockSpec(shape, lambda *_: (0,) * len(shape), pipeline_mode=pl.Buffered(1))


def _params(n_axes, vmem_bytes):
    return pltpu.CompilerParams(
        dimension_semantics=("arbitrary",) * n_axes,
        vmem_limit_bytes=min(int(vmem_bytes), V7X_SCOPED_VMEM_LIMIT),
    )


def _rms(x, gain):
    return x * lax.rsqrt(jnp.mean(x * x, axis=-1, keepdims=True) + EPS) * gain


def _dot(a, b):
    return jnp.dot(a, b, preferred_element_type=F32)


def _dot_nt(a, b):
    return lax.dot_general(a, b, (((1,), (1,)), ((), ())), preferred_element_type=F32)


def _in_proj_kernel(x_ref, cos_ref, sin_ref, prev_ref, norm_ref, win_ref, qn_ref, kn_ref, wpool_ref, pscale_ref,
                    pool_ref, q_ref, k_ref, v_ref, pstate_ref, u_scr,
                    *, tm, nb, pos0, n_tiles, n_heads):
    i = pl.program_id(0)
    tail = POOL_TAIL * nb
    pool_width = pool_ref.shape[1]
    attn_width = n_heads * HEAD_DIM
    group = pool_width // len(POOL_WINDOWS)
    sr = min(tm, IN_PROJ_GROUP_ROWS)

    @pl.when(i == 0)
    def _():
        u_scr[0:tail, :] = prev_ref[...]

    for r0 in range(0, tm, sr):
        rows = slice(r0, r0 + sr)
        h = _rms(x_ref[rows, :], norm_ref[...]).astype(BF16)
        u_scr[tail + r0:tail + r0 + sr, :] = _dot(h, win_ref[:, 0:pool_width])
        row = i * tm + r0 + lax.broadcasted_iota(jnp.int32, (sr, 1), 0)
        pos = pos0 + (row >> int(math.log2(nb)))
        diffs = []
        for g, w in enumerate(POOL_WINDOWS):
            cols = slice(g * group, (g + 1) * group)
            u_t = u_scr[tail + r0:tail + r0 + sr, cols]
            acc = u_t
            for j in range(1, w):
                acc = acc + u_scr[tail + r0 - j * nb:tail + r0 - j * nb + sr, cols]
            cnt = jnp.minimum(pos + 1, w).astype(F32)
            diffs.append((acc / cnt - u_t).astype(BF16))

        cosv = cos_ref[rows, :]
        sinv = sin_ref[rows, :]

        def normed_rope(col0, gain_ref, hm_ref):
            proj = _dot(h, win_ref[:, col0:col0 + attn_width])
            for hh in range(n_heads):
                xn = _rms(proj[:, hh * HEAD_DIM:(hh + 1) * HEAD_DIM], gain_ref[...])
                hm_ref[hh, rows, :] = xn * cosv + pltpu.roll(xn, HEAD_DIM // 2, 1) * sinv

        normed_rope(pool_width, qn_ref, q_ref)
        normed_rope(pool_width + attn_width, kn_ref, k_ref)
        vproj = _dot(h, win_ref[:, pool_width + 2 * attn_width:pool_width + 3 * attn_width])
        for hh in range(n_heads):
            v_ref[hh, rows, :] = vproj[:, hh * HEAD_DIM:(hh + 1) * HEAD_DIM]

        for g, diff in enumerate(diffs):
            cols = slice(g * group, (g + 1) * group)
            pool_ref[rows, cols] = (_dot(diff, wpool_ref[g]) * pscale_ref[:, cols]).astype(pool_ref.dtype)

    @pl.when(i == n_tiles - 1)
    def _():
        pstate_ref[...] = u_scr[tm + nb:tm + tail, :]

    if n_tiles > 1:
        u_scr[0:tail, :] = u_scr[tm:tm + tail, :]


def _in_proj(x, cos2, sin2, pool_prev, norm, w_in, q_norm, k_norm, w_pool, pool_scale, *, tm, nb, pos0):
    rows, d_model = x.shape
    pool_width = pool_scale.shape[-1]
    attn_width = (w_in.shape[1] - pool_width) // 3
    n_heads = attn_width // HEAD_DIM
    n_tiles = rows // tm
    assert rows % tm == 0 and nb & (nb - 1) == 0 and (tm >= POOL_TAIL * nb or n_tiles == 1)
    tile = lambda width: pl.BlockSpec((tm, width), lambda i: (i, 0))
    heads = pl.BlockSpec((n_heads, tm, HEAD_DIM), lambda i: (0, i, 0))
    vmem = (2 * tm * d_model * 4 + w_in.size * 2 + 2 * 2 * tm * LANES * 4 + 2 * tm * pool_width * 2
            + 3 * 2 * tm * attn_width * 4 + (POOL_TAIL * nb + tm) * pool_width * 4
            + 2 * POOL_TAIL * nb * pool_width * 4 + 6 * tm * attn_width * 4 + tm * d_model * 6 + (4 << 20))
    hm_shape = jax.ShapeDtypeStruct((n_heads, rows, HEAD_DIM), F32)
    return pl.pallas_call(
        functools.partial(_in_proj_kernel, tm=tm, nb=nb, pos0=pos0, n_tiles=n_tiles, n_heads=n_heads),
        grid=(n_tiles,),
        in_specs=[tile(d_model), tile(HEAD_DIM), tile(HEAD_DIM), _const_spec(pool_prev.shape),
                  _const_spec(norm.shape), _const_spec(w_in.shape), _const_spec(q_norm.shape),
                  _const_spec(k_norm.shape), _const_spec(w_pool.shape), _const_spec(pool_scale.shape)],
        out_specs=[tile(pool_width), heads, heads, heads,
                   pl.BlockSpec((POOL_STATE * nb, pool_width), lambda i: (0, 0))],
        out_shape=[jax.ShapeDtypeStruct((rows, pool_width), BF16), hm_shape, hm_shape, hm_shape,
                   jax.ShapeDtypeStruct((POOL_STATE * nb, pool_width), F32)],
        scratch_shapes=[pltpu.VMEM((POOL_TAIL * nb + tm, pool_width), F32)],
        compiler_params=_params(1, vmem),
        name="in_proj",
    )(x, cos2, sin2, pool_prev, norm, w_in, q_norm, k_norm, w_pool, pool_scale)


CLASSES = 4
LOG2E = 1.4426950408889634


def _dilated_attn_kernel(q_ref, k_ref, v_ref, o_ref, klast_ref, vlast_ref,
                         yq, yk, yv, q1, k1, v1, q4, k4, v4, q16, k16, v16,
                         o1, m1, l1, o4, m4, l4, o16, m16, l16, out_f, bias):
    i = pl.program_id(1)
    sb = q_ref.shape[0]

    @pl.when(i == pl.num_programs(1) - 1)
    def _():
        klast_ref[...] = k_ref[...]
        vlast_ref[...] = v_ref[...]

    n4 = sb // CLASSES
    nk4 = BAND + n4
    nk16 = 2 * BAND
    slots = sb // BAND
    scale = HEAD_DIM ** -0.5 * LOG2E
    keys = ((k1, v1), (k4, v4), (k16, v16))
    prev_rows = ([(0, sb)], [(r * nk4, r * nk4 + n4) for r in range(CLASSES)],
                 [(s * nk16, s * nk16 + BAND) for s in range(slots)])
    val = slice(0, HEAD_DIM)

    @pl.when(i == 0)
    def _():
        for (kd, vd), rows in zip(keys, prev_rows):
            vd[:, HEAD_DIM:] = jnp.ones((vd.shape[0], HEAD_DIM), BF16)
            for dst, _ in rows:
                kd[dst:dst + BAND, :] = jnp.zeros((BAND, HEAD_DIM), BF16)
                vd[dst:dst + BAND, val] = jnp.zeros((BAND, HEAD_DIM), BF16)
        qa = lax.broadcasted_iota(jnp.int32, (BAND, 2 * BAND), 0)
        col = lax.broadcasted_iota(jnp.int32, (BAND, 2 * BAND), 1)
        in_band = (col >= qa) & (col <= qa + BAND)
        bias[0] = jnp.where(in_band, 0.0, NEG_INF)
        bias[1] = jnp.where(in_band & (col >= BAND), 0.0, NEG_INF)

    @pl.when(i > 0)
    def _():
        for (kd, vd), rows in zip(keys, prev_rows):
            for dst, src in rows:
                kd[dst:dst + BAND, :] = kd[src:src + BAND, :]
                vd[dst:dst + BAND, val] = vd[src:src + BAND, val]

    q1[...] = q_ref[...].astype(BF16)
    k1[BAND:BAND + sb, :] = k_ref[...].astype(BF16)
    v1[BAND:BAND + sb, val] = v_ref[...].astype(BF16)
    for src, y, d4, d16, lead, lanes in ((q_ref, yq, q4, q16, 0, slice(None)), (k_ref, yk, k4, k16, BAND, slice(None)),
                                        (v_ref, yv, v4, v16, BAND, val)):
        for r1 in range(CLASSES):
            rows = src[pl.ds(r1, n4, stride=CLASSES), :]
            y[r1 * n4:(r1 + 1) * n4, :] = rows
            d4[r1 * (lead + n4) + lead:(r1 + 1) * (lead + n4), lanes] = rows.astype(BF16)
        for s in range(slots):
            r1, r2 = divmod(s, CLASSES)
            rows = y[pl.ds(r1 * n4 + r2, BAND, stride=CLASSES), :]
            d16[s * (lead + BAND) + lead:(s + 1) * (lead + BAND), lanes] = rows.astype(BF16)

    first = (i == 0).astype(jnp.int32)

    def branch(qd, kd, vd, units, od, md, ld, rows_of):
        for u, (q0, k0, leads) in enumerate(units):
            s = _dot_nt(qd[q0:q0 + BAND, :], kd[k0:k0 + 2 * BAND, :]) * scale + bias[first if leads else 0]
            m = jnp.max(s, axis=-1, keepdims=True)
            ol = _dot(jnp.exp2(s - m).astype(BF16), vd[k0:k0 + 2 * BAND, :])
            od[rows_of(u), :] = ol[:, val]
            md[rows_of(u), :] = jnp.broadcast_to(m, (BAND, HEAD_DIM))
            ld[rows_of(u), :] = ol[:, HEAD_DIM:]

    blocks4 = n4 // BAND
    in_order = lambda u: slice(u * BAND, (u + 1) * BAND)
    branch(q1, k1, v1, [(u * BAND, u * BAND, u == 0) for u in range(slots)], o1, m1, l1, in_order)
    branch(q4, k4, v4, [(r * n4 + b * BAND, r * nk4 + b * BAND, b == 0)
                        for r in range(CLASSES) for b in range(blocks4)], o4, m4, l4, in_order)
    branch(q16, k16, v16, [(s * BAND, s * nk16, True) for s in range(slots)], o16, m16, l16,
           lambda s: pl.ds((s // CLASSES) * n4 + s % CLASSES, BAND, stride=CLASSES))

    for piece in range(slots):
        r1, j = divmod(piece, blocks4)
        rows = in_order(piece)
        nat = pl.ds(r1 + CLASSES * BAND * j, BAND, stride=CLASSES)
        ma, mb, mc = m1[nat, :], m4[rows, :], m16[rows, :]
        m = jnp.maximum(jnp.maximum(ma, mb), mc)
        wa, wb, wc = jnp.exp2(ma - m), jnp.exp2(mb - m), jnp.exp2(mc - m)
        num = wa * o1[nat, :] + wb * o4[rows, :] + wc * o16[rows, :]
        den = wa * l1[nat, :] + wb * l4[rows, :] + wc * l16[rows, :]
        out_f[nat, :] = num / den
    o_ref[...] = out_f[...].astype(o_ref.dtype)


def _dilated_attn(q_hm, k_hm, v_hm):
    n_heads, rows, _ = q_hm.shape
    sb = SUPER_BLOCK
    assert rows % sb == 0 and [d for _, d in DILATIONS] == [1, CLASSES, CLASSES ** 2]
    assert all(w == BAND * d for w, d in DILATIONS) and sb == BAND * CLASSES ** 2
    block = pl.BlockSpec((None, sb, HEAD_DIM), lambda h, i: (h, i, 0))
    blk = lambda n, dt, width=HEAD_DIM: pltpu.VMEM((n, width), dt)
    lead = (BAND, CLASSES * BAND, sb)
    scratch = ([blk(sb, F32)] * 3
               + [s for n in lead for s in (blk(sb, BF16), blk(n + sb, BF16), blk(n + sb, BF16, 2 * HEAD_DIM))]
               + [blk(sb, F32)] * 10
               + [pltpu.VMEM((2, BAND, 2 * BAND), F32)])
    vmem = (2 * 3 * sb * HEAD_DIM * 4 + 2 * sb * HEAD_DIM * 2 + 2 * 2 * sb * HEAD_DIM * 4 + 13 * sb * HEAD_DIM * 4
            + (12 * sb + 3 * sum(lead)) * HEAD_DIM * 2 + 2 * BAND * 2 * BAND * 4 + (12 << 20))
    last = pl.BlockSpec((sb, HEAD_DIM), lambda h, i: (0, h))
    last_shape = jax.ShapeDtypeStruct((sb, n_heads * HEAD_DIM), F32)
    return pl.pallas_call(
        _dilated_attn_kernel,
        grid=(n_heads, rows // sb),
        in_specs=[block, block, block],
        out_specs=[pl.BlockSpec((sb, HEAD_DIM), lambda h, i: (i, h)), last, last],
        out_shape=[jax.ShapeDtypeStruct((rows, n_heads * HEAD_DIM), BF16), last_shape, last_shape],
        scratch_shapes=scratch,
        compiler_params=_params(2, vmem),
        name="dilated_attn",
    )(q_hm, k_hm, v_hm)


FAR_STEP = DILATIONS[2][1]
NEAR_ROWS = DILATIONS[1][0]


def _sample_attn_kernel(q_ref, kn_ref, vn_ref, *refs, n_heads, past_len, cache, n_far):
    far_k, far_v = refs[0:n_far], refs[n_far:2 * n_far]
    near_k, near_v, o_ref = refs[2 * n_far:]
    tq = q_ref.shape[0]
    far_m = cache // FAR_STEP
    scale = HEAD_DIM ** -0.5

    def multiplicity(key, ok):
        t = lax.broadcasted_iota(jnp.int32, key.shape, 0)
        dist = cache + t - key
        seen = ok & (dist >= 0) & (past_len + t - dist >= 0)
        n = jnp.zeros(key.shape, F32)
        for window, dil in DILATIONS:
            hit = seen & (dist <= window) & ((dist & (dil - 1)) == 0)
            n = n + jnp.where(hit, 1.0, 0.0)
        return n

    lane = lax.broadcasted_iota(jnp.int32, (tq, far_m), 1)
    near_key = cache - NEAR_ROWS + lax.broadcasted_iota(jnp.int32, (tq, NEAR_ROWS), 1)
    old_key = jnp.concatenate([FAR_STEP * lane + r for r in range(n_far)] + [near_key], axis=1)
    far_col = lax.broadcasted_iota(jnp.int32, old_key.shape, 1) < n_far * far_m
    n_old = multiplicity(old_key, ~(far_col & (old_key >= cache - NEAR_ROWS)))
    n_new = multiplicity(cache + lax.broadcasted_iota(jnp.int32, (tq, tq), 1), True)

    def head_rows(far, near, hh):
        parts = [f.reshape(far_m * n_heads, HEAD_DIM)[pl.ds(hh, far_m, stride=n_heads), :] for f in far]
        parts.append(near[pl.ds(hh, NEAR_ROWS, stride=n_heads), :])
        return jnp.concatenate(parts, axis=0).astype(BF16)

    for hh in range(n_heads):
        cols = slice(hh * HEAD_DIM, (hh + 1) * HEAD_DIM)
        q = q_ref[:, cols].astype(BF16)
        s_o = jnp.where(n_old > 0, _dot_nt(q, head_rows(far_k, near_k, hh)) * scale, NEG_INF)
        s_n = jnp.where(n_new > 0, _dot_nt(q, kn_ref[:, cols].astype(BF16)) * scale, NEG_INF)
        m = jnp.maximum(jnp.max(s_o, axis=-1, keepdims=True), jnp.max(s_n, axis=-1, keepdims=True))
        p_o = n_old * jnp.exp(s_o - m)
        p_n = n_new * jnp.exp(s_n - m)
        den = jnp.sum(p_o, axis=-1, keepdims=True) + jnp.sum(p_n, axis=-1, keepdims=True)
        num = (_dot(p_o.astype(BF16), head_rows(far_v, near_v, hh))
               + _dot(p_n.astype(BF16), vn_ref[:, cols].astype(BF16)))
        o_ref[:, cols] = (num / den).astype(o_ref.dtype)


def _sample_attn(q, k_new, v_new, cache_k, cache_v, *, n_queries, past_len):
    batch, tq, width = q.shape
    _, cache, n_heads, _ = cache_k.shape
    n_far = min(n_queries, FAR_STEP)
    assert all(d & (d - 1) == 0 for _, d in DILATIONS) and cache % NEAR_ROWS == 0 and cache % FAR_STEP == 0
    assert DILATIONS[0][0] <= NEAR_ROWS and tq % SUBLANES == 0
    new = pl.BlockSpec((None, tq, width), lambda b: (b, 0, 0))
    far_view = lambda c: c.reshape(batch, cache // FAR_STEP, FAR_STEP, n_heads, HEAD_DIM)
    near_view = lambda c: c.reshape(batch, cache * n_heads, HEAD_DIM)
    far = [pl.BlockSpec((None, cache // FAR_STEP, None, n_heads, HEAD_DIM),
                        functools.partial(lambda b, r: (b, 0, r, 0, 0), r=r)) for r in range(n_far)]
    near = pl.BlockSpec((None, NEAR_ROWS * n_heads, HEAD_DIM), lambda b: (b, cache // NEAR_ROWS - 1, 0))
    vmem = 2 * 2 * (n_far * cache // FAR_STEP + NEAR_ROWS) * width * 4 + (16 << 20)
    return pl.pallas_call(
        functools.partial(_sample_attn_kernel, n_heads=n_heads, past_len=past_len, cache=cache, n_far=n_far),
        grid=(batch,),
        in_specs=[new, new, new] + far + far + [near, near],
        out_specs=new,
        out_shape=jax.ShapeDtypeStruct((batch, tq, width), BF16),
        compiler_params=_params(1, vmem),
        name="sample_attn",
    )(q, k_new, v_new, *([far_view(cache_k)] * n_far), *([far_view(cache_v)] * n_far),
      near_view(cache_k), near_view(cache_v))


def _memory_kv_kernel(mem_ref, norm_ref, wk_ref, wv_ref, kn_ref, k_ref, v_ref):
    m = _rms(mem_ref[...], norm_ref[...]).astype(BF16)
    n_mem = m.shape[0]
    k = _dot(m, wk_ref[...])
    v = _dot(m, wv_ref[...])
    heads = k.shape[1] // MEM_HEAD_DIM
    for hh in range(heads):
        cols = slice(hh * MEM_HEAD_DIM, (hh + 1) * MEM_HEAD_DIM)
        k_ref[pl.ds(hh, n_mem, stride=heads), :] = _rms(k[:, cols], kn_ref[...])
        v_ref[pl.ds(hh, n_mem, stride=heads), :] = v[:, cols]


def _memory_kv(mem, norm_src, w_k, w_v, k_norm):
    n_mem = mem.shape[0]
    heads = w_k.shape[1] // MEM_HEAD_DIM
    args = (mem, norm_src, w_k, w_v, k_norm)
    out = jax.ShapeDtypeStruct((n_mem * heads, MEM_HEAD_DIM), F32)
    return pl.pallas_call(
        _memory_kv_kernel,
        grid=(1,),
        in_specs=[_const_spec(a.shape) for a in args],
        out_specs=[pl.BlockSpec((n_mem * heads, MEM_HEAD_DIM), lambda i: (0, 0))] * 2,
        out_shape=[out, out],
        compiler_params=_params(1, 32 << 20),
        name="memory_kv",
    )(*args)


def _mem_mix_kernel(x_ref, pool_ref, attn_ref, wout_ref, nmem_ref, wq_ref, qn_ref, mk_ref, mv_ref, wo_ref, nffn_ref,
                    x2_ref, h3_ref, *, nb, n_mem):
    tm = x_ref.shape[0]
    pool_width = pool_ref.shape[1]
    x1 = x_ref[...] + _dot(pool_ref[...], wout_ref[0:pool_width, :]) + _dot(attn_ref[...], wout_ref[pool_width:, :])
    h2 = _rms(x1, nmem_ref[...]).astype(BF16)
    qm = _dot(h2, wq_ref[...])
    scale = MEM_HEAD_DIM ** -0.5
    n_heads = qm.shape[1] // MEM_HEAD_DIM
    n_keys = mk_ref.shape[0] // n_heads
    if nb > 1:
        seq_q = (pl.program_id(0) * tm + lax.broadcasted_iota(jnp.int32, (tm, n_keys), 0)) & (nb - 1)
        seq_k = lax.broadcasted_iota(jnp.int32, (tm, n_keys), 1) >> int(math.log2(n_mem))
        own = seq_q == seq_k
    heads = []
    for hh in range(n_heads):
        cols = slice(hh * MEM_HEAD_DIM, (hh + 1) * MEM_HEAD_DIM)
        head_rows = pl.ds(hh, n_keys, stride=n_heads)
        q = _rms(qm[:, cols], qn_ref[...]).astype(BF16)
        s = _dot_nt(q, mk_ref[head_rows, :].astype(BF16)) * scale
        if nb > 1:
            s = jnp.where(own, s, NEG_INF)
        p = jnp.exp(s - jnp.max(s, axis=-1, keepdims=True))
        o = _dot(p.astype(BF16), mv_ref[head_rows, :].astype(BF16)) / jnp.sum(p, axis=-1, keepdims=True)
        heads.append(o.astype(BF16))
    x2 = x1 + _dot(jnp.concatenate(heads, axis=-1), wo_ref[...])
    x2_ref[...] = x2
    h3_ref[...] = _rms(x2, nffn_ref[...]).astype(h3_ref.dtype)


def _mem_mix(x, pool_out, attn_out, w_out, norm_mem, w_q, q_norm, mem_k, mem_v, w_o, norm_ffn, *, tm, nb, n_mem):
    rows, d_model = x.shape
    n_keys = mem_k.shape[0] * MEM_HEAD_DIM // w_q.shape[1]
    assert rows % tm == 0 and nb & (nb - 1) == 0 and n_mem & (n_mem - 1) == 0 and n_keys == nb * n_mem
    tile = lambda a: pl.BlockSpec((tm, a.shape[1]), lambda i: (i, 0))
    consts = (w_out, norm_mem, w_q, q_norm, mem_k, mem_v, w_o, norm_ffn)
    vmem = (sum(c.size * c.dtype.itemsize for c in consts) + 2 * tm * d_model * (4 + 2 + 4 + 2)
            + tm * d_model * 16 + 4 * tm * n_keys * 4 + (6 << 20))
    return pl.pallas_call(
        functools.partial(_mem_mix_kernel, nb=nb, n_mem=n_mem),
        grid=(rows // tm,),
        in_specs=[tile(x), tile(pool_out), tile(attn_out)] + [_const_spec(c.shape) for c in consts],
        out_specs=[pl.BlockSpec((tm, d_model), lambda i: (i, 0))] * 2,
        out_shape=[jax.ShapeDtypeStruct((rows, d_model), F32), jax.ShapeDtypeStruct((rows, d_model), BF16)],
        compiler_params=_params(1, vmem),
        name="mem_mix",
    )(x, pool_out, attn_out, *consts)


def _conv_ffn_kernel(h_ref, x_ref, wg_ref, wu_ref, cw_ref, cb_ref, wd_ref, prev_ref, y_ref, state_ref,
                     g_ext, g_scr, *, nb):
    i = pl.program_id(0)
    j = pl.program_id(1)
    tm = h_ref.shape[0]
    tail = CONV_TAIL * nb

    @pl.when(j == 0)
    def _():
        y_ref[...] = x_ref[...]

    @pl.when(i == 0)
    def _():
        g_ext[0:tail, :] = prev_ref[...]

    @pl.when(i > 0)
    def _():
        g_ext[0:tail, :] = g_scr[j]

    g_ext[tail:tail + tm, :] = _dot(h_ref[...], wg_ref[...])
    up = _dot(h_ref[...], wu_ref[...])
    c = cb_ref[...]
    for tap in range(CONV_WIDTH):
        c = c + g_ext[tap * nb:tap * nb + tm, :] * cw_ref[tap:tap + 1, :]
    act = (c * jax.nn.sigmoid(c) * up).astype(BF16)
    y_ref[...] += _dot(act, wd_ref[...])

    last = g_ext[tm:tm + tail, :]
    g_scr[j] = last
    state_ref[:, pl.ds(pl.multiple_of(j * last.shape[1], LANES), last.shape[1])] = last


def _conv_ffn(h3, x2, w_gate, w_up, conv_w, conv_b, w_down, conv_prev, *, tm, tf, nb):
    rows, d_model = x2.shape
    d_ff = w_gate.shape[1]
    assert rows % tm == 0 and d_ff % tf == 0 and tm >= CONV_TAIL * nb
    tail = CONV_TAIL * nb
    row_tile = pl.BlockSpec((tm, d_model), lambda i, j: (i, 0))
    chunk = lambda n: pl.BlockSpec((n, tf), lambda i, j: (0, j))
    vmem = (2 * tm * d_model * (2 + 4 + 4) + 2 * 3 * d_model * tf * 2 + (tm + 3 * tail) * tf * 4
            + d_ff // tf * tail * tf * 4 + 5 * tm * tf * 4 + tm * d_model * 4 + (6 << 20))
    return pl.pallas_call(
        functools.partial(_conv_ffn_kernel, nb=nb),
        grid=(rows // tm, d_ff // tf),
        in_specs=[row_tile, row_tile, chunk(d_model), chunk(d_model), chunk(CONV_WIDTH), chunk(1),
                  pl.BlockSpec((tf, d_model), lambda i, j: (j, 0)), chunk(tail)],
        out_specs=[row_tile, pl.BlockSpec((tail, d_ff), lambda i, j: (0, 0))],
        out_shape=[jax.ShapeDtypeStruct((rows, d_model), F32), jax.ShapeDtypeStruct((tail, d_ff), F32)],
        scratch_shapes=[pltpu.VMEM((tail + tm, tf), F32), pltpu.VMEM((d_ff // tf, tail, tf), F32)],
        compiler_params=_params(2, vmem),
        name="conv_ffn",
    )(h3, x2, w_gate, w_up, conv_w, conv_b, w_down, conv_prev)


ROPE_SPLIT = 128


def _rope_tables(pos0, n_pos):
    half = HEAD_DIM // 2
    inv = 1.0 / (ROPE_THETA ** (jnp.arange(half, dtype=F32) * (2.0 / HEAD_DIM)))
    angle = lambda pos: pos.astype(F32)[:, None] * inv[None, :]
    if pos0 % ROPE_SPLIT == 0 and n_pos % ROPE_SPLIT == 0:
        a = angle(pos0 + ROPE_SPLIT * jnp.arange(n_pos // ROPE_SPLIT))[:, None, :]
        b = angle(jnp.arange(ROPE_SPLIT))[None, :, :]
        cos = (jnp.cos(a) * jnp.cos(b) - jnp.sin(a) * jnp.sin(b)).reshape(n_pos, half)
        sin = (jnp.sin(a) * jnp.cos(b) + jnp.cos(a) * jnp.sin(b)).reshape(n_pos, half)
    else:
        ang = angle(pos0 + jnp.arange(n_pos))
        cos, sin = jnp.cos(ang), jnp.sin(ang)
    return jnp.concatenate([cos, cos], axis=-1), jnp.concatenate([-sin, sin], axis=-1)


def _row_tile(rows, want):
    return want if rows % want == 0 else rows


def _layer(x, pos0, nb, pool_prev, conv_prev, mem_k, mem_v, n_mem, attend, p):
    rows = x.shape[0]
    cos2, sin2 = (jnp.repeat(t, nb, axis=0) for t in _rope_tables(pos0, rows // nb))
    pool_out, q_hm, k_hm, v_hm, pool_state = _in_proj(
        x, cos2, sin2, pool_prev, p["norm_mix"], p["w_in"], p["q_norm"], p["k_norm"], p["w_pool"], p["pool_scale"],
        tm=_row_tile(rows, IN_PROJ_ROWS), nb=nb, pos0=pos0)
    attn_out, k_keep, v_keep = attend(q_hm, k_hm, v_hm)
    x2, h3 = _mem_mix(x, pool_out, attn_out, p["w_out"], p["norm_mem"], p["w_mem_q"], p["mem_q_norm"], mem_k, mem_v,
                      p["w_mem_o"], p["norm_ffn"], tm=_row_tile(rows, MEM_MIX_ROWS), nb=nb, n_mem=n_mem)
    y, conv_state = _conv_ffn(h3, x2, p["w_gate"], p["w_up"], p["conv_w"], p["conv_b"], p["w_down"], conv_prev,
                              tm=_row_tile(rows, FFN_ROWS), tf=FFN_COLS, nb=nb)
    return y, pool_state, k_keep, v_keep, conv_state


def kernel(x_prompt, x_sample, state_pool, cache_win_k, cache_win_v, cache_mem_k, cache_mem_v, state_conv, mem_prompt, norm_mix, w_in, q_norm, k_norm, w_pool, pool_scale, w_out, norm_mem, norm_mem_src, w_mem_q, w_mem_k, w_mem_v, mem_q_norm, mem_k_norm, w_mem_o, norm_ffn, w_gate, w_up, conv_w, conv_b, w_down):
    bp, tp, d_model = x_prompt.shape
    bs, ts, _ = x_sample.shape
    depth = w_in.shape[0]
    assert bp == 1
    pool_width = pool_scale.shape[-1]
    d_ff = w_gate.shape[-1]
    n_mem = mem_prompt.shape[1]
    n_heads = cache_win_k.shape[3]
    mem_heads = cache_mem_k.shape[3]
    attn_width = n_heads * HEAD_DIM
    keep_p = min(MAX_WINDOW, tp)
    assert keep_p == SUPER_BLOCK
    row = lambda a: a.reshape(1, -1)

    xp = x_prompt.reshape(tp, d_model)
    xs = x_sample.transpose(1, 0, 2).reshape(ts * bs, d_model)
    outs = {k: [] for k in ("p_pool", "p_k", "p_v", "p_mk", "p_mv", "p_conv", "s_pool", "s_k", "s_v", "s_conv")}
    for l in range(depth):
        prm = {
            "norm_mix": row(norm_mix[l]), "w_in": w_in[l].astype(BF16), "q_norm": row(q_norm[l]),
            "k_norm": row(k_norm[l]), "w_pool": w_pool[l].astype(BF16), "pool_scale": row(pool_scale[l]),
            "w_out": w_out[l].astype(BF16), "norm_mem": row(norm_mem[l]), "w_mem_q": w_mem_q[l].astype(BF16),
            "mem_q_norm": row(mem_q_norm[l]), "w_mem_o": w_mem_o[l].astype(BF16), "norm_ffn": row(norm_ffn[l]),
            "w_gate": w_gate[l].astype(BF16), "w_up": w_up[l].astype(BF16), "conv_w": conv_w[l],
            "conv_b": row(conv_b[l]), "w_down": w_down[l].astype(BF16),
        }
        mk, mv = _memory_kv(mem_prompt[0], row(norm_mem_src[l]), w_mem_k[l].astype(BF16), w_mem_v[l].astype(BF16),
                            row(mem_k_norm[l]))
        xp, pool_st, kp, vp, conv_st = _layer(
            xp, 0, 1, jnp.zeros((POOL_TAIL, pool_width), F32), jnp.zeros((CONV_TAIL, d_ff), F32), mk, mv, n_mem,
            _dilated_attn, prm)
        outs["p_pool"].append(pool_st[None])
        outs["p_k"].append(kp.reshape(1, keep_p, n_heads, HEAD_DIM))
        outs["p_v"].append(vp.reshape(1, keep_p, n_heads, HEAD_DIM))
        outs["p_mk"].append(mk.reshape(1, n_mem, mem_heads, MEM_HEAD_DIM))
        outs["p_mv"].append(mv.reshape(1, n_mem, mem_heads, MEM_HEAD_DIM))
        outs["p_conv"].append(conv_st[None])

        pool_prev = jnp.pad(state_pool[l].transpose(1, 0, 2), ((POOL_TAIL - POOL_STATE, 0), (0, 0), (0, 0)))
        conv_prev = state_conv[l].transpose(1, 0, 2).reshape(CONV_TAIL * bs, d_ff)

        def attend_samples(q_hm, k_hm, v_hm, l=l):
            pad = lambda a: jnp.pad(a, ((0, 0), (0, -ts % SUBLANES), (0, 0)))
            to_seq = lambda a: a.reshape(n_heads, ts, bs, HEAD_DIM).transpose(2, 1, 0, 3).reshape(bs, ts, attn_width)
            k_new, v_new = to_seq(k_hm), to_seq(v_hm)
            o = _sample_attn(pad(to_seq(q_hm)), pad(k_new), pad(v_new), cache_win_k[l], cache_win_v[l],
                             n_queries=ts, past_len=PAST_LEN)
            return o[:, :ts].transpose(1, 0, 2).reshape(ts * bs, attn_width), k_new, v_new

        xs, pool_st, ks, vs, conv_st = _layer(
            xs, PAST_LEN, bs, pool_prev.reshape(POOL_TAIL * bs, pool_width), conv_prev,
            cache_mem_k[l].reshape(bs * n_mem * mem_heads, MEM_HEAD_DIM),
            cache_mem_v[l].reshape(bs * n_mem * mem_heads, MEM_HEAD_DIM), n_mem, attend_samples, prm)
        seq_major = lambda a, t: a.reshape(t, bs, -1).transpose(1, 0, 2)
        outs["s_pool"].append(seq_major(pool_st, POOL_STATE))
        outs["s_k"].append(ks.reshape(bs, ts, n_heads, HEAD_DIM))
        outs["s_v"].append(vs.reshape(bs, ts, n_heads, HEAD_DIM))
        outs["s_conv"].append(seq_major(conv_st, CONV_TAIL))
    stack = lambda k: jnp.stack(outs[k])
    return (xp.reshape(bp, tp, d_model), xs.reshape(ts, bs, d_model).transpose(1, 0, 2),
            stack("p_pool"), stack("p_k"), stack("p_v"), stack("p_mk"), stack("p_mv"), stack("p_conv"),
            stack("s_pool"), stack("s_k"), stack("s_v"), stack("s_conv"))
```

```python
import functools
import math

import jax
import jax.numpy as jnp
from jax import lax
from jax.experimental import pallas as pl
from jax.experimental.pallas import tpu as pltpu

PAST_LEN = 16384
POOL_WINDOWS = (2, 4, 8, 16)
HEAD_DIM = 128
DILATIONS = ((128, 1), (512, 4), (2048, 16))
MAX_WINDOW = 2048
ROPE_THETA = 10000.0
MEM_HEAD_DIM = 128
CONV_WIDTH = 3
EPS = 1e-6
NEG_INF = -1e30

POOL_STATE = max(POOL_WINDOWS) - 1
POOL_TAIL = max(POOL_WINDOWS)
CONV_TAIL = CONV_WIDTH - 1
BAND = DILATIONS[0][0]
SUPER_BLOCK = MAX_WINDOW

V7X_SCOPED_VMEM_LIMIT = 60000 * 1024
LANES = 128
SUBLANES = 8

IN_PROJ_ROWS = 512
IN_PROJ_GROUP_ROWS = 256
MEM_MIX_ROWS = 512
FFN_ROWS = 512
FFN_COLS = 512

F32 = jnp.float32
BF16 = jnp.bfloat16


def _const_spec(shape):
    return pl.BlockSpec(shape, lambda *_: (0,) * len(shape), pipeline_mode=pl.Buffered(1))


def _params(n_axes, vmem_bytes):
    return pltpu.CompilerParams(
        dimension_semantics=("arbitrary",) * n_axes,
        vmem_limit_bytes=min(int(vmem_bytes), V7X_SCOPED_VMEM_LIMIT),
    )


def _rms(x, gain):
    return x * lax.rsqrt(jnp.mean(x * x, axis=-1, keepdims=True) + EPS) * gain


def _dot(a, b):
    return jnp.dot(a, b, preferred_element_type=F32)


def _dot_nt(a, b):
    return lax.dot_general(a, b, (((1,), (1,)), ((), ())), preferred_element_type=F32)


def _in_proj_kernel(x_ref, cos_ref, sin_ref, prev_ref, norm_ref, win_ref, qn_ref, kn_ref, wpool_ref, pscale_ref,
                    pool_ref, q_ref, k_ref, v_ref, pstate_ref, u_scr,
                    *, tm, nb, pos0, n_tiles, n_heads):
    i = pl.program_id(0)
    tail = POOL_TAIL * nb
    pool_width = pool_ref.shape[1]
    attn_width = n_heads * HEAD_DIM
    group = pool_width // len(POOL_WINDOWS)
    sr = min(tm, IN_PROJ_GROUP_ROWS)

    @pl.when(i == 0)
    def _():
        u_scr[0:tail, :] = prev_ref[...]

    for r0 in range(0, tm, sr):
        rows = slice(r0, r0 + sr)
        h = _rms(x_ref[rows, :], norm_ref[...]).astype(BF16)
        u_scr[tail + r0:tail + r0 + sr, :] = _dot(h, win_ref[:, 0:pool_width])
        row = i * tm + r0 + lax.broadcasted_iota(jnp.int32, (sr, 1), 0)
        pos = pos0 + (row >> int(math.log2(nb)))
        diffs = []
        for g, w in enumerate(POOL_WINDOWS):
            cols = slice(g * group, (g + 1) * group)
            u_t = u_scr[tail + r0:tail + r0 + sr, cols]
            acc = u_t
            for j in range(1, w):
                acc = acc + u_scr[tail + r0 - j * nb:tail + r0 - j * nb + sr, cols]
            cnt = jnp.minimum(pos + 1, w).astype(F32)
            diffs.append((acc / cnt - u_t).astype(BF16))

        cosv = cos_ref[rows, :]
        sinv = sin_ref[rows, :]

        def normed_rope(col0, gain_ref, hm_ref):
            proj = _dot(h, win_ref[:, col0:col0 + attn_width])
            for hh in range(n_heads):
                xn = _rms(proj[:, hh * HEAD_DIM:(hh + 1) * HEAD_DIM], gain_ref[...])
                hm_ref[hh, rows, :] = xn * cosv + pltpu.roll(xn, HEAD_DIM // 2, 1) * sinv

        normed_rope(pool_width, qn_ref, q_ref)
        normed_rope(pool_width + attn_width, kn_ref, k_ref)
        vproj = _dot(h, win_ref[:, pool_width + 2 * attn_width:pool_width + 3 * attn_width])
        for hh in range(n_heads):
            v_ref[hh, rows, :] = vproj[:, hh * HEAD_DIM:(hh + 1) * HEAD_DIM]

        for g, diff in enumerate(diffs):
            cols = slice(g * group, (g + 1) * group)
            pool_ref[rows, cols] = (_dot(diff, wpool_ref[g]) * pscale_ref[:, cols]).astype(pool_ref.dtype)

    @pl.when(i == n_tiles - 1)
    def _():
        pstate_ref[...] = u_scr[tm + nb:tm + tail, :]

    if n_tiles > 1:
        u_scr[0:tail, :] = u_scr[tm:tm + tail, :]


def _in_proj(x, cos2, sin2, pool_prev, norm, w_in, q_norm, k_norm, w_pool, pool_scale, *, tm, nb, pos0):
    rows, d_model = x.shape
    pool_width = pool_scale.shape[-1]
    attn_width = (w_in.shape[1] - pool_width) // 3
    n_heads = attn_width // HEAD_DIM
    n_tiles = rows // tm
    assert rows % tm == 0 and nb & (nb - 1) == 0 and (tm >= POOL_TAIL * nb or n_tiles == 1)
    tile = lambda width: pl.BlockSpec((tm, width), lambda i: (i, 0))
    heads = pl.BlockSpec((n_heads, tm, HEAD_DIM), lambda i: (0, i, 0))
    vmem = (2 * tm * d_model * 4 + w_in.size * 2 + 2 * 2 * tm * LANES * 4 + 2 * tm * pool_width * 2
            + 3 * 2 * tm * attn_width * 4 + (POOL_TAIL * nb + tm) * pool_width * 4
            + 2 * POOL_TAIL * nb * pool_width * 4 + 6 * tm * attn_width * 4 + tm * d_model * 6 + (4 << 20))
    hm_shape = jax.ShapeDtypeStruct((n_heads, rows, HEAD_DIM), F32)
    return pl.pallas_call(
        functools.partial(_in_proj_kernel, tm=tm, nb=nb, pos0=pos0, n_tiles=n_tiles, n_heads=n_heads),
        grid=(n_tiles,),
        in_specs=[tile(d_model), tile(HEAD_DIM), tile(HEAD_DIM), _const_spec(pool_prev.shape),
                  _const_spec(norm.shape), _const_spec(w_in.shape), _const_spec(q_norm.shape),
                  _const_spec(k_norm.shape), _const_spec(w_pool.shape), _const_spec(pool_scale.shape)],
        out_specs=[tile(pool_width), heads, heads, heads,
                   pl.BlockSpec((POOL_STATE * nb, pool_width), lambda i: (0, 0))],
        out_shape=[jax.ShapeDtypeStruct((rows, pool_width), BF16), hm_shape, hm_shape, hm_shape,
                   jax.ShapeDtypeStruct((POOL_STATE * nb, pool_width), F32)],
        scratch_shapes=[pltpu.VMEM((POOL_TAIL * nb + tm, pool_width), F32)],
        compiler_params=_params(1, vmem),
        name="in_proj",
    )(x, cos2, sin2, pool_prev, norm, w_in, q_norm, k_norm, w_pool, pool_scale)


CLASSES = 4
LOG2E = 1.4426950408889634


def _dilated_attn_kernel(q_ref, k_ref, v_ref, o_ref, klast_ref, vlast_ref,
                         yq, yk, yv, q1, k1, v1, q4, k4, v4, q16, k16, v16,
                         o1, m1, l1, o4, m4, l4, o16, m16, l16, out_f, bias):
    i = pl.program_id(1)
    sb = q_ref.shape[0]

    @pl.when(i == pl.num_programs(1) - 1)
    def _():
        klast_ref[...] = k_ref[...]
        vlast_ref[...] = v_ref[...]

    n4 = sb // CLASSES
    nk4 = BAND + n4
    nk16 = 2 * BAND
    slots = sb // BAND
    scale = HEAD_DIM ** -0.5 * LOG2E
    keys = ((k1, v1), (k4, v4), (k16, v16))
    prev_rows = ([(0, sb)], [(r * nk4, r * nk4 + n4) for r in range(CLASSES)],
                 [(s * nk16, s * nk16 + BAND) for s in range(slots)])
    val = slice(0, HEAD_DIM)

    @pl.when(i == 0)
    def _():
        for (kd, vd), rows in zip(keys, prev_rows):
            vd[:, HEAD_DIM:] = jnp.ones((vd.shape[0], HEAD_DIM), BF16)
            for dst, _ in rows:
                kd[dst:dst + BAND, :] = jnp.zeros((BAND, HEAD_DIM), BF16)
                vd[dst:dst + BAND, val] = jnp.zeros((BAND, HEAD_DIM), BF16)
        qa = lax.broadcasted_iota(jnp.int32, (BAND, 2 * BAND), 0)
        col = lax.broadcasted_iota(jnp.int32, (BAND, 2 * BAND), 1)
        in_band = (col >= qa) & (col <= qa + BAND)
        bias[0] = jnp.where(in_band, 0.0, NEG_INF)
        bias[1] = jnp.where(in_band & (col >= BAND), 0.0, NEG_INF)

    @pl.when(i > 0)
    def _():
        for (kd, vd), rows in zip(keys, prev_rows):
            for dst, src in rows:
                kd[dst:dst + BAND, :] = kd[src:src + BAND, :]
                vd[dst:dst + BAND, val] = vd[src:src + BAND, val]

    q1[...] = q_ref[...].astype(BF16)
    k1[BAND:BAND + sb, :] = k_ref[...].astype(BF16)
    v1[BAND:BAND + sb, val] = v_ref[...].astype(BF16)
    for src, y, d4, d16, lead, lanes in ((q_ref, yq, q4, q16, 0, slice(None)), (k_ref, yk, k4, k16, BAND, slice(None)),
                                        (v_ref, yv, v4, v16, BAND, val)):
        for r1 in range(CLASSES):
            rows = src[pl.ds(r1, n4, stride=CLASSES), :]
            y[r1 * n4:(r1 + 1) * n4, :] = rows
            d4[r1 * (lead + n4) + lead:(r1 + 1) * (lead + n4), lanes] = rows.astype(BF16)
        for s in range(slots):
            r1, r2 = divmod(s, CLASSES)
            rows = y[pl.ds(r1 * n4 + r2, BAND, stride=CLASSES), :]
            d16[s * (lead + BAND) + lead:(s + 1) * (lead + BAND), lanes] = rows.astype(BF16)

    first = (i == 0).astype(jnp.int32)

    def branch(qd, kd, vd, units, od, md, ld, rows_of):
        for u, (q0, k0, leads) in enumerate(units):
            s = _dot_nt(qd[q0:q0 + BAND, :], kd[k0:k0 + 2 * BAND, :]) * scale + bias[first if leads else 0]
            m = jnp.max(s, axis=-1, keepdims=True)
            ol = _dot(jnp.exp2(s - m).astype(BF16), vd[k0:k0 + 2 * BAND, :])
            od[rows_of(u), :] = ol[:, val]
            md[rows_of(u), :] = jnp.broadcast_to(m, (BAND, HEAD_DIM))
            ld[rows_of(u), :] = ol[:, HEAD_DIM:]

    blocks4 = n4 // BAND
    in_order = lambda u: slice(u * BAND, (u + 1) * BAND)
    branch(q1, k1, v1, [(u * BAND, u * BAND, u == 0) for u in range(slots)], o1, m1, l1, in_order)
    branch(q4, k4, v4, [(r * n4 + b * BAND, r * nk4 + b * BAND, b == 0)
                        for r in range(CLASSES) for b in range(blocks4)], o4, m4, l4, in_order)
    branch(q16, k16, v16, [(s * BAND, s * nk16, True) for s in range(slots)], o16, m16, l16,
           lambda s: pl.ds((s // CLASSES) * n4 + s % CLASSES, BAND, stride=CLASSES))

    for piece in range(slots):
        r1, j = divmod(piece, blocks4)
        rows = in_order(piece)
        nat = pl.ds(r1 + CLASSES * BAND * j, BAND, stride=CLASSES)
        ma, mb, mc = m1[nat, :], m4[rows, :], m16[rows, :]
        m = jnp.maximum(jnp.maximum(ma, mb), mc)
        wa, wb, wc = jnp.exp2(ma - m), jnp.exp2(mb - m), jnp.exp2(mc - m)
        num = wa * o1[nat, :] + wb * o4[rows, :] + wc * o16[rows, :]
        den = wa * l1[nat, :] + wb * l4[rows, :] + wc * l16[rows, :]
        out_f[nat, :] = num / den
    o_ref[...] = out_f[...].astype(o_ref.dtype)


def _dilated_attn(q_hm, k_hm, v_hm):
    n_heads, rows, _ = q_hm.shape
    sb = SUPER_BLOCK
    assert rows % sb == 0 and [d for _, d in DILATIONS] == [1, CLASSES, CLASSES ** 2]
    assert all(w == BAND * d for w, d in DILATIONS) and sb == BAND * CLASSES ** 2
    block = pl.BlockSpec((None, sb, HEAD_DIM), lambda h, i: (h, i, 0))
    blk = lambda n, dt, width=HEAD_DIM: pltpu.VMEM((n, width), dt)
    lead = (BAND, CLASSES * BAND, sb)
    scratch = ([blk(sb, F32)] * 3
               + [s for n in lead for s in (blk(sb, BF16), blk(n + sb, BF16), blk(n + sb, BF16, 2 * HEAD_DIM))]
               + [blk(sb, F32)] * 10
               + [pltpu.VMEM((2, BAND, 2 * BAND), F32)])
    vmem = (2 * 3 * sb * HEAD_DIM * 4 + 2 * sb * HEAD_DIM * 2 + 2 * 2 * sb * HEAD_DIM * 4 + 13 * sb * HEAD_DIM * 4
            + (12 * sb + 3 * sum(lead)) * HEAD_DIM * 2 + 2 * BAND * 2 * BAND * 4 + (12 << 20))
    last = pl.BlockSpec((sb, HEAD_DIM), lambda h, i: (0, h))
    last_shape = jax.ShapeDtypeStruct((sb, n_heads * HEAD_DIM), F32)
    return pl.pallas_call(
        _dilated_attn_kernel,
        grid=(n_heads, rows // sb),
        in_specs=[block, block, block],
        out_specs=[pl.BlockSpec((sb, HEAD_DIM), lambda h, i: (i, h)), last, last],
        out_shape=[jax.ShapeDtypeStruct((rows, n_heads * HEAD_DIM), BF16), last_shape, last_shape],
        scratch_shapes=scratch,
        compiler_params=_params(2, vmem),
        name="dilated_attn",
    )(q_hm, k_hm, v_hm)


FAR_STEP = DILATIONS[2][1]
NEAR_ROWS = DILATIONS[1][0]


def _sample_attn_kernel(q_ref, kn_ref, vn_ref, *refs, n_heads, past_len, cache, n_far):
    far_k, far_v = refs[0:n_far], refs[n_far:2 * n_far]
    near_k, near_v, o_ref = refs[2 * n_far:]
    tq = q_ref.shape[0]
    far_m = cache // FAR_STEP
    scale = HEAD_DIM ** -0.5

    def multiplicity(key, ok):
        t = lax.broadcasted_iota(jnp.int32, key.shape, 0)
        dist = cache + t - key
        seen = ok & (dist >= 0) & (past_len + t - dist >= 0)
        n = jnp.zeros(key.shape, F32)
        for window, dil in DILATIONS:
            hit = seen & (dist <= window) & ((dist & (dil - 1)) == 0)
            n = n + jnp.where(hit, 1.0, 0.0)
        return n

    lane = lax.broadcasted_iota(jnp.int32, (tq, far_m), 1)
    near_key = cache - NEAR_ROWS + lax.broadcasted_iota(jnp.int32, (tq, NEAR_ROWS), 1)
    old_key = jnp.concatenate([FAR_STEP * lane + r for r in range(n_far)] + [near_key], axis=1)
    far_col = lax.broadcasted_iota(jnp.int32, old_key.shape, 1) < n_far * far_m
    n_old = multiplicity(old_key, ~(far_col & (old_key >= cache - NEAR_ROWS)))
    n_new = multiplicity(cache + lax.broadcasted_iota(jnp.int32, (tq, tq), 1), True)

    def head_rows(far, near, hh):
        parts = [f.reshape(far_m * n_heads, HEAD_DIM)[pl.ds(hh, far_m, stride=n_heads), :] for f in far]
        parts.append(near[pl.ds(hh, NEAR_ROWS, stride=n_heads), :])
        return jnp.concatenate(parts, axis=0).astype(BF16)

    for hh in range(n_heads):
        cols = slice(hh * HEAD_DIM, (hh + 1) * HEAD_DIM)
        q = q_ref[:, cols].astype(BF16)
        s_o = jnp.where(n_old > 0, _dot_nt(q, head_rows(far_k, near_k, hh)) * scale, NEG_INF)
        s_n = jnp.where(n_new > 0, _dot_nt(q, kn_ref[:, cols].astype(BF16)) * scale, NEG_INF)
        m = jnp.maximum(jnp.max(s_o, axis=-1, keepdims=True), jnp.max(s_n, axis=-1, keepdims=True))
        p_o = n_old * jnp.exp(s_o - m)
        p_n = n_new * jnp.exp(s_n - m)
        den = jnp.sum(p_o, axis=-1, keepdims=True) + jnp.sum(p_n, axis=-1, keepdims=True)
        num = (_dot(p_o.astype(BF16), head_rows(far_v, near_v, hh))
               + _dot(p_n.astype(BF16), vn_ref[:, cols].astype(BF16)))
        o_ref[:, cols] = (num / den).astype(o_ref.dtype)


def _sample_attn(q, k_new, v_new, cache_k, cache_v, *, n_queries, past_len):
    batch, tq, width = q.shape
    _, cache, n_heads, _ = cache_k.shape
    n_far = min(n_queries, FAR_STEP)
    assert all(d & (d - 1) == 0 for _, d in DILATIONS) and cache % NEAR_ROWS == 0 and cache % FAR_STEP == 0
    assert DILATIONS[0][0] <= NEAR_ROWS and tq % SUBLANES == 0
    new = pl.BlockSpec((None, tq, width), lambda b: (b, 0, 0))
    far_view = lambda c: c.reshape(batch, cache // FAR_STEP, FAR_STEP, n_heads, HEAD_DIM)
    near_view = lambda c: c.reshape(batch, cache * n_heads, HEAD_DIM)
    far = [pl.BlockSpec((None, cache // FAR_STEP, None, n_heads, HEAD_DIM),
                        functools.partial(lambda b, r: (b, 0, r, 0, 0), r=r)) for r in range(n_far)]
    near = pl.BlockSpec((None, NEAR_ROWS * n_heads, HEAD_DIM), lambda b: (b, cache // NEAR_ROWS - 1, 0))
    vmem = 2 * 2 * (n_far * cache // FAR_STEP + NEAR_ROWS) * width * 4 + (16 << 20)
    return pl.pallas_call(
        functools.partial(_sample_attn_kernel, n_heads=n_heads, past_len=past_len, cache=cache, n_far=n_far),
        grid=(batch,),
        in_specs=[new, new, new] + far + far + [near, near],
        out_specs=new,
        out_shape=jax.ShapeDtypeStruct((batch, tq, width), BF16),
        compiler_params=_params(1, vmem),
        name="sample_attn",
    )(q, k_new, v_new, *([far_view(cache_k)] * n_far), *([far_view(cache_v)] * n_far),
      near_view(cache_k), near_view(cache_v))


def _memory_kv_kernel(mem_ref, norm_ref, wk_ref, wv_ref, kn_ref, k_ref, v_ref):
    m = _rms(mem_ref[...], norm_ref[...]).astype(BF16)
    n_mem = m.shape[0]
    k = _dot(m, wk_ref[...])
    v = _dot(m, wv_ref[...])
    heads = k.shape[1] // MEM_HEAD_DIM
    for hh in range(heads):
        cols = slice(hh * MEM_HEAD_DIM, (hh + 1) * MEM_HEAD_DIM)
        k_ref[pl.ds(hh, n_mem, stride=heads), :] = _rms(k[:, cols], kn_ref[...])
        v_ref[pl.ds(hh, n_mem, stride=heads), :] = v[:, cols]


def _memory_kv(mem, norm_src, w_k, w_v, k_norm):
    n_mem = mem.shape[0]
    heads = w_k.shape[1] // MEM_HEAD_DIM
    args = (mem, norm_src, w_k, w_v, k_norm)
    out = jax.ShapeDtypeStruct((n_mem * heads, MEM_HEAD_DIM), F32)
    return pl.pallas_call(
        _memory_kv_kernel,
        grid=(1,),
        in_specs=[_const_spec(a.shape) for a in args],
        out_specs=[pl.BlockSpec((n_mem * heads, MEM_HEAD_DIM), lambda i: (0, 0))] * 2,
        out_shape=[out, out],
        compiler_params=_params(1, 32 << 20),
        name="memory_kv",
    )(*args)


def _mem_mix_kernel(x_ref, pool_ref, attn_ref, wout_ref, nmem_ref, wq_ref, qn_ref, mk_ref, mv_ref, wo_ref, nffn_ref,
                    x2_ref, h3_ref, *, nb, n_mem):
    tm = x_ref.shape[0]
    pool_width = pool_ref.shape[1]
    x1 = x_ref[...] + _dot(pool_ref[...], wout_ref[0:pool_width, :]) + _dot(attn_ref[...], wout_ref[pool_width:, :])
    h2 = _rms(x1, nmem_ref[...]).astype(BF16)
    qm = _dot(h2, wq_ref[...])
    scale = MEM_HEAD_DIM ** -0.5
    n_heads = qm.shape[1] // MEM_HEAD_DIM
    n_keys = mk_ref.shape[0] // n_heads
    if nb > 1:
        seq_q = (pl.program_id(0) * tm + lax.broadcasted_iota(jnp.int32, (tm, n_keys), 0)) & (nb - 1)
        seq_k = lax.broadcasted_iota(jnp.int32, (tm, n_keys), 1) >> int(math.log2(n_mem))
        own = seq_q == seq_k
    heads = []
    for hh in range(n_heads):
        cols = slice(hh * MEM_HEAD_DIM, (hh + 1) * MEM_HEAD_DIM)
        head_rows = pl.ds(hh, n_keys, stride=n_heads)
        q = _rms(qm[:, cols], qn_ref[...]).astype(BF16)
        s = _dot_nt(q, mk_ref[head_rows, :].astype(BF16)) * scale
        if nb > 1:
            s = jnp.where(own, s, NEG_INF)
        p = jnp.exp(s - jnp.max(s, axis=-1, keepdims=True))
        o = _dot(p.astype(BF16), mv_ref[head_rows, :].astype(BF16)) / jnp.sum(p, axis=-1, keepdims=True)
        heads.append(o.astype(BF16))
    x2 = x1 + _dot(jnp.concatenate(heads, axis=-1), wo_ref[...])
    x2_ref[...] = x2
    h3_ref[...] = _rms(x2, nffn_ref[...]).astype(h3_ref.dtype)


def _mem_mix(x, pool_out, attn_out, w_out, norm_mem, w_q, q_norm, mem_k, mem_v, w_o, norm_ffn, *, tm, nb, n_mem):
    rows, d_model = x.shape
    n_keys = mem_k.shape[0] * MEM_HEAD_DIM // w_q.shape[1]
    assert rows % tm == 0 and nb & (nb - 1) == 0 and n_mem & (n_mem - 1) == 0 and n_keys == nb * n_mem
    tile = lambda a: pl.BlockSpec((tm, a.shape[1]), lambda i: (i, 0))
    consts = (w_out, norm_mem, w_q, q_norm, mem_k, mem_v, w_o, norm_ffn)
    vmem = (sum(c.size * c.dtype.itemsize for c in consts) + 2 * tm * d_model * (4 + 2 + 4 + 2)
            + tm * d_model * 16 + 4 * tm * n_keys * 4 + (6 << 20))
    return pl.pallas_call(
        functools.partial(_mem_mix_kernel, nb=nb, n_mem=n_mem),
        grid=(rows // tm,),
        in_specs=[tile(x), tile(pool_out), tile(attn_out)] + [_const_spec(c.shape) for c in consts],
        out_specs=[pl.BlockSpec((tm, d_model), lambda i: (i, 0))] * 2,
        out_shape=[jax.ShapeDtypeStruct((rows, d_model), F32), jax.ShapeDtypeStruct((rows, d_model), BF16)],
        compiler_params=_params(1, vmem),
        name="mem_mix",
    )(x, pool_out, attn_out, *consts)


def _conv_ffn_kernel(h_ref, x_ref, wg_ref, wu_ref, cw_ref, cb_ref, wd_ref, prev_ref, y_ref, state_ref,
                     *rest, nb):
    *rounded_refs, g_ext, g_scr = rest
    i = pl.program_id(0)
    j = pl.program_id(1)
    tm = h_ref.shape[0]
    tail = CONV_TAIL * nb
    if rounded_refs:
        for src, dst in zip((wg_ref, wu_ref, wd_ref), rounded_refs):
            dst[...] = src[...].astype(BF16)
        wg_ref, wu_ref, wd_ref = rounded_refs

    @pl.when(j == 0)
    def _():
        y_ref[...] = x_ref[...]

    @pl.when(i == 0)
    def _():
        g_ext[0:tail, :] = prev_ref[...]

    @pl.when(i > 0)
    def _():
        g_ext[0:tail, :] = g_scr[j]

    g_ext[tail:tail + tm, :] = _dot(h_ref[...], wg_ref[...])
    up = _dot(h_ref[...], wu_ref[...])
    c = cb_ref[...]
    for tap in range(CONV_WIDTH):
        c = c + g_ext[tap * nb:tap * nb + tm, :] * cw_ref[tap:tap + 1, :]
    act = (c * jax.nn.sigmoid(c) * up).astype(BF16)
    y_ref[...] += _dot(act, wd_ref[...])

    last = g_ext[tm:tm + tail, :]
    g_scr[j] = last
    state_ref[:, pl.ds(pl.multiple_of(j * last.shape[1], LANES), last.shape[1])] = last


def _conv_ffn(h3, x2, w_gate, w_up, conv_w, conv_b, w_down, conv_prev, *, tm, tf, nb):
    rows, d_model = x2.shape
    d_ff = w_gate.shape[1]
    weights = (w_gate, w_up, w_down)
    rounds = w_gate.dtype != BF16
    assert rows % tm == 0 and d_ff % tf == 0 and tm >= CONV_TAIL * nb and (rows == tm or not rounds)
    assert all(w.dtype == w_gate.dtype for w in weights)
    tail = CONV_TAIL * nb
    row_tile = pl.BlockSpec((tm, d_model), lambda i, j: (i, 0))
    chunk = lambda n: pl.BlockSpec((n, tf), lambda i, j: (0, j))
    down_chunk = pl.BlockSpec((tf, d_model), lambda i, j: (j, 0))
    w_bytes = w_gate.dtype.itemsize + (2 if rounds else 0)
    vmem = (2 * tm * d_model * (2 + 4 + 4) + 2 * 3 * d_model * tf * w_bytes + (tm + 3 * tail) * tf * 4
            + d_ff // tf * tail * tf * 4 + 5 * tm * tf * 4 + tm * d_model * 4 + (6 << 20))
    y, state, *rounded = pl.pallas_call(
        functools.partial(_conv_ffn_kernel, nb=nb),
        grid=(rows // tm, d_ff // tf),
        in_specs=[row_tile, row_tile, chunk(d_model), chunk(d_model), chunk(CONV_WIDTH), chunk(1),
                  down_chunk, chunk(tail)],
        out_specs=([row_tile, pl.BlockSpec((tail, d_ff), lambda i, j: (0, 0))]
                   + ([chunk(d_model), chunk(d_model), down_chunk] if rounds else [])),
        out_shape=([jax.ShapeDtypeStruct((rows, d_model), F32), jax.ShapeDtypeStruct((tail, d_ff), F32)]
                   + ([jax.ShapeDtypeStruct(w.shape, BF16) for w in weights] if rounds else [])),
        scratch_shapes=[pltpu.VMEM((tail + tm, tf), F32), pltpu.VMEM((d_ff // tf, tail, tf), F32)],
        compiler_params=_params(2, vmem),
        name="conv_ffn",
    )(h3, x2, w_gate, w_up, conv_w, conv_b, w_down, conv_prev)
    return y, state, tuple(rounded) if rounds else weights


ROPE_SPLIT = 128


def _rope_tables(pos0, n_pos):
    half = HEAD_DIM // 2
    inv = 1.0 / (ROPE_THETA ** (jnp.arange(half, dtype=F32) * (2.0 / HEAD_DIM)))
    angle = lambda pos: pos.astype(F32)[:, None] * inv[None, :]
    if pos0 % ROPE_SPLIT == 0 and n_pos % ROPE_SPLIT == 0:
        a = angle(pos0 + ROPE_SPLIT * jnp.arange(n_pos // ROPE_SPLIT))[:, None, :]
        b = angle(jnp.arange(ROPE_SPLIT))[None, :, :]
        cos = (jnp.cos(a) * jnp.cos(b) - jnp.sin(a) * jnp.sin(b)).reshape(n_pos, half)
        sin = (jnp.sin(a) * jnp.cos(b) + jnp.cos(a) * jnp.sin(b)).reshape(n_pos, half)
    else:
        ang = angle(pos0 + jnp.arange(n_pos))
        cos, sin = jnp.cos(ang), jnp.sin(ang)
    return jnp.concatenate([cos, cos], axis=-1), jnp.concatenate([-sin, sin], axis=-1)


def _row_tile(rows, want):
    return want if rows % want == 0 else rows


def _layer(x, pos0, nb, pool_prev, conv_prev, mem_k, mem_v, n_mem, attend, p):
    rows = x.shape[0]
    cos2, sin2 = (jnp.repeat(t, nb, axis=0) for t in _rope_tables(pos0, rows // nb))
    pool_out, q_hm, k_hm, v_hm, pool_state = _in_proj(
        x, cos2, sin2, pool_prev, p["norm_mix"], p["w_in"], p["q_norm"], p["k_norm"], p["w_pool"], p["pool_scale"],
        tm=_row_tile(rows, IN_PROJ_ROWS), nb=nb, pos0=pos0)
    attn_out, k_keep, v_keep = attend(q_hm, k_hm, v_hm)
    x2, h3 = _mem_mix(x, pool_out, attn_out, p["w_out"], p["norm_mem"], p["w_mem_q"], p["mem_q_norm"], mem_k, mem_v,
                      p["w_mem_o"], p["norm_ffn"], tm=_row_tile(rows, MEM_MIX_ROWS), nb=nb, n_mem=n_mem)
    y, conv_state, ffn_weights = _conv_ffn(h3, x2, p["w_gate"], p["w_up"], p["conv_w"], p["conv_b"], p["w_down"],
                                           conv_prev, tm=_row_tile(rows, FFN_ROWS), tf=FFN_COLS, nb=nb)
    return y, pool_state, k_keep, v_keep, conv_state, ffn_weights


def kernel(x_prompt, x_sample, state_pool, cache_win_k, cache_win_v, cache_mem_k, cache_mem_v, state_conv, mem_prompt, norm_mix, w_in, q_norm, k_norm, w_pool, pool_scale, w_out, norm_mem, norm_mem_src, w_mem_q, w_mem_k, w_mem_v, mem_q_norm, mem_k_norm, w_mem_o, norm_ffn, w_gate, w_up, conv_w, conv_b, w_down):
    bp, tp, d_model = x_prompt.shape
    bs, ts, _ = x_sample.shape
    depth = w_in.shape[0]
    assert bp == 1
    pool_width = pool_scale.shape[-1]
    d_ff = w_gate.shape[-1]
    n_mem = mem_prompt.shape[1]
    n_heads = cache_win_k.shape[3]
    mem_heads = cache_mem_k.shape[3]
    attn_width = n_heads * HEAD_DIM
    keep_p = min(MAX_WINDOW, tp)
    assert keep_p == SUPER_BLOCK
    row = lambda a: a.reshape(1, -1)

    xp = x_prompt.reshape(tp, d_model)
    xs = x_sample.transpose(1, 0, 2).reshape(ts * bs, d_model)
    outs = {k: [] for k in ("p_pool", "p_k", "p_v", "p_mk", "p_mv", "p_conv", "s_pool", "s_k", "s_v", "s_conv")}
    for l in range(depth):
        prm = {
            "norm_mix": row(norm_mix[l]), "w_in": w_in[l].astype(BF16), "q_norm": row(q_norm[l]),
            "k_norm": row(k_norm[l]), "w_pool": w_pool[l].astype(BF16), "pool_scale": row(pool_scale[l]),
            "w_out": w_out[l].astype(BF16), "norm_mem": row(norm_mem[l]), "w_mem_q": w_mem_q[l].astype(BF16),
            "mem_q_norm": row(mem_q_norm[l]), "w_mem_o": w_mem_o[l].astype(BF16), "norm_ffn": row(norm_ffn[l]),
            "w_gate": w_gate[l], "w_up": w_up[l], "conv_w": conv_w[l], "conv_b": row(conv_b[l]), "w_down": w_down[l],
        }
        pool_prev = jnp.pad(state_pool[l].transpose(1, 0, 2), ((POOL_TAIL - POOL_STATE, 0), (0, 0), (0, 0)))
        conv_prev = state_conv[l].transpose(1, 0, 2).reshape(CONV_TAIL * bs, d_ff)

        def attend_samples(q_hm, k_hm, v_hm, l=l):
            pad = lambda a: jnp.pad(a, ((0, 0), (0, -ts % SUBLANES), (0, 0)))
            to_seq = lambda a: a.reshape(n_heads, ts, bs, HEAD_DIM).transpose(2, 1, 0, 3).reshape(bs, ts, attn_width)
            k_new, v_new = to_seq(k_hm), to_seq(v_hm)
            o = _sample_attn(pad(to_seq(q_hm)), pad(k_new), pad(v_new), cache_win_k[l], cache_win_v[l],
                             n_queries=ts, past_len=PAST_LEN)
            return o[:, :ts].transpose(1, 0, 2).reshape(ts * bs, attn_width), k_new, v_new

        xs, pool_st, ks, vs, conv_st, ffn_weights = _layer(
            xs, PAST_LEN, bs, pool_prev.reshape(POOL_TAIL * bs, pool_width), conv_prev,
            cache_mem_k[l].reshape(bs * n_mem * mem_heads, MEM_HEAD_DIM),
            cache_mem_v[l].reshape(bs * n_mem * mem_heads, MEM_HEAD_DIM), n_mem, attend_samples, prm)
        seq_major = lambda a, t: a.reshape(t, bs, -1).transpose(1, 0, 2)
        outs["s_pool"].append(seq_major(pool_st, POOL_STATE))
        outs["s_k"].append(ks.reshape(bs, ts, n_heads, HEAD_DIM))
        outs["s_v"].append(vs.reshape(bs, ts, n_heads, HEAD_DIM))
        outs["s_conv"].append(seq_major(conv_st, CONV_TAIL))

        mk, mv = _memory_kv(mem_prompt[0], row(norm_mem_src[l]), w_mem_k[l].astype(BF16), w_mem_v[l].astype(BF16),
                            row(mem_k_norm[l]))
        xp, pool_st, kp, vp, conv_st, _ = _layer(
            xp, 0, 1, jnp.zeros((POOL_TAIL, pool_width), F32), jnp.zeros((CONV_TAIL, d_ff), F32), mk, mv, n_mem,
            _dilated_attn, dict(prm, **dict(zip(("w_gate", "w_up", "w_down"), ffn_weights))))
        outs["p_pool"].append(pool_st[None])
        outs["p_k"].append(kp.reshape(1, keep_p, n_heads, HEAD_DIM))
        outs["p_v"].append(vp.reshape(1, keep_p, n_heads, HEAD_DIM))
        outs["p_mk"].append(mk.reshape(1, n_mem, mem_heads, MEM_HEAD_DIM))
        outs["p_mv"].append(mv.reshape(1, n_mem, mem_heads, MEM_HEAD_DIM))
        outs["p_conv"].append(conv_st[None])
    stack = lambda k: jnp.stack(outs[k])
    return (xp.reshape(bp, tp, d_model), xs.reshape(ts, bs, d_model).transpose(1, 0, 2),
            stack("p_pool"), stack("p_k"), stack("p_v"), stack("p_mk"), stack("p_mv"), stack("p_conv"),
            stack("s_pool"), stack("s_k"), stack("s_v"), stack("s_conv"))
```

```python
import functools
import math

import jax
import jax.numpy as jnp
from jax import lax
from jax.experimental import pallas as pl
from jax.experimental.pallas import tpu as pltpu

PAST_LEN = 16384
POOL_WINDOWS = (2, 4, 8, 16)
HEAD_DIM = 128
DILATIONS = ((128, 1), (512, 4), (2048, 16))
MAX_WINDOW = 2048
ROPE_THETA = 10000.0
MEM_HEAD_DIM = 128
CONV_WIDTH = 3
EPS = 1e-6
NEG_INF = -1e30

POOL_STATE = max(POOL_WINDOWS) - 1
POOL_TAIL = max(POOL_WINDOWS)
CONV_TAIL = CONV_WIDTH - 1
BAND = DILATIONS[0][0]
SUPER_BLOCK = MAX_WINDOW

V7X_SCOPED_VMEM_LIMIT = 60000 * 1024
LANES = 128
SUBLANES = 8

IN_PROJ_ROWS = 512
IN_PROJ_GROUP_ROWS = 256
MEM_MIX_ROWS = 512
FFN_ROWS = 512
FFN_COLS = 512

F32 = jnp.float32
BF16 = jnp.bfloat16


def _const_spec(shape):
    return pl.BlockSpec(shape, lambda *_: (0,) * len(shape), pipeline_mode=pl.Buffered(1))


def _params(n_axes, vmem_bytes):
    return pltpu.CompilerParams(
        dimension_semantics=("arbitrary",) * n_axes,
        vmem_limit_bytes=min(int(vmem_bytes), V7X_SCOPED_VMEM_LIMIT),
    )


def _rms(x, gain):
    return x * lax.rsqrt(jnp.mean(x * x, axis=-1, keepdims=True) + EPS) * gain


def _dot(a, b):
    return jnp.dot(a, b, preferred_element_type=F32)


def _dot_nt(a, b):
    return lax.dot_general(a, b, (((1,), (1,)), ((), ())), preferred_element_type=F32)


def _in_proj_kernel(x_ref, cos_ref, sin_ref, prev_ref, norm_ref, win_ref, qn_ref, kn_ref, wpool_ref, pscale_ref,
                    pool_ref, q_ref, k_ref, v_ref, pstate_ref, u_scr,
                    *, tm, nb, pos0, n_tiles, n_heads):
    i = pl.program_id(0)
    tail = POOL_TAIL * nb
    pool_width = pool_ref.shape[1]
    attn_width = n_heads * HEAD_DIM
    group = pool_width // len(POOL_WINDOWS)
    sr = min(tm, IN_PROJ_GROUP_ROWS)

    @pl.when(i == 0)
    def _():
        u_scr[0:tail, :] = prev_ref[...]

    for r0 in range(0, tm, sr):
        rows = slice(r0, r0 + sr)
        h = _rms(x_ref[rows, :], norm_ref[...]).astype(BF16)
        u_scr[tail + r0:tail + r0 + sr, :] = _dot(h, win_ref[:, 0:pool_width])
        row = i * tm + r0 + lax.broadcasted_iota(jnp.int32, (sr, 1), 0)
        pos = pos0 + (row >> int(math.log2(nb)))
        diffs = []
        for g, w in enumerate(POOL_WINDOWS):
            cols = slice(g * group, (g + 1) * group)
            u_t = u_scr[tail + r0:tail + r0 + sr, cols]
            acc = u_t
            for j in range(1, w):
                acc = acc + u_scr[tail + r0 - j * nb:tail + r0 - j * nb + sr, cols]
            cnt = jnp.minimum(pos + 1, w).astype(F32)
            diffs.append((acc / cnt - u_t).astype(BF16))

        cosv = jnp.concatenate([cos_ref[rows, :]] * 2, axis=-1)
        sinv = jnp.concatenate([-sin_ref[rows, :], sin_ref[rows, :]], axis=-1)

        def normed_rope(col0, gain_ref, hm_ref):
            proj = _dot(h, win_ref[:, col0:col0 + attn_width])
            for hh in range(n_heads):
                xn = _rms(proj[:, hh * HEAD_DIM:(hh + 1) * HEAD_DIM], gain_ref[...])
                hm_ref[hh, rows, :] = xn * cosv + pltpu.roll(xn, HEAD_DIM // 2, 1) * sinv

        normed_rope(pool_width, qn_ref, q_ref)
        normed_rope(pool_width + attn_width, kn_ref, k_ref)
        vproj = _dot(h, win_ref[:, pool_width + 2 * attn_width:pool_width + 3 * attn_width])
        for hh in range(n_heads):
            v_ref[hh, rows, :] = vproj[:, hh * HEAD_DIM:(hh + 1) * HEAD_DIM]

        for g, diff in enumerate(diffs):
            cols = slice(g * group, (g + 1) * group)
            pool_ref[rows, cols] = (_dot(diff, wpool_ref[g]) * pscale_ref[:, cols]).astype(pool_ref.dtype)

    @pl.when(i == n_tiles - 1)
    def _():
        pstate_ref[...] = u_scr[tm + nb:tm + tail, :]

    if n_tiles > 1:
        u_scr[0:tail, :] = u_scr[tm:tm + tail, :]


def _in_proj(x, cos2, sin2, pool_prev, norm, w_in, q_norm, k_norm, w_pool, pool_scale, *, tm, nb, pos0):
    rows, d_model = x.shape
    pool_width = pool_scale.shape[-1]
    attn_width = (w_in.shape[1] - pool_width) // 3
    n_heads = attn_width // HEAD_DIM
    n_tiles = rows // tm
    assert rows % tm == 0 and nb & (nb - 1) == 0 and (tm >= POOL_TAIL * nb or n_tiles == 1)
    tile = lambda width: pl.BlockSpec((tm, width), lambda i: (i, 0))
    heads = pl.BlockSpec((n_heads, tm, HEAD_DIM), lambda i: (0, i, 0))
    vmem = (2 * tm * d_model * 4 + w_in.size * 2 + 2 * 2 * tm * LANES * 4 + 2 * tm * pool_width * 2
            + 3 * 2 * tm * attn_width * 4 + (POOL_TAIL * nb + tm) * pool_width * 4
            + 2 * POOL_TAIL * nb * pool_width * 4 + 6 * tm * attn_width * 4 + tm * d_model * 6 + (4 << 20))
    hm_shape = jax.ShapeDtypeStruct((n_heads, rows, HEAD_DIM), F32)
    return pl.pallas_call(
        functools.partial(_in_proj_kernel, tm=tm, nb=nb, pos0=pos0, n_tiles=n_tiles, n_heads=n_heads),
        grid=(n_tiles,),
        in_specs=[tile(d_model), tile(HEAD_DIM // 2), tile(HEAD_DIM // 2), _const_spec(pool_prev.shape),
                  _const_spec(norm.shape), _const_spec(w_in.shape), _const_spec(q_norm.shape),
                  _const_spec(k_norm.shape), _const_spec(w_pool.shape), _const_spec(pool_scale.shape)],
        out_specs=[tile(pool_width), heads, heads, heads,
                   pl.BlockSpec((POOL_STATE * nb, pool_width), lambda i: (0, 0))],
        out_shape=[jax.ShapeDtypeStruct((rows, pool_width), BF16), hm_shape, hm_shape, hm_shape,
                   jax.ShapeDtypeStruct((POOL_STATE * nb, pool_width), F32)],
        scratch_shapes=[pltpu.VMEM((POOL_TAIL * nb + tm, pool_width), F32)],
        compiler_params=_params(1, vmem),
        name="in_proj",
    )(x, cos2, sin2, pool_prev, norm, w_in, q_norm, k_norm, w_pool, pool_scale)


CLASSES = 4
LOG2E = 1.4426950408889634


def _dilated_attn_kernel(q_ref, k_ref, v_ref, o_ref, klast_ref, vlast_ref,
                         yq, yk, yv, q1, k1, v1, q4, k4, v4, q16, k16, v16,
                         o1, m1, l1, o4, m4, l4, o16, m16, l16, out_f, bias):
    i = pl.program_id(1)
    sb = q_ref.shape[0]

    @pl.when(i == pl.num_programs(1) - 1)
    def _():
        klast_ref[...] = k_ref[...]
        vlast_ref[...] = v_ref[...]

    n4 = sb // CLASSES
    nk4 = BAND + n4
    nk16 = 2 * BAND
    slots = sb // BAND
    scale = HEAD_DIM ** -0.5 * LOG2E
    keys = ((k1, v1), (k4, v4), (k16, v16))
    prev_rows = ([(0, sb)], [(r * nk4, r * nk4 + n4) for r in range(CLASSES)],
                 [(s * nk16, s * nk16 + BAND) for s in range(slots)])
    val = slice(0, HEAD_DIM)

    @pl.when(i == 0)
    def _():
        for (kd, vd), rows in zip(keys, prev_rows):
            vd[:, HEAD_DIM:] = jnp.ones((vd.shape[0], HEAD_DIM), BF16)
            for dst, _ in rows:
                kd[dst:dst + BAND, :] = jnp.zeros((BAND, HEAD_DIM), BF16)
                vd[dst:dst + BAND, val] = jnp.zeros((BAND, HEAD_DIM), BF16)
        qa = lax.broadcasted_iota(jnp.int32, (BAND, 2 * BAND), 0)
        col = lax.broadcasted_iota(jnp.int32, (BAND, 2 * BAND), 1)
        in_band = (col >= qa) & (col <= qa + BAND)
        bias[0] = jnp.where(in_band, 0.0, NEG_INF)
        bias[1] = jnp.where(in_band & (col >= BAND), 0.0, NEG_INF)

    @pl.when(i > 0)
    def _():
        for (kd, vd), rows in zip(keys, prev_rows):
            for dst, src in rows:
                kd[dst:dst + BAND, :] = kd[src:src + BAND, :]
                vd[dst:dst + BAND, val] = vd[src:src + BAND, val]

    q1[...] = q_ref[...].astype(BF16)
    k1[BAND:BAND + sb, :] = k_ref[...].astype(BF16)
    v1[BAND:BAND + sb, val] = v_ref[...].astype(BF16)
    for src, y, d4, d16, lead, lanes in ((q_ref, yq, q4, q16, 0, slice(None)), (k_ref, yk, k4, k16, BAND, slice(None)),
                                        (v_ref, yv, v4, v16, BAND, val)):
        for r1 in range(CLASSES):
            rows = src[pl.ds(r1, n4, stride=CLASSES), :]
            y[r1 * n4:(r1 + 1) * n4, :] = rows
            d4[r1 * (lead + n4) + lead:(r1 + 1) * (lead + n4), lanes] = rows.astype(BF16)
        for s in range(slots):
            r1, r2 = divmod(s, CLASSES)
            rows = y[pl.ds(r1 * n4 + r2, BAND, stride=CLASSES), :]
            d16[s * (lead + BAND) + lead:(s + 1) * (lead + BAND), lanes] = rows.astype(BF16)

    first = (i == 0).astype(jnp.int32)

    def branch(qd, kd, vd, units, od, md, ld, rows_of):
        for u, (q0, k0, leads) in enumerate(units):
            s = _dot_nt(qd[q0:q0 + BAND, :], kd[k0:k0 + 2 * BAND, :]) * scale + bias[first if leads else 0]
            m = jnp.max(s, axis=-1, keepdims=True)
            ol = _dot(jnp.exp2(s - m).astype(BF16), vd[k0:k0 + 2 * BAND, :])
            od[rows_of(u), :] = ol[:, val]
            md[rows_of(u), :] = jnp.broadcast_to(m, (BAND, HEAD_DIM))
            ld[rows_of(u), :] = ol[:, HEAD_DIM:]

    blocks4 = n4 // BAND
    in_order = lambda u: slice(u * BAND, (u + 1) * BAND)
    branch(q1, k1, v1, [(u * BAND, u * BAND, u == 0) for u in range(slots)], o1, m1, l1, in_order)
    branch(q4, k4, v4, [(r * n4 + b * BAND, r * nk4 + b * BAND, b == 0)
                        for r in range(CLASSES) for b in range(blocks4)], o4, m4, l4, in_order)
    branch(q16, k16, v16, [(s * BAND, s * nk16, True) for s in range(slots)], o16, m16, l16,
           lambda s: pl.ds((s // CLASSES) * n4 + s % CLASSES, BAND, stride=CLASSES))

    for piece in range(slots):
        r1, j = divmod(piece, blocks4)
        rows = in_order(piece)
        nat = pl.ds(r1 + CLASSES * BAND * j, BAND, stride=CLASSES)
        ma, mb, mc = m1[nat, :], m4[rows, :], m16[rows, :]
        m = jnp.maximum(jnp.maximum(ma, mb), mc)
        wa, wb, wc = jnp.exp2(ma - m), jnp.exp2(mb - m), jnp.exp2(mc - m)
        num = wa * o1[nat, :] + wb * o4[rows, :] + wc * o16[rows, :]
        den = wa * l1[nat, :] + wb * l4[rows, :] + wc * l16[rows, :]
        out_f[nat, :] = num / den
    o_ref[...] = out_f[...].astype(o_ref.dtype)


def _dilated_attn(q_hm, k_hm, v_hm):
    n_heads, rows, _ = q_hm.shape
    sb = SUPER_BLOCK
    assert rows % sb == 0 and [d for _, d in DILATIONS] == [1, CLASSES, CLASSES ** 2]
    assert all(w == BAND * d for w, d in DILATIONS) and sb == BAND * CLASSES ** 2
    block = pl.BlockSpec((None, sb, HEAD_DIM), lambda h, i: (h, i, 0))
    blk = lambda n, dt, width=HEAD_DIM: pltpu.VMEM((n, width), dt)
    lead = (BAND, CLASSES * BAND, sb)
    scratch = ([blk(sb, F32)] * 3
               + [s for n in lead for s in (blk(sb, BF16), blk(n + sb, BF16), blk(n + sb, BF16, 2 * HEAD_DIM))]
               + [blk(sb, F32)] * 10
               + [pltpu.VMEM((2, BAND, 2 * BAND), F32)])
    vmem = (2 * 3 * sb * HEAD_DIM * 4 + 2 * sb * HEAD_DIM * 2 + 2 * 2 * sb * HEAD_DIM * 4 + 13 * sb * HEAD_DIM * 4
            + (12 * sb + 3 * sum(lead)) * HEAD_DIM * 2 + 2 * BAND * 2 * BAND * 4 + (12 << 20))
    last = pl.BlockSpec((sb, HEAD_DIM), lambda h, i: (0, h))
    last_shape = jax.ShapeDtypeStruct((sb, n_heads * HEAD_DIM), F32)
    return pl.pallas_call(
        _dilated_attn_kernel,
        grid=(n_heads, rows // sb),
        in_specs=[block, block, block],
        out_specs=[pl.BlockSpec((sb, HEAD_DIM), lambda h, i: (i, h)), last, last],
        out_shape=[jax.ShapeDtypeStruct((rows, n_heads * HEAD_DIM), BF16), last_shape, last_shape],
        scratch_shapes=scratch,
        compiler_params=_params(2, vmem),
        name="dilated_attn",
    )(q_hm, k_hm, v_hm)


FAR_STEP = DILATIONS[2][1]
NEAR_ROWS = DILATIONS[1][0]


def _sample_attn_kernel(q_ref, kn_ref, vn_ref, *refs, n_heads, past_len, cache, n_far):
    far_k, far_v = refs[0:n_far], refs[n_far:2 * n_far]
    near_k, near_v, o_ref = refs[2 * n_far:]
    tq = q_ref.shape[0]
    far_m = (cache - NEAR_ROWS) // FAR_STEP
    scale = HEAD_DIM ** -0.5

    def multiplicity(key, ok):
        t = lax.broadcasted_iota(jnp.int32, key.shape, 0)
        dist = cache + t - key
        seen = ok & (dist >= 0) & (past_len + t - dist >= 0)
        n = jnp.zeros(key.shape, F32)
        for window, dil in DILATIONS:
            hit = seen & (dist <= window) & ((dist & (dil - 1)) == 0)
            n = n + jnp.where(hit, 1.0, 0.0)
        return n

    col = lax.broadcasted_iota(jnp.int32, (tq, n_far * far_m + NEAR_ROWS), 1)
    cls = sum((col >= r * far_m).astype(jnp.int32) for r in range(1, n_far))
    far_key = FAR_STEP * (col - cls * far_m) + cls
    old_key = jnp.where(col < n_far * far_m, far_key, cache - NEAR_ROWS + col - n_far * far_m)
    n_old = multiplicity(old_key, True)
    n_new = multiplicity(cache + lax.broadcasted_iota(jnp.int32, (tq, tq), 1), True)

    def head_rows(far, near, hh):
        parts = [f.reshape(far_m * n_heads, HEAD_DIM)[pl.ds(hh, far_m, stride=n_heads), :] for f in far]
        parts.append(near[pl.ds(hh, NEAR_ROWS, stride=n_heads), :])
        return jnp.concatenate(parts, axis=0).astype(BF16)

    for hh in range(n_heads):
        cols = slice(hh * HEAD_DIM, (hh + 1) * HEAD_DIM)
        q = q_ref[:, cols].astype(BF16)
        s_o = jnp.where(n_old > 0, _dot_nt(q, head_rows(far_k, near_k, hh)) * scale, NEG_INF)
        s_n = jnp.where(n_new > 0, _dot_nt(q, kn_ref[:, cols].astype(BF16)) * scale, NEG_INF)
        m = jnp.maximum(jnp.max(s_o, axis=-1, keepdims=True), jnp.max(s_n, axis=-1, keepdims=True))
        p_o = n_old * jnp.exp(s_o - m)
        p_n = n_new * jnp.exp(s_n - m)
        den = jnp.sum(p_o, axis=-1, keepdims=True) + jnp.sum(p_n, axis=-1, keepdims=True)
        num = (_dot(p_o.astype(BF16), head_rows(far_v, near_v, hh))
               + _dot(p_n.astype(BF16), vn_ref[:, cols].astype(BF16)))
        o_ref[:, cols] = (num / den).astype(o_ref.dtype)


def _sample_attn(q, k_new, v_new, cache_k, cache_v, *, n_queries, past_len):
    batch, tq, width = q.shape
    _, cache, n_heads, _ = cache_k.shape
    far_m = (cache - NEAR_ROWS) // FAR_STEP
    n_far = min(n_queries, FAR_STEP) if far_m else 0
    assert all(d & (d - 1) == 0 for _, d in DILATIONS) and cache % NEAR_ROWS == 0 and cache % FAR_STEP == 0
    assert DILATIONS[0][0] <= NEAR_ROWS and tq % SUBLANES == 0 and NEAR_ROWS % FAR_STEP == 0
    new = pl.BlockSpec((None, tq, width), lambda b: (b, 0, 0))
    far_view = lambda c: c.reshape(batch, cache // FAR_STEP, FAR_STEP, n_heads, HEAD_DIM)
    near_view = lambda c: c.reshape(batch, cache * n_heads, HEAD_DIM)
    far = [pl.BlockSpec((None, far_m, None, n_heads, HEAD_DIM),
                        functools.partial(lambda b, r: (b, 0, r, 0, 0), r=r)) for r in range(n_far)]
    near = pl.BlockSpec((None, NEAR_ROWS * n_heads, HEAD_DIM), lambda b: (b, cache // NEAR_ROWS - 1, 0))
    vmem = 2 * 2 * (n_far * far_m + NEAR_ROWS) * width * 4 + (16 << 20)
    return pl.pallas_call(
        functools.partial(_sample_attn_kernel, n_heads=n_heads, past_len=past_len, cache=cache, n_far=n_far),
        grid=(batch,),
        in_specs=[new, new, new] + far + far + [near, near],
        out_specs=new,
        out_shape=jax.ShapeDtypeStruct((batch, tq, width), BF16),
        compiler_params=_params(1, vmem),
        name="sample_attn",
    )(q, k_new, v_new, *([far_view(cache_k)] * n_far), *([far_view(cache_v)] * n_far),
      near_view(cache_k), near_view(cache_v))


def _memory_kv_kernel(mem_ref, norm_ref, wk_ref, wv_ref, kn_ref, k_ref, v_ref):
    m = _rms(mem_ref[...], norm_ref[...]).astype(BF16)
    n_mem = m.shape[0]
    k = _dot(m, wk_ref[...])
    v = _dot(m, wv_ref[...])
    heads = k.shape[1] // MEM_HEAD_DIM
    for hh in range(heads):
        cols = slice(hh * MEM_HEAD_DIM, (hh + 1) * MEM_HEAD_DIM)
        k_ref[pl.ds(hh, n_mem, stride=heads), :] = _rms(k[:, cols], kn_ref[...])
        v_ref[pl.ds(hh, n_mem, stride=heads), :] = v[:, cols]


def _memory_kv(mem, norm_src, w_k, w_v, k_norm):
    n_mem = mem.shape[0]
    heads = w_k.shape[1] // MEM_HEAD_DIM
    args = (mem, norm_src, w_k, w_v, k_norm)
    out = jax.ShapeDtypeStruct((n_mem * heads, MEM_HEAD_DIM), F32)
    return pl.pallas_call(
        _memory_kv_kernel,
        grid=(1,),
        in_specs=[_const_spec(a.shape) for a in args],
        out_specs=[pl.BlockSpec((n_mem * heads, MEM_HEAD_DIM), lambda i: (0, 0))] * 2,
        out_shape=[out, out],
        compiler_params=_params(1, 32 << 20),
        name="memory_kv",
    )(*args)


def _mem_mix_kernel(x_ref, pool_ref, attn_ref, wout_ref, nmem_ref, wq_ref, qn_ref, mk_ref, mv_ref, wo_ref, nffn_ref,
                    x2_ref, h3_ref, *, nb, n_mem):
    tm = x_ref.shape[0]
    pool_width = pool_ref.shape[1]
    x1 = x_ref[...] + _dot(pool_ref[...], wout_ref[0:pool_width, :]) + _dot(attn_ref[...], wout_ref[pool_width:, :])
    h2 = _rms(x1, nmem_ref[...]).astype(BF16)
    qm = _dot(h2, wq_ref[...])
    scale = MEM_HEAD_DIM ** -0.5
    n_heads = qm.shape[1] // MEM_HEAD_DIM
    n_keys = mk_ref.shape[0] // n_heads
    if nb > 1:
        seq_q = (pl.program_id(0) * tm + lax.broadcasted_iota(jnp.int32, (tm, n_keys), 0)) & (nb - 1)
        seq_k = lax.broadcasted_iota(jnp.int32, (tm, n_keys), 1) >> int(math.log2(n_mem))
        own = seq_q == seq_k
    heads = []
    for hh in range(n_heads):
        cols = slice(hh * MEM_HEAD_DIM, (hh + 1) * MEM_HEAD_DIM)
        head_rows = pl.ds(hh, n_keys, stride=n_heads)
        q = _rms(qm[:, cols], qn_ref[...]).astype(BF16)
        s = _dot_nt(q, mk_ref[head_rows, :].astype(BF16)) * scale
        if nb > 1:
            s = jnp.where(own, s, NEG_INF)
        p = jnp.exp(s - jnp.max(s, axis=-1, keepdims=True))
        o = _dot(p.astype(BF16), mv_ref[head_rows, :].astype(BF16)) / jnp.sum(p, axis=-1, keepdims=True)
        heads.append(o.astype(BF16))
    x2 = x1 + _dot(jnp.concatenate(heads, axis=-1), wo_ref[...])
    x2_ref[...] = x2
    h3_ref[...] = _rms(x2, nffn_ref[...]).astype(h3_ref.dtype)


def _mem_mix(x, pool_out, attn_out, w_out, norm_mem, w_q, q_norm, mem_k, mem_v, w_o, norm_ffn, *, tm, nb, n_mem):
    rows, d_model = x.shape
    n_keys = mem_k.shape[0] * MEM_HEAD_DIM // w_q.shape[1]
    assert rows % tm == 0 and nb & (nb - 1) == 0 and n_mem & (n_mem - 1) == 0 and n_keys == nb * n_mem
    tile = lambda a: pl.BlockSpec((tm, a.shape[1]), lambda i: (i, 0))
    consts = (w_out, norm_mem, w_q, q_norm, mem_k, mem_v, w_o, norm_ffn)
    vmem = (sum(c.size * c.dtype.itemsize for c in consts) + 2 * tm * d_model * (4 + 2 + 4 + 2)
            + tm * d_model * 16 + 4 * tm * n_keys * 4 + (6 << 20))
    return pl.pallas_call(
        functools.partial(_mem_mix_kernel, nb=nb, n_mem=n_mem),
        grid=(rows // tm,),
        in_specs=[tile(x), tile(pool_out), tile(attn_out)] + [_const_spec(c.shape) for c in consts],
        out_specs=[pl.BlockSpec((tm, d_model), lambda i: (i, 0))] * 2,
        out_shape=[jax.ShapeDtypeStruct((rows, d_model), F32), jax.ShapeDtypeStruct((rows, d_model), BF16)],
        compiler_params=_params(1, vmem),
        name="mem_mix",
    )(x, pool_out, attn_out, *consts)


def _conv_ffn_kernel(h_ref, x_ref, wg_ref, wu_ref, cw_ref, cb_ref, wd_ref, prev_ref, y_ref, state_ref,
                     *rest, nb):
    *rounded_refs, g_ext, g_scr = rest
    i = pl.program_id(0)
    j = pl.program_id(1)
    tm = h_ref.shape[0]
    tail = CONV_TAIL * nb
    if rounded_refs:
        for src, dst in zip((wg_ref, wu_ref, wd_ref), rounded_refs):
            dst[...] = src[...].astype(BF16)
        wg_ref, wu_ref, wd_ref = rounded_refs

    @pl.when(j == 0)
    def _():
        y_ref[...] = x_ref[...]

    @pl.when(i == 0)
    def _():
        g_ext[0:tail, :] = prev_ref[...]

    @pl.when(i > 0)
    def _():
        g_ext[0:tail, :] = g_scr[j]

    g_ext[tail:tail + tm, :] = _dot(h_ref[...], wg_ref[...])
    up = _dot(h_ref[...], wu_ref[...])
    c = cb_ref[...]
    for tap in range(CONV_WIDTH):
        c = c + g_ext[tap * nb:tap * nb + tm, :] * cw_ref[tap:tap + 1, :]
    act = (c * jax.nn.sigmoid(c) * up).astype(BF16)
    y_ref[...] += _dot(act, wd_ref[...])

    last = g_ext[tm:tm + tail, :]
    g_scr[j] = last
    state_ref[:, pl.ds(pl.multiple_of(j * last.shape[1], LANES), last.shape[1])] = last


def _conv_ffn(h3, x2, w_gate, w_up, conv_w, conv_b, w_down, conv_prev, *, tm, tf, nb):
    rows, d_model = x2.shape
    d_ff = w_gate.shape[1]
    weights = (w_gate, w_up, w_down)
    rounds = w_gate.dtype != BF16
    assert rows % tm == 0 and d_ff % tf == 0 and tm >= CONV_TAIL * nb and (rows == tm or not rounds)
    assert all(w.dtype == w_gate.dtype for w in weights)
    tail = CONV_TAIL * nb
    row_tile = pl.BlockSpec((tm, d_model), lambda i, j: (i, 0))
    chunk = lambda n: pl.BlockSpec((n, tf), lambda i, j: (0, j))
    down_chunk = pl.BlockSpec((tf, d_model), lambda i, j: (j, 0))
    w_bytes = w_gate.dtype.itemsize + (2 if rounds else 0)
    vmem = (2 * tm * d_model * (2 + 4 + 4) + 2 * 3 * d_model * tf * w_bytes + (tm + 3 * tail) * tf * 4
            + d_ff // tf * tail * tf * 4 + 5 * tm * tf * 4 + tm * d_model * 4 + (6 << 20))
    y, state, *rounded = pl.pallas_call(
        functools.partial(_conv_ffn_kernel, nb=nb),
        grid=(rows // tm, d_ff // tf),
        in_specs=[row_tile, row_tile, chunk(d_model), chunk(d_model), chunk(CONV_WIDTH), chunk(1),
                  down_chunk, chunk(tail)],
        out_specs=([row_tile, pl.BlockSpec((tail, d_ff), lambda i, j: (0, 0))]
                   + ([chunk(d_model), chunk(d_model), down_chunk] if rounds else [])),
        out_shape=([jax.ShapeDtypeStruct((rows, d_model), F32), jax.ShapeDtypeStruct((tail, d_ff), F32)]
                   + ([jax.ShapeDtypeStruct(w.shape, BF16) for w in weights] if rounds else [])),
        scratch_shapes=[pltpu.VMEM((tail + tm, tf), F32), pltpu.VMEM((d_ff // tf, tail, tf), F32)],
        compiler_params=_params(2, vmem),
        name="conv_ffn",
    )(h3, x2, w_gate, w_up, conv_w, conv_b, w_down, conv_prev)
    return y, state, tuple(rounded) if rounds else weights


ROPE_SPLIT = 128


def _rope_tables(pos0, n_pos):
    half = HEAD_DIM // 2
    inv = 1.0 / (ROPE_THETA ** (jnp.arange(half, dtype=F32) * (2.0 / HEAD_DIM)))
    angle = lambda pos: pos.astype(F32)[:, None] * inv[None, :]
    if pos0 % ROPE_SPLIT == 0 and n_pos % ROPE_SPLIT == 0:
        a = angle(pos0 + ROPE_SPLIT * jnp.arange(n_pos // ROPE_SPLIT))[:, None, :]
        b = angle(jnp.arange(ROPE_SPLIT))[None, :, :]
        cos = (jnp.cos(a) * jnp.cos(b) - jnp.sin(a) * jnp.sin(b)).reshape(n_pos, half)
        sin = (jnp.sin(a) * jnp.cos(b) + jnp.cos(a) * jnp.sin(b)).reshape(n_pos, half)
    else:
        ang = angle(pos0 + jnp.arange(n_pos))
        cos, sin = jnp.cos(ang), jnp.sin(ang)
    return cos, sin


def _row_tile(rows, want):
    return want if rows % want == 0 else rows


def _layer(x, pos0, nb, pool_prev, conv_prev, mem_k, mem_v, n_mem, attend, p):
    rows = x.shape[0]
    cos2, sin2 = (jnp.repeat(t, nb, axis=0) for t in _rope_tables(pos0, rows // nb))
    pool_out, q_hm, k_hm, v_hm, pool_state = _in_proj(
        x, cos2, sin2, pool_prev, p["norm_mix"], p["w_in"], p["q_norm"], p["k_norm"], p["w_pool"], p["pool_scale"],
        tm=_row_tile(rows, IN_PROJ_ROWS), nb=nb, pos0=pos0)
    attn_out, k_keep, v_keep = attend(q_hm, k_hm, v_hm)
    x2, h3 = _mem_mix(x, pool_out, attn_out, p["w_out"], p["norm_mem"], p["w_mem_q"], p["mem_q_norm"], mem_k, mem_v,
                      p["w_mem_o"], p["norm_ffn"], tm=_row_tile(rows, MEM_MIX_ROWS), nb=nb, n_mem=n_mem)
    y, conv_state, ffn_weights = _conv_ffn(h3, x2, p["w_gate"], p["w_up"], p["conv_w"], p["conv_b"], p["w_down"],
                                           conv_prev, tm=_row_tile(rows, FFN_ROWS), tf=FFN_COLS, nb=nb)
    return y, pool_state, k_keep, v_keep, conv_state, ffn_weights


def kernel(x_prompt, x_sample, state_pool, cache_win_k, cache_win_v, cache_mem_k, cache_mem_v, state_conv, mem_prompt, norm_mix, w_in, q_norm, k_norm, w_pool, pool_scale, w_out, norm_mem, norm_mem_src, w_mem_q, w_mem_k, w_mem_v, mem_q_norm, mem_k_norm, w_mem_o, norm_ffn, w_gate, w_up, conv_w, conv_b, w_down):
    bp, tp, d_model = x_prompt.shape
    bs, ts, _ = x_sample.shape
    depth = w_in.shape[0]
    assert bp == 1
    pool_width = pool_scale.shape[-1]
    d_ff = w_gate.shape[-1]
    n_mem = mem_prompt.shape[1]
    n_heads = cache_win_k.shape[3]
    mem_heads = cache_mem_k.shape[3]
    attn_width = n_heads * HEAD_DIM
    keep_p = min(MAX_WINDOW, tp)
    assert keep_p == SUPER_BLOCK
    row = lambda a: a.reshape(1, -1)

    xp = x_prompt.reshape(tp, d_model)
    xs = x_sample.transpose(1, 0, 2).reshape(ts * bs, d_model)
    outs = {k: [] for k in ("p_pool", "p_k", "p_v", "p_mk", "p_mv", "p_conv", "s_pool", "s_k", "s_v", "s_conv")}
    for l in range(depth):
        prm = {
            "norm_mix": row(norm_mix[l]), "w_in": w_in[l].astype(BF16), "q_norm": row(q_norm[l]),
            "k_norm": row(k_norm[l]), "w_pool": w_pool[l].astype(BF16), "pool_scale": row(pool_scale[l]),
            "w_out": w_out[l].astype(BF16), "norm_mem": row(norm_mem[l]), "w_mem_q": w_mem_q[l].astype(BF16),
            "mem_q_norm": row(mem_q_norm[l]), "w_mem_o": w_mem_o[l].astype(BF16), "norm_ffn": row(norm_ffn[l]),
            "w_gate": w_gate[l], "w_up": w_up[l], "conv_w": conv_w[l], "conv_b": row(conv_b[l]), "w_down": w_down[l],
        }
        pool_prev = jnp.pad(state_pool[l].transpose(1, 0, 2), ((POOL_TAIL - POOL_STATE, 0), (0, 0), (0, 0)))
        conv_prev = state_conv[l].transpose(1, 0, 2).reshape(CONV_TAIL * bs, d_ff)

        def attend_samples(q_hm, k_hm, v_hm, l=l):
            pad = lambda a: jnp.pad(a, ((0, 0), (0, -ts % SUBLANES), (0, 0)))
            to_seq = lambda a: a.reshape(n_heads, ts, bs, HEAD_DIM).transpose(2, 1, 0, 3).reshape(bs, ts, attn_width)
            k_new, v_new = to_seq(k_hm), to_seq(v_hm)
            o = _sample_attn(pad(to_seq(q_hm)), pad(k_new), pad(v_new), cache_win_k[l], cache_win_v[l],
                             n_queries=ts, past_len=PAST_LEN)
            return o[:, :ts].transpose(1, 0, 2).reshape(ts * bs, attn_width), k_new, v_new

        xs, pool_st, ks, vs, conv_st, ffn_weights = _layer(
            xs, PAST_LEN, bs, pool_prev.reshape(POOL_TAIL * bs, pool_width), conv_prev,
            cache_mem_k[l].reshape(bs * n_mem * mem_heads, MEM_HEAD_DIM),
            cache_mem_v[l].reshape(bs * n_mem * mem_heads, MEM_HEAD_DIM), n_mem, attend_samples, prm)
        seq_major = lambda a, t: a.reshape(t, bs, -1).transpose(1, 0, 2)
        outs["s_pool"].append(seq_major(pool_st, POOL_STATE))
        outs["s_k"].append(ks.reshape(bs, ts, n_heads, HEAD_DIM))
        outs["s_v"].append(vs.reshape(bs, ts, n_heads, HEAD_DIM))
        outs["s_conv"].append(seq_major(conv_st, CONV_TAIL))

        mk, mv = _memory_kv(mem_prompt[0], row(norm_mem_src[l]), w_mem_k[l].astype(BF16), w_mem_v[l].astype(BF16),
                            row(mem_k_norm[l]))
        xp, pool_st, kp, vp, conv_st, _ = _layer(
            xp, 0, 1, jnp.zeros((POOL_TAIL, pool_width), F32), jnp.zeros((CONV_TAIL, d_ff), F32), mk, mv, n_mem,
            _dilated_attn, dict(prm, **dict(zip(("w_gate", "w_up", "w_down"), ffn_weights))))
        outs["p_pool"].append(pool_st[None])
        outs["p_k"].append(kp.reshape(1, keep_p, n_heads, HEAD_DIM))
        outs["p_v"].append(vp.reshape(1, keep_p, n_heads, HEAD_DIM))
        outs["p_mk"].append(mk.reshape(1, n_mem, mem_heads, MEM_HEAD_DIM))
        outs["p_mv"].append(mv.reshape(1, n_mem, mem_heads, MEM_HEAD_DIM))
        outs["p_conv"].append(conv_st[None])
    stack = lambda k: jnp.stack(outs[k])
    return (xp.reshape(bp, tp, d_model), xs.reshape(ts, bs, d_model).transpose(1, 0, 2),
            stack("p_pool"), stack("p_k"), stack("p_v"), stack("p_mk"), stack("p_mv"), stack("p_conv"),
            stack("s_pool"), stack("s_k"), stack("s_v"), stack("s_conv"))
```

```python
import functools
import math

import jax
import jax.numpy as jnp
from jax import lax
from jax.experimental import pallas as pl
from jax.experimental.pallas import tpu as pltpu

PAST_LEN = 16384
POOL_WINDOWS = (2, 4, 8, 16)
HEAD_DIM = 128
DILATIONS = ((128, 1), (512, 4), (2048, 16))
MAX_WINDOW = 2048
ROPE_THETA = 10000.0
MEM_HEAD_DIM = 128
CONV_WIDTH = 3
EPS = 1e-6
NEG_INF = -1e30

POOL_STATE = max(POOL_WINDOWS) - 1
POOL_TAIL = max(POOL_WINDOWS)
CONV_TAIL = CONV_WIDTH - 1
BAND = DILATIONS[0][0]
SUPER_BLOCK = MAX_WINDOW

V7X_SCOPED_VMEM_LIMIT = 60000 * 1024
LANES = 128
SUBLANES = 8

IN_PROJ_ROWS = 512
IN_PROJ_GROUP_ROWS = 256
MEM_MIX_ROWS = 512
FFN_ROWS = 512
FFN_COLS = 512

F32 = jnp.float32
BF16 = jnp.bfloat16


def _const_spec(shape):
    return pl.BlockSpec(shape, lambda *_: (0,) * len(shape), pipeline_mode=pl.Buffered(1))


def _params(n_axes, vmem_bytes):
    return pltpu.CompilerParams(
        dimension_semantics=("arbitrary",) * n_axes,
        vmem_limit_bytes=min(int(vmem_bytes), V7X_SCOPED_VMEM_LIMIT),
    )


def _rms(x, gain):
    return x * lax.rsqrt(jnp.mean(x * x, axis=-1, keepdims=True) + EPS) * gain


def _dot(a, b):
    return jnp.dot(a, b, preferred_element_type=F32)


def _dot_nt(a, b):
    return lax.dot_general(a, b, (((1,), (1,)), ((), ())), preferred_element_type=F32)


def _in_proj_kernel(x_ref, cos_ref, sin_ref, prev_ref, norm_ref, win_ref, qn_ref, kn_ref, wpool_ref, pscale_ref,
                    pool_ref, q_ref, k_ref, v_ref, pstate_ref, u_scr,
                    *, tm, nb, pos0, n_tiles, n_heads):
    i = pl.program_id(0)
    tail = POOL_TAIL * nb
    pool_width = pool_ref.shape[1]
    attn_width = n_heads * HEAD_DIM
    group = pool_width // len(POOL_WINDOWS)
    sr = min(tm, IN_PROJ_GROUP_ROWS)

    @pl.when(i == 0)
    def _():
        u_scr[0:tail, :] = prev_ref[...]

    for r0 in range(0, tm, sr):
        rows = slice(r0, r0 + sr)
        h = _rms(x_ref[rows, :], norm_ref[...]).astype(BF16)
        u_scr[tail + r0:tail + r0 + sr, :] = _dot(h, win_ref[:, 0:pool_width])
        row = i * tm + r0 + lax.broadcasted_iota(jnp.int32, (sr, 1), 0)
        pos = pos0 + (row >> int(math.log2(nb)))
        diffs = []
        for g, w in enumerate(POOL_WINDOWS):
            cols = slice(g * group, (g + 1) * group)
            u_t = u_scr[tail + r0:tail + r0 + sr, cols]
            acc = u_t
            for j in range(1, w):
                acc = acc + u_scr[tail + r0 - j * nb:tail + r0 - j * nb + sr, cols]
            cnt = jnp.minimum(pos + 1, w).astype(F32)
            diffs.append((acc / cnt - u_t).astype(BF16))

        cosv = jnp.concatenate([cos_ref[rows, :]] * 2, axis=-1)
        sinv = jnp.concatenate([-sin_ref[rows, :], sin_ref[rows, :]], axis=-1)

        def normed_rope(col0, gain_ref, hm_ref):
            proj = _dot(h, win_ref[:, col0:col0 + attn_width])
            for hh in range(n_heads):
                xn = _rms(proj[:, hh * HEAD_DIM:(hh + 1) * HEAD_DIM], gain_ref[...])
                hm_ref[hh, rows, :] = xn * cosv + pltpu.roll(xn, HEAD_DIM // 2, 1) * sinv

        normed_rope(pool_width, qn_ref, q_ref)
        normed_rope(pool_width + attn_width, kn_ref, k_ref)
        vproj = _dot(h, win_ref[:, pool_width + 2 * attn_width:pool_width + 3 * attn_width])
        for hh in range(n_heads):
            v_ref[hh, rows, :] = vproj[:, hh * HEAD_DIM:(hh + 1) * HEAD_DIM]

        for g, diff in enumerate(diffs):
            cols = slice(g * group, (g + 1) * group)
            pool_ref[rows, cols] = (_dot(diff, wpool_ref[g]) * pscale_ref[:, cols]).astype(pool_ref.dtype)

    @pl.when(i == n_tiles - 1)
    def _():
        pstate_ref[...] = u_scr[tm + nb:tm + tail, :]

    if n_tiles > 1:
        u_scr[0:tail, :] = u_scr[tm:tm + tail, :]


def _in_proj(x, cos, sin, pool_prev, norm, w_in, q_norm, k_norm, w_pool, pool_scale, *, tm, nb, pos0):
    rows, d_model = x.shape
    pool_width = pool_scale.shape[-1]
    attn_width = (w_in.shape[1] - pool_width) // 3
    n_heads = attn_width // HEAD_DIM
    n_tiles = rows // tm
    assert rows % tm == 0 and nb & (nb - 1) == 0 and (tm >= POOL_TAIL * nb or n_tiles == 1)
    tile = lambda width: pl.BlockSpec((tm, width), lambda i: (i, 0))
    heads = pl.BlockSpec((n_heads, tm, HEAD_DIM), lambda i: (0, i, 0))
    vmem = (2 * tm * d_model * 4 + w_in.size * 2 + 2 * 2 * tm * LANES * 4 + 2 * tm * pool_width * 2
            + 3 * 2 * tm * attn_width * 4 + (POOL_TAIL * nb + tm) * pool_width * 4
            + 2 * POOL_TAIL * nb * pool_width * 4 + 6 * tm * attn_width * 4 + tm * d_model * 6 + (4 << 20))
    hm_shape = jax.ShapeDtypeStruct((n_heads, rows, HEAD_DIM), F32)
    return pl.pallas_call(
        functools.partial(_in_proj_kernel, tm=tm, nb=nb, pos0=pos0, n_tiles=n_tiles, n_heads=n_heads),
        grid=(n_tiles,),
        in_specs=[tile(d_model), tile(HEAD_DIM // 2), tile(HEAD_DIM // 2), _const_spec(pool_prev.shape),
                  _const_spec(norm.shape), _const_spec(w_in.shape), _const_spec(q_norm.shape),
                  _const_spec(k_norm.shape), _const_spec(w_pool.shape), _const_spec(pool_scale.shape)],
        out_specs=[tile(pool_width), heads, heads, heads,
                   pl.BlockSpec((POOL_STATE * nb, pool_width), lambda i: (0, 0))],
        out_shape=[jax.ShapeDtypeStruct((rows, pool_width), BF16), hm_shape, hm_shape, hm_shape,
                   jax.ShapeDtypeStruct((POOL_STATE * nb, pool_width), F32)],
        scratch_shapes=[pltpu.VMEM((POOL_TAIL * nb + tm, pool_width), F32)],
        compiler_params=_params(1, vmem),
        name="in_proj",
    )(x, cos, sin, pool_prev, norm, w_in, q_norm, k_norm, w_pool, pool_scale)


CLASSES = 4
LOG2E = 1.4426950408889634


def _dilated_attn_kernel(q_ref, k_ref, v_ref, o_ref, klast_ref, vlast_ref,
                         yq, yk, yv, q1, k1, v1, q4, k4, v4, q16, k16, v16,
                         o1, m1, l1, o4, m4, l4, o16, m16, l16, out_f, bias):
    i = pl.program_id(1)
    sb = q_ref.shape[0]

    @pl.when(i == pl.num_programs(1) - 1)
    def _():
        klast_ref[...] = k_ref[...]
        vlast_ref[...] = v_ref[...]

    n4 = sb // CLASSES
    nk4 = BAND + n4
    nk16 = 2 * BAND
    slots = sb // BAND
    scale = HEAD_DIM ** -0.5 * LOG2E
    keys = ((k1, v1), (k4, v4), (k16, v16))
    prev_rows = ([(0, sb)], [(r * nk4, r * nk4 + n4) for r in range(CLASSES)],
                 [(s * nk16, s * nk16 + BAND) for s in range(slots)])
    val = slice(0, HEAD_DIM)

    @pl.when(i == 0)
    def _():
        for (kd, vd), rows in zip(keys, prev_rows):
            vd[:, HEAD_DIM:] = jnp.ones((vd.shape[0], HEAD_DIM), BF16)
            for dst, _ in rows:
                kd[dst:dst + BAND, :] = jnp.zeros((BAND, HEAD_DIM), BF16)
                vd[dst:dst + BAND, val] = jnp.zeros((BAND, HEAD_DIM), BF16)
        qa = lax.broadcasted_iota(jnp.int32, (BAND, 2 * BAND), 0)
        col = lax.broadcasted_iota(jnp.int32, (BAND, 2 * BAND), 1)
        in_band = (col >= qa) & (col <= qa + BAND)
        bias[0] = jnp.where(in_band, 0.0, NEG_INF)
        bias[1] = jnp.where(in_band & (col >= BAND), 0.0, NEG_INF)

    @pl.when(i > 0)
    def _():
        for (kd, vd), rows in zip(keys, prev_rows):
            for dst, src in rows:
                kd[dst:dst + BAND, :] = kd[src:src + BAND, :]
                vd[dst:dst + BAND, val] = vd[src:src + BAND, val]

    q1[...] = q_ref[...].astype(BF16)
    k1[BAND:BAND + sb, :] = k_ref[...].astype(BF16)
    v1[BAND:BAND + sb, val] = v_ref[...].astype(BF16)
    for src, y, d4, d16, lead, lanes in ((q_ref, yq, q4, q16, 0, slice(None)), (k_ref, yk, k4, k16, BAND, slice(None)),
                                        (v_ref, yv, v4, v16, BAND, val)):
        for r1 in range(CLASSES):
            rows = src[pl.ds(r1, n4, stride=CLASSES), :]
            y[r1 * n4:(r1 + 1) * n4, :] = rows
            d4[r1 * (lead + n4) + lead:(r1 + 1) * (lead + n4), lanes] = rows.astype(BF16)
        for s in range(slots):
            r1, r2 = divmod(s, CLASSES)
            rows = y[pl.ds(r1 * n4 + r2, BAND, stride=CLASSES), :]
            d16[s * (lead + BAND) + lead:(s + 1) * (lead + BAND), lanes] = rows.astype(BF16)

    first = (i == 0).astype(jnp.int32)

    def branch(qd, kd, vd, units, od, md, ld, rows_of):
        for u, (q0, k0, leads) in enumerate(units):
            s = _dot_nt(qd[q0:q0 + BAND, :], kd[k0:k0 + 2 * BAND, :]) * scale + bias[first if leads else 0]
            m = jnp.max(s, axis=-1, keepdims=True)
            ol = _dot(jnp.exp2(s - m).astype(BF16), vd[k0:k0 + 2 * BAND, :])
            od[rows_of(u), :] = ol[:, val]
            md[rows_of(u), :] = jnp.broadcast_to(m, (BAND, HEAD_DIM))
            ld[rows_of(u), :] = ol[:, HEAD_DIM:]

    blocks4 = n4 // BAND
    in_order = lambda u: slice(u * BAND, (u + 1) * BAND)
    branch(q1, k1, v1, [(u * BAND, u * BAND, u == 0) for u in range(slots)], o1, m1, l1, in_order)
    branch(q4, k4, v4, [(r * n4 + b * BAND, r * nk4 + b * BAND, b == 0)
                        for r in range(CLASSES) for b in range(blocks4)], o4, m4, l4, in_order)
    branch(q16, k16, v16, [(s * BAND, s * nk16, True) for s in range(slots)], o16, m16, l16,
           lambda s: pl.ds((s // CLASSES) * n4 + s % CLASSES, BAND, stride=CLASSES))

    for piece in range(slots):
        r1, j = divmod(piece, blocks4)
        rows = in_order(piece)
        nat = pl.ds(r1 + CLASSES * BAND * j, BAND, stride=CLASSES)
        ma, mb, mc = m1[nat, :], m4[rows, :], m16[rows, :]
        m = jnp.maximum(jnp.maximum(ma, mb), mc)
        wa, wb, wc = jnp.exp2(ma - m), jnp.exp2(mb - m), jnp.exp2(mc - m)
        num = wa * o1[nat, :] + wb * o4[rows, :] + wc * o16[rows, :]
        den = wa * l1[nat, :] + wb * l4[rows, :] + wc * l16[rows, :]
        out_f[nat, :] = num / den
    o_ref[...] = out_f[...].astype(o_ref.dtype)


def _dilated_attn(q_hm, k_hm, v_hm):
    n_heads, rows, _ = q_hm.shape
    sb = SUPER_BLOCK
    assert rows % sb == 0 and [d for _, d in DILATIONS] == [1, CLASSES, CLASSES ** 2]
    assert all(w == BAND * d for w, d in DILATIONS) and sb == BAND * CLASSES ** 2
    block = pl.BlockSpec((None, sb, HEAD_DIM), lambda h, i: (h, i, 0))
    blk = lambda n, dt, width=HEAD_DIM: pltpu.VMEM((n, width), dt)
    lead = (BAND, CLASSES * BAND, sb)
    scratch = ([blk(sb, F32)] * 3
               + [s for n in lead for s in (blk(sb, BF16), blk(n + sb, BF16), blk(n + sb, BF16, 2 * HEAD_DIM))]
               + [blk(sb, F32)] * 10
               + [pltpu.VMEM((2, BAND, 2 * BAND), F32)])
    vmem = (2 * 3 * sb * HEAD_DIM * 4 + 2 * sb * HEAD_DIM * 2 + 2 * 2 * sb * HEAD_DIM * 4 + 13 * sb * HEAD_DIM * 4
            + (12 * sb + 3 * sum(lead)) * HEAD_DIM * 2 + 2 * BAND * 2 * BAND * 4 + (12 << 20))
    last = pl.BlockSpec((sb, HEAD_DIM), lambda h, i: (0, h))
    last_shape = jax.ShapeDtypeStruct((sb, n_heads * HEAD_DIM), F32)
    return pl.pallas_call(
        _dilated_attn_kernel,
        grid=(n_heads, rows // sb),
        in_specs=[block, block, block],
        out_specs=[pl.BlockSpec((sb, HEAD_DIM), lambda h, i: (i, h)), last, last],
        out_shape=[jax.ShapeDtypeStruct((rows, n_heads * HEAD_DIM), BF16), last_shape, last_shape],
        scratch_shapes=scratch,
        compiler_params=_params(2, vmem),
        name="dilated_attn",
    )(q_hm, k_hm, v_hm)


FAR_STEP = DILATIONS[2][1]
NEAR_ROWS = DILATIONS[1][0]


def _sample_attn_kernel(q_ref, kn_ref, vn_ref, *refs, n_heads, past_len, cache, n_far):
    far_k, far_v = refs[0:n_far], refs[n_far:2 * n_far]
    near_k, near_v, o_ref, count = refs[2 * n_far:]
    tq = q_ref.shape[0]
    far_m = (cache - NEAR_ROWS) // FAR_STEP
    scale = HEAD_DIM ** -0.5
    n_rows, n_cols = count.shape
    log_tq, log_heads = int(math.log2(tq)), int(math.log2(n_heads))

    def multiplicity(t, key, ok):
        dist = cache + t - key
        seen = ok & (dist >= 0) & (past_len + t - dist >= 0)
        n = jnp.zeros(key.shape, F32)
        for window, dil in DILATIONS:
            hit = seen & (dist <= window) & ((dist & (dil - 1)) == 0)
            n = n + jnp.where(hit, 1.0, 0.0)
        return n

    @pl.when(pl.program_id(0) == 0)
    def _():
        row = lax.broadcasted_iota(jnp.int32, (n_rows, n_cols), 0)
        col = lax.broadcasted_iota(jnp.int32, (n_rows, n_cols), 1)
        pos = col >> log_heads
        cls = sum((pos >= r * far_m).astype(jnp.int32) for r in range(1, n_far))
        far_key = FAR_STEP * (pos - cls * far_m) + cls
        key = jnp.where(pos < n_far * far_m, far_key, cache - NEAR_ROWS + pos - n_far * far_m)
        count[...] = multiplicity(row & (tq - 1), key, (row >> log_tq) == (col & (n_heads - 1)))

    def all_heads(far, near):
        parts = [f.reshape(far_m * n_heads, HEAD_DIM)[...] for f in far] + [near[...]]
        return jnp.concatenate(parts, axis=0).astype(BF16)

    head = lambda a, hh: a[:, hh * HEAD_DIM:(hh + 1) * HEAD_DIM]
    rows_of = lambda a, hh: a[hh * tq:(hh + 1) * tq, :]
    q = [head(q_ref, hh).astype(BF16) for hh in range(n_heads)]
    n_old = count[...]
    s_o = jnp.where(n_old > 0, _dot_nt(jnp.concatenate(q, axis=0), all_heads(far_k, near_k)) * scale, NEG_INF)
    t_new = lax.broadcasted_iota(jnp.int32, (n_rows, tq), 0) & (tq - 1)
    n_new = multiplicity(t_new, cache + lax.broadcasted_iota(jnp.int32, (n_rows, tq), 1), True)
    s_n = jnp.concatenate([_dot_nt(q[hh], head(kn_ref, hh).astype(BF16)) for hh in range(n_heads)], axis=0) * scale
    s_n = jnp.where(n_new > 0, s_n, NEG_INF)
    m = jnp.maximum(jnp.max(s_o, axis=-1, keepdims=True), jnp.max(s_n, axis=-1, keepdims=True))
    p_o = n_old * jnp.exp(s_o - m)
    p_n = n_new * jnp.exp(s_n - m)
    den = jnp.sum(p_o, axis=-1, keepdims=True) + jnp.sum(p_n, axis=-1, keepdims=True)
    num = _dot(p_o.astype(BF16), all_heads(far_v, near_v))
    for hh in range(n_heads):
        new = _dot(rows_of(p_n, hh).astype(BF16), head(vn_ref, hh).astype(BF16))
        o_ref[:, hh * HEAD_DIM:(hh + 1) * HEAD_DIM] = ((rows_of(num, hh) + new) / rows_of(den, hh)).astype(o_ref.dtype)


def _sample_attn(q, k_new, v_new, cache_k, cache_v, *, n_queries, past_len):
    batch, tq, width = q.shape
    _, cache, n_heads, _ = cache_k.shape
    far_m = (cache - NEAR_ROWS) // FAR_STEP
    n_far = min(n_queries, FAR_STEP) if far_m else 0
    assert all(d & (d - 1) == 0 for _, d in DILATIONS) and cache % NEAR_ROWS == 0 and cache % FAR_STEP == 0
    assert DILATIONS[0][0] <= NEAR_ROWS and tq == SUBLANES and NEAR_ROWS % FAR_STEP == 0
    assert n_heads & (n_heads - 1) == 0
    new = pl.BlockSpec((None, tq, width), lambda b: (b, 0, 0))
    far_view = lambda c: c.reshape(batch, cache // FAR_STEP, FAR_STEP, n_heads, HEAD_DIM)
    near_view = lambda c: c.reshape(batch, cache * n_heads, HEAD_DIM)
    far = [pl.BlockSpec((None, far_m, None, n_heads, HEAD_DIM),
                        functools.partial(lambda b, r: (b, 0, r, 0, 0), r=r)) for r in range(n_far)]
    near = pl.BlockSpec((None, NEAR_ROWS * n_heads, HEAD_DIM), lambda b: (b, cache // NEAR_ROWS - 1, 0))
    n_cols = (n_far * far_m + NEAR_ROWS) * n_heads
    vmem = 2 * 2 * n_cols * HEAD_DIM * 4 + 8 * n_heads * tq * n_cols * 4 + 4 * n_cols * HEAD_DIM * 2 + (16 << 20)
    return pl.pallas_call(
        functools.partial(_sample_attn_kernel, n_heads=n_heads, past_len=past_len, cache=cache, n_far=n_far),
        grid=(batch,),
        in_specs=[new, new, new] + far + far + [near, near],
        out_specs=new,
        out_shape=jax.ShapeDtypeStruct((batch, tq, width), BF16),
        scratch_shapes=[pltpu.VMEM((n_heads * tq, n_cols), F32)],
        compiler_params=_params(1, vmem),
        name="sample_attn",
    )(q, k_new, v_new, *([far_view(cache_k)] * n_far), *([far_view(cache_v)] * n_far),
      near_view(cache_k), near_view(cache_v))


def _memory_kv_kernel(mem_ref, norm_ref, wk_ref, wv_ref, kn_ref, k_ref, v_ref):
    m = _rms(mem_ref[...], norm_ref[...]).astype(BF16)
    n_mem = m.shape[0]
    k = _dot(m, wk_ref[...])
    v = _dot(m, wv_ref[...])
    heads = k.shape[1] // MEM_HEAD_DIM
    for hh in range(heads):
        cols = slice(hh * MEM_HEAD_DIM, (hh + 1) * MEM_HEAD_DIM)
        k_ref[pl.ds(hh, n_mem, stride=heads), :] = _rms(k[:, cols], kn_ref[...])
        v_ref[pl.ds(hh, n_mem, stride=heads), :] = v[:, cols]


def _memory_kv(mem, norm_src, w_k, w_v, k_norm):
    n_mem = mem.shape[0]
    heads = w_k.shape[1] // MEM_HEAD_DIM
    args = (mem, norm_src, w_k, w_v, k_norm)
    out = jax.ShapeDtypeStruct((n_mem * heads, MEM_HEAD_DIM), F32)
    return pl.pallas_call(
        _memory_kv_kernel,
        grid=(1,),
        in_specs=[_const_spec(a.shape) for a in args],
        out_specs=[pl.BlockSpec((n_mem * heads, MEM_HEAD_DIM), lambda i: (0, 0))] * 2,
        out_shape=[out, out],
        compiler_params=_params(1, 32 << 20),
        name="memory_kv",
    )(*args)


def _mem_mix_kernel(x_ref, pool_ref, attn_ref, wout_ref, nmem_ref, wq_ref, qn_ref, mk_ref, mv_ref, wo_ref, nffn_ref,
                    x2_ref, h3_ref, *, nb, n_mem):
    tm = x_ref.shape[0]
    pool_width = pool_ref.shape[1]
    x1 = x_ref[...] + _dot(pool_ref[...], wout_ref[0:pool_width, :]) + _dot(attn_ref[...], wout_ref[pool_width:, :])
    h2 = _rms(x1, nmem_ref[...]).astype(BF16)
    qm = _dot(h2, wq_ref[...])
    scale = MEM_HEAD_DIM ** -0.5
    n_heads = qm.shape[1] // MEM_HEAD_DIM
    n_keys = mk_ref.shape[0] // n_heads
    if nb > 1:
        seq_q = (pl.program_id(0) * tm + lax.broadcasted_iota(jnp.int32, (tm, n_keys), 0)) & (nb - 1)
        seq_k = lax.broadcasted_iota(jnp.int32, (tm, n_keys), 1) >> int(math.log2(n_mem))
        own = seq_q == seq_k
    def scores(hh):
        q = _rms(qm[:, hh * MEM_HEAD_DIM:(hh + 1) * MEM_HEAD_DIM], qn_ref[...]).astype(BF16)
        s = _dot_nt(q, mk_ref[pl.ds(hh, n_keys, stride=n_heads), :].astype(BF16)) * scale
        return jnp.where(own, s, NEG_INF) if nb > 1 else s

    heads = []
    s_next = scores(0)
    for hh in range(n_heads):
        s = s_next
        if hh + 1 < n_heads:
            s_next = scores(hh + 1)
        p = jnp.exp(s - jnp.max(s, axis=-1, keepdims=True))
        o = _dot(p.astype(BF16), mv_ref[pl.ds(hh, n_keys, stride=n_heads), :].astype(BF16))
        heads.append((o / jnp.sum(p, axis=-1, keepdims=True)).astype(BF16))
    x2 = x1 + _dot(jnp.concatenate(heads, axis=-1), wo_ref[...])
    x2_ref[...] = x2
    h3_ref[...] = _rms(x2, nffn_ref[...]).astype(h3_ref.dtype)


def _mem_mix(x, pool_out, attn_out, w_out, norm_mem, w_q, q_norm, mem_k, mem_v, w_o, norm_ffn, *, tm, nb, n_mem):
    rows, d_model = x.shape
    n_keys = mem_k.shape[0] * MEM_HEAD_DIM // w_q.shape[1]
    assert rows % tm == 0 and nb & (nb - 1) == 0 and n_mem & (n_mem - 1) == 0 and n_keys == nb * n_mem
    tile = lambda a: pl.BlockSpec((tm, a.shape[1]), lambda i: (i, 0))
    consts = (w_out, norm_mem, w_q, q_norm, mem_k, mem_v, w_o, norm_ffn)
    vmem = (sum(c.size * c.dtype.itemsize for c in consts) + 2 * tm * d_model * (4 + 2 + 4 + 2)
            + tm * d_model * 16 + 4 * tm * n_keys * 4 + (6 << 20))
    return pl.pallas_call(
        functools.partial(_mem_mix_kernel, nb=nb, n_mem=n_mem),
        grid=(rows // tm,),
        in_specs=[tile(x), tile(pool_out), tile(attn_out)] + [_const_spec(c.shape) for c in consts],
        out_specs=[pl.BlockSpec((tm, d_model), lambda i: (i, 0))] * 2,
        out_shape=[jax.ShapeDtypeStruct((rows, d_model), F32), jax.ShapeDtypeStruct((rows, d_model), BF16)],
        compiler_params=_params(1, vmem),
        name="mem_mix",
    )(x, pool_out, attn_out, *consts)


def _conv_ffn_kernel(h_ref, x_ref, wg_ref, wu_ref, cw_ref, cb_ref, wd_ref, prev_ref, y_ref, state_ref,
                     *rest, nb):
    *rounded_refs, g_ext, g_scr = rest
    i = pl.program_id(0)
    j = pl.program_id(1)
    tm = h_ref.shape[0]
    tail = CONV_TAIL * nb
    if rounded_refs:
        for src, dst in zip((wg_ref, wu_ref, wd_ref), rounded_refs):
            dst[...] = src[...].astype(BF16)
        wg_ref, wu_ref, wd_ref = rounded_refs

    @pl.when(j == 0)
    def _():
        y_ref[...] = x_ref[...]

    @pl.when(i == 0)
    def _():
        g_ext[0:tail, :] = prev_ref[...]

    @pl.when(i > 0)
    def _():
        g_ext[0:tail, :] = g_scr[j]

    g_ext[tail:tail + tm, :] = _dot(h_ref[...], wg_ref[...])
    up = _dot(h_ref[...], wu_ref[...])
    c = cb_ref[...]
    for tap in range(CONV_WIDTH):
        c = c + g_ext[tap * nb:tap * nb + tm, :] * cw_ref[tap:tap + 1, :]
    act = (c * jax.nn.sigmoid(c) * up).astype(BF16)
    y_ref[...] += _dot(act, wd_ref[...])

    last = g_ext[tm:tm + tail, :]
    g_scr[j] = last
    state_ref[:, pl.ds(pl.multiple_of(j * last.shape[1], LANES), last.shape[1])] = last


def _conv_ffn(h3, x2, w_gate, w_up, conv_w, conv_b, w_down, conv_prev, *, tm, tf, nb):
    rows, d_model = x2.shape
    d_ff = w_gate.shape[1]
    weights = (w_gate, w_up, w_down)
    rounds = w_gate.dtype != BF16
    assert rows % tm == 0 and d_ff % tf == 0 and tm >= CONV_TAIL * nb and (rows == tm or not rounds)
    assert all(w.dtype == w_gate.dtype for w in weights)
    tail = CONV_TAIL * nb
    row_tile = pl.BlockSpec((tm, d_model), lambda i, j: (i, 0))
    chunk = lambda n: pl.BlockSpec((n, tf), lambda i, j: (0, j))
    down_chunk = pl.BlockSpec((tf, d_model), lambda i, j: (j, 0))
    w_bytes = w_gate.dtype.itemsize + (2 if rounds else 0)
    vmem = (2 * tm * d_model * (2 + 4 + 4) + 2 * 3 * d_model * tf * w_bytes + (tm + 3 * tail) * tf * 4
            + d_ff // tf * tail * tf * 4 + 5 * tm * tf * 4 + tm * d_model * 4 + (6 << 20))
    y, state, *rounded = pl.pallas_call(
        functools.partial(_conv_ffn_kernel, nb=nb),
        grid=(rows // tm, d_ff // tf),
        in_specs=[row_tile, row_tile, chunk(d_model), chunk(d_model), chunk(CONV_WIDTH), chunk(1),
                  down_chunk, chunk(tail)],
        out_specs=([row_tile, pl.BlockSpec((tail, d_ff), lambda i, j: (0, 0))]
                   + ([chunk(d_model), chunk(d_model), down_chunk] if rounds else [])),
        out_shape=([jax.ShapeDtypeStruct((rows, d_model), F32), jax.ShapeDtypeStruct((tail, d_ff), F32)]
                   + ([jax.ShapeDtypeStruct(w.shape, BF16) for w in weights] if rounds else [])),
        scratch_shapes=[pltpu.VMEM((tail + tm, tf), F32), pltpu.VMEM((d_ff // tf, tail, tf), F32)],
        compiler_params=_params(2, vmem),
        name="conv_ffn",
    )(h3, x2, w_gate, w_up, conv_w, conv_b, w_down, conv_prev)
    return y, state, tuple(rounded) if rounds else weights


ROPE_SPLIT = 128


def _rope_tables(pos0, n_pos):
    half = HEAD_DIM // 2
    inv = 1.0 / (ROPE_THETA ** (jnp.arange(half, dtype=F32) * (2.0 / HEAD_DIM)))
    angle = lambda pos: pos.astype(F32)[:, None] * inv[None, :]
    if pos0 % ROPE_SPLIT == 0 and n_pos % ROPE_SPLIT == 0:
        a = angle(pos0 + ROPE_SPLIT * jnp.arange(n_pos // ROPE_SPLIT))[:, None, :]
        b = angle(jnp.arange(ROPE_SPLIT))[None, :, :]
        cos = (jnp.cos(a) * jnp.cos(b) - jnp.sin(a) * jnp.sin(b)).reshape(n_pos, half)
        sin = (jnp.sin(a) * jnp.cos(b) + jnp.cos(a) * jnp.sin(b)).reshape(n_pos, half)
    else:
        ang = angle(pos0 + jnp.arange(n_pos))
        cos, sin = jnp.cos(ang), jnp.sin(ang)
    return cos, sin


def _row_tile(rows, want):
    return want if rows % want == 0 else rows


def _layer(x, pos0, nb, pool_prev, conv_prev, mem_k, mem_v, n_mem, attend, p):
    rows = x.shape[0]
    cos, sin = _rope_tables(pos0, rows // nb)
    if nb > 1:
        cos, sin = jnp.repeat(cos, nb, axis=0), jnp.repeat(sin, nb, axis=0)
    pool_out, q_hm, k_hm, v_hm, pool_state = _in_proj(
        x, cos, sin, pool_prev, p["norm_mix"], p["w_in"], p["q_norm"], p["k_norm"], p["w_pool"], p["pool_scale"],
        tm=_row_tile(rows, IN_PROJ_ROWS), nb=nb, pos0=pos0)
    attn_out, k_keep, v_keep = attend(q_hm, k_hm, v_hm)
    x2, h3 = _mem_mix(x, pool_out, attn_out, p["w_out"], p["norm_mem"], p["w_mem_q"], p["mem_q_norm"], mem_k, mem_v,
                      p["w_mem_o"], p["norm_ffn"], tm=_row_tile(rows, MEM_MIX_ROWS), nb=nb, n_mem=n_mem)
    y, conv_state, ffn_weights = _conv_ffn(h3, x2, p["w_gate"], p["w_up"], p["conv_w"], p["conv_b"], p["w_down"],
                                           conv_prev, tm=_row_tile(rows, FFN_ROWS), tf=FFN_COLS, nb=nb)
    return y, pool_state, k_keep, v_keep, conv_state, ffn_weights


def kernel(x_prompt, x_sample, state_pool, cache_win_k, cache_win_v, cache_mem_k, cache_mem_v, state_conv, mem_prompt, norm_mix, w_in, q_norm, k_norm, w_pool, pool_scale, w_out, norm_mem, norm_mem_src, w_mem_q, w_mem_k, w_mem_v, mem_q_norm, mem_k_norm, w_mem_o, norm_ffn, w_gate, w_up, conv_w, conv_b, w_down):
    bp, tp, d_model = x_prompt.shape
    bs, ts, _ = x_sample.shape
    depth = w_in.shape[0]
    assert bp == 1
    pool_width = pool_scale.shape[-1]
    d_ff = w_gate.shape[-1]
    n_mem = mem_prompt.shape[1]
    n_heads = cache_win_k.shape[3]
    mem_heads = cache_mem_k.shape[3]
    attn_width = n_heads * HEAD_DIM
    keep_p = min(MAX_WINDOW, tp)
    assert keep_p == SUPER_BLOCK
    row = lambda a: a.reshape(1, -1)

    xp = x_prompt.reshape(tp, d_model)
    xs = x_sample.transpose(1, 0, 2).reshape(ts * bs, d_model)
    outs = {k: [] for k in ("p_pool", "p_k", "p_v", "p_mk", "p_mv", "p_conv", "s_pool", "s_k", "s_v", "s_conv")}
    for l in range(depth):
        prm = {
            "norm_mix": row(norm_mix[l]), "w_in": w_in[l].astype(BF16), "q_norm": row(q_norm[l]),
            "k_norm": row(k_norm[l]), "w_pool": w_pool[l].astype(BF16), "pool_scale": row(pool_scale[l]),
            "w_out": w_out[l].astype(BF16), "norm_mem": row(norm_mem[l]), "w_mem_q": w_mem_q[l].astype(BF16),
            "mem_q_norm": row(mem_q_norm[l]), "w_mem_o": w_mem_o[l].astype(BF16), "norm_ffn": row(norm_ffn[l]),
            "w_gate": w_gate[l], "w_up": w_up[l], "conv_w": conv_w[l], "conv_b": row(conv_b[l]), "w_down": w_down[l],
        }
        pool_prev = jnp.pad(state_pool[l].transpose(1, 0, 2), ((POOL_TAIL - POOL_STATE, 0), (0, 0), (0, 0)))
        conv_prev = state_conv[l].transpose(1, 0, 2).reshape(CONV_TAIL * bs, d_ff)

        def attend_samples(q_hm, k_hm, v_hm, l=l):
            pad = lambda a: jnp.pad(a, ((0, 0), (0, -ts % SUBLANES), (0, 0)))
            to_seq = lambda a: a.reshape(n_heads, ts, bs, HEAD_DIM).transpose(2, 1, 0, 3).reshape(bs, ts, attn_width)
            k_new, v_new = to_seq(k_hm), to_seq(v_hm)
            o = _sample_attn(pad(to_seq(q_hm)), pad(k_new), pad(v_new), cache_win_k[l], cache_win_v[l],
                             n_queries=ts, past_len=PAST_LEN)
            return o[:, :ts].transpose(1, 0, 2).reshape(ts * bs, attn_width), k_new, v_new

        xs, pool_st, ks, vs, conv_st, ffn_weights = _layer(
            xs, PAST_LEN, bs, pool_prev.reshape(POOL_TAIL * bs, pool_width), conv_prev,
            cache_mem_k[l].reshape(bs * n_mem * mem_heads, MEM_HEAD_DIM),
            cache_mem_v[l].reshape(bs * n_mem * mem_heads, MEM_HEAD_DIM), n_mem, attend_samples, prm)
        seq_major = lambda a, t: a.reshape(t, bs, -1).transpose(1, 0, 2)
        outs["s_pool"].append(seq_major(pool_st, POOL_STATE))
        outs["s_k"].append(ks.reshape(bs, ts, n_heads, HEAD_DIM))
        outs["s_v"].append(vs.reshape(bs, ts, n_heads, HEAD_DIM))
        outs["s_conv"].append(seq_major(conv_st, CONV_TAIL))

        mk, mv = _memory_kv(mem_prompt[0], row(norm_mem_src[l]), w_mem_k[l].astype(BF16), w_mem_v[l].astype(BF16),
                            row(mem_k_norm[l]))
        xp, pool_st, kp, vp, conv_st, _ = _layer(
            xp, 0, 1, jnp.zeros((POOL_TAIL, pool_width), F32), jnp.zeros((CONV_TAIL, d_ff), F32), mk, mv, n_mem,
            _dilated_attn, dict(prm, **dict(zip(("w_gate", "w_up", "w_down"), ffn_weights))))
        outs["p_pool"].append(pool_st[None])
        outs["p_k"].append(kp.reshape(1, keep_p, n_heads, HEAD_DIM))
        outs["p_v"].append(vp.reshape(1, keep_p, n_heads, HEAD_DIM))
        outs["p_mk"].append(mk.reshape(1, n_mem, mem_heads, MEM_HEAD_DIM))
        outs["p_mv"].append(mv.reshape(1, n_mem, mem_heads, MEM_HEAD_DIM))
        outs["p_conv"].append(conv_st[None])
    stack = lambda k: jnp.stack(outs[k])
    return (xp.reshape(bp, tp, d_model), xs.reshape(ts, bs, d_model).transpose(1, 0, 2),
            stack("p_pool"), stack("p_k"), stack("p_v"), stack("p_mk"), stack("p_mv"), stack("p_conv"),
            stack("s_pool"), stack("s_k"), stack("s_v"), stack("s_conv"))
```

```python
import functools
import math

import jax
import jax.numpy as jnp
from jax import lax
from jax.experimental import pallas as pl
from jax.experimental.pallas import tpu as pltpu

PAST_LEN = 16384
POOL_WINDOWS = (2, 4, 8, 16)
HEAD_DIM = 128
DILATIONS = ((128, 1), (512, 4), (2048, 16))
MAX_WINDOW = 2048
ROPE_THETA = 10000.0
MEM_HEAD_DIM = 128
CONV_WIDTH = 3
EPS = 1e-6
NEG_INF = -1e30

POOL_STATE = max(POOL_WINDOWS) - 1
POOL_TAIL = max(POOL_WINDOWS)
CONV_TAIL = CONV_WIDTH - 1
BAND = DILATIONS[0][0]
SUPER_BLOCK = MAX_WINDOW

V7X_SCOPED_VMEM_LIMIT = 60000 * 1024
LANES = 128
SUBLANES = 8

IN_PROJ_ROWS = 512
IN_PROJ_GROUP_ROWS = 256
MEM_MIX_ROWS = 512
FFN_ROWS = 512
FFN_COLS = 512

F32 = jnp.float32
BF16 = jnp.bfloat16


def _const_spec(shape):
    return pl.BlockSpec(shape, lambda *_: (0,) * len(shape), pipeline_mode=pl.Buffered(1))


def _params(n_axes, vmem_bytes):
    return pltpu.CompilerParams(
        dimension_semantics=("arbitrary",) * n_axes,
        vmem_limit_bytes=min(int(vmem_bytes), V7X_SCOPED_VMEM_LIMIT),
    )


def _rms(x, gain):
    return x * lax.rsqrt(jnp.mean(x * x, axis=-1, keepdims=True) + EPS) * gain


def _dot(a, b):
    return jnp.dot(a, b, preferred_element_type=F32)


def _dot_nt(a, b):
    return lax.dot_general(a, b, (((1,), (1,)), ((), ())), preferred_element_type=F32)


def _in_proj_kernel(x_ref, cos_ref, sin_ref, prev_ref, norm_ref, win_ref, qn_ref, kn_ref, wpool_ref, pscale_ref,
                    pool_ref, q_ref, k_ref, v_ref, pstate_ref, u_scr,
                    *, tm, nb, pos0, n_tiles, n_heads):
    i = pl.program_id(0)
    tail = POOL_TAIL * nb
    pool_width = pool_ref.shape[1]
    attn_width = n_heads * HEAD_DIM
    group = pool_width // len(POOL_WINDOWS)
    sr = min(tm, IN_PROJ_GROUP_ROWS)

    @pl.when(i == 0)
    def _():
        u_scr[0:tail, :] = prev_ref[...]

    for r0 in range(0, tm, sr):
        rows = slice(r0, r0 + sr)
        h = _rms(x_ref[rows, :], norm_ref[...]).astype(BF16)
        u_scr[tail + r0:tail + r0 + sr, :] = _dot(h, win_ref[:, 0:pool_width])
        row = i * tm + r0 + lax.broadcasted_iota(jnp.int32, (sr, 1), 0)
        pos = pos0 + (row >> int(math.log2(nb)))
        diffs = []
        for g, w in enumerate(POOL_WINDOWS):
            cols = slice(g * group, (g + 1) * group)
            u_t = u_scr[tail + r0:tail + r0 + sr, cols]
            acc = u_t
            for j in range(1, w):
                acc = acc + u_scr[tail + r0 - j * nb:tail + r0 - j * nb + sr, cols]
            cnt = jnp.minimum(pos + 1, w).astype(F32)
            diffs.append((acc / cnt - u_t).astype(BF16))

        cosv = jnp.concatenate([cos_ref[rows, :]] * 2, axis=-1)
        sinv = jnp.concatenate([-sin_ref[rows, :], sin_ref[rows, :]], axis=-1)

        def normed_rope(col0, gain_ref, hm_ref):
            proj = _dot(h, win_ref[:, col0:col0 + attn_width])
            for hh in range(n_heads):
                xn = _rms(proj[:, hh * HEAD_DIM:(hh + 1) * HEAD_DIM], gain_ref[...])
                hm_ref[hh, rows, :] = xn * cosv + pltpu.roll(xn, HEAD_DIM // 2, 1) * sinv

        normed_rope(pool_width, qn_ref, q_ref)
        normed_rope(pool_width + attn_width, kn_ref, k_ref)
        vproj = _dot(h, win_ref[:, pool_width + 2 * attn_width:pool_width + 3 * attn_width])
        for hh in range(n_heads):
            v_ref[hh, rows, :] = vproj[:, hh * HEAD_DIM:(hh + 1) * HEAD_DIM]

        for g, diff in enumerate(diffs):
            cols = slice(g * group, (g + 1) * group)
            pool_ref[rows, cols] = (_dot(diff, wpool_ref[g]) * pscale_ref[:, cols]).astype(pool_ref.dtype)

    @pl.when(i == n_tiles - 1)
    def _():
        pstate_ref[...] = u_scr[tm + nb:tm + tail, :]

    if n_tiles > 1:
        u_scr[0:tail, :] = u_scr[tm:tm + tail, :]


def _in_proj(x, cos, sin, pool_prev, norm, w_in, q_norm, k_norm, w_pool, pool_scale, *, tm, nb, pos0):
    rows, d_model = x.shape
    pool_width = pool_scale.shape[-1]
    attn_width = (w_in.shape[1] - pool_width) // 3
    n_heads = attn_width // HEAD_DIM
    n_tiles = rows // tm
    assert rows % tm == 0 and nb & (nb - 1) == 0 and (tm >= POOL_TAIL * nb or n_tiles == 1)
    tile = lambda width: pl.BlockSpec((tm, width), lambda i: (i, 0))
    heads = pl.BlockSpec((n_heads, tm, HEAD_DIM), lambda i: (0, i, 0))
    vmem = (2 * tm * d_model * 4 + w_in.size * 2 + 2 * 2 * tm * LANES * 4 + 2 * tm * pool_width * 2
            + 3 * 2 * tm * attn_width * 4 + (POOL_TAIL * nb + tm) * pool_width * 4
            + 2 * POOL_TAIL * nb * pool_width * 4 + 6 * tm * attn_width * 4 + tm * d_model * 6 + (4 << 20))
    hm_shape = jax.ShapeDtypeStruct((n_heads, rows, HEAD_DIM), F32)
    return pl.pallas_call(
        functools.partial(_in_proj_kernel, tm=tm, nb=nb, pos0=pos0, n_tiles=n_tiles, n_heads=n_heads),
        grid=(n_tiles,),
        in_specs=[tile(d_model), tile(HEAD_DIM // 2), tile(HEAD_DIM // 2), _const_spec(pool_prev.shape),
                  _const_spec(norm.shape), _const_spec(w_in.shape), _const_spec(q_norm.shape),
                  _const_spec(k_norm.shape), _const_spec(w_pool.shape), _const_spec(pool_scale.shape)],
        out_specs=[tile(pool_width), heads, heads, heads,
                   pl.BlockSpec((POOL_STATE * nb, pool_width), lambda i: (0, 0))],
        out_shape=[jax.ShapeDtypeStruct((rows, pool_width), BF16), hm_shape, hm_shape, hm_shape,
                   jax.ShapeDtypeStruct((POOL_STATE * nb, pool_width), F32)],
        scratch_shapes=[pltpu.VMEM((POOL_TAIL * nb + tm, pool_width), F32)],
        compiler_params=_params(1, vmem),
        name="in_proj",
    )(x, cos, sin, pool_prev, norm, w_in, q_norm, k_norm, w_pool, pool_scale)


CLASSES = 4
LOG2E = 1.4426950408889634


def _dilated_attn_kernel(q_ref, k_ref, v_ref, o_ref, klast_ref, vlast_ref,
                         yq, yk, yv, q1, k1, v1, q4, k4, v4, q16, k16, v16,
                         o1, m1, l1, o4, m4, l4, o16, m16, l16, out_f, bias):
    i = pl.program_id(1)
    sb = q_ref.shape[0]

    @pl.when(i == pl.num_programs(1) - 1)
    def _():
        klast_ref[...] = k_ref[...]
        vlast_ref[...] = v_ref[...]

    n4 = sb // CLASSES
    nk4 = BAND + n4
    nk16 = 2 * BAND
    slots = sb // BAND
    scale = HEAD_DIM ** -0.5 * LOG2E
    keys = ((k1, v1), (k4, v4), (k16, v16))
    prev_rows = ([(0, sb)], [(r * nk4, r * nk4 + n4) for r in range(CLASSES)],
                 [(s * nk16, s * nk16 + BAND) for s in range(slots)])
    val = slice(0, HEAD_DIM)

    @pl.when(i == 0)
    def _():
        for (kd, vd), rows in zip(keys, prev_rows):
            vd[:, HEAD_DIM:] = jnp.ones((vd.shape[0], HEAD_DIM), BF16)
            for dst, _ in rows:
                kd[dst:dst + BAND, :] = jnp.zeros((BAND, HEAD_DIM), BF16)
                vd[dst:dst + BAND, val] = jnp.zeros((BAND, HEAD_DIM), BF16)
        qa = lax.broadcasted_iota(jnp.int32, (BAND, 2 * BAND), 0)
        col = lax.broadcasted_iota(jnp.int32, (BAND, 2 * BAND), 1)
        in_band = (col >= qa) & (col <= qa + BAND)
        bias[0] = jnp.where(in_band, 0.0, NEG_INF)
        bias[1] = jnp.where(in_band & (col >= BAND), 0.0, NEG_INF)

    @pl.when(i > 0)
    def _():
        for (kd, vd), rows in zip(keys, prev_rows):
            for dst, src in rows:
                kd[dst:dst + BAND, :] = kd[src:src + BAND, :]
                vd[dst:dst + BAND, val] = vd[src:src + BAND, val]

    q1[...] = q_ref[...].astype(BF16)
    k1[BAND:BAND + sb, :] = k_ref[...].astype(BF16)
    v1[BAND:BAND + sb, val] = v_ref[...].astype(BF16)
    for src, y, d4, d16, lead, lanes in ((q_ref, yq, q4, q16, 0, slice(None)), (k_ref, yk, k4, k16, BAND, slice(None)),
                                        (v_ref, yv, v4, v16, BAND, val)):
        for r1 in range(CLASSES):
            rows = src[pl.ds(r1, n4, stride=CLASSES), :]
            y[r1 * n4:(r1 + 1) * n4, :] = rows
            d4[r1 * (lead + n4) + lead:(r1 + 1) * (lead + n4), lanes] = rows.astype(BF16)
        for s in range(slots):
            r1, r2 = divmod(s, CLASSES)
            rows = y[pl.ds(r1 * n4 + r2, BAND, stride=CLASSES), :]
            d16[s * (lead + BAND) + lead:(s + 1) * (lead + BAND), lanes] = rows.astype(BF16)

    first = (i == 0).astype(jnp.int32)

    def branch(qd, kd, vd, units, od, md, ld, rows_of):
        for u, (q0, k0, leads) in enumerate(units):
            s = _dot_nt(qd[q0:q0 + BAND, :], kd[k0:k0 + 2 * BAND, :]) * scale + bias[first if leads else 0]
            m = jnp.max(s, axis=-1, keepdims=True)
            ol = _dot(jnp.exp2(s - m).astype(BF16), vd[k0:k0 + 2 * BAND, :])
            od[rows_of(u), :] = ol[:, val]
            md[rows_of(u), :] = jnp.broadcast_to(m, (BAND, HEAD_DIM))
            ld[rows_of(u), :] = ol[:, HEAD_DIM:]

    blocks4 = n4 // BAND
    in_order = lambda u: slice(u * BAND, (u + 1) * BAND)
    branch(q1, k1, v1, [(u * BAND, u * BAND, u == 0) for u in range(slots)], o1, m1, l1, in_order)
    branch(q4, k4, v4, [(r * n4 + b * BAND, r * nk4 + b * BAND, b == 0)
                        for r in range(CLASSES) for b in range(blocks4)], o4, m4, l4, in_order)
    branch(q16, k16, v16, [(s * BAND, s * nk16, True) for s in range(slots)], o16, m16, l16,
           lambda s: pl.ds((s // CLASSES) * n4 + s % CLASSES, BAND, stride=CLASSES))

    for piece in range(slots):
        r1, j = divmod(piece, blocks4)
        rows = in_order(piece)
        nat = pl.ds(r1 + CLASSES * BAND * j, BAND, stride=CLASSES)
        ma, mb, mc = m1[nat, :], m4[rows, :], m16[rows, :]
        m = jnp.maximum(jnp.maximum(ma, mb), mc)
        wa, wb, wc = jnp.exp2(ma - m), jnp.exp2(mb - m), jnp.exp2(mc - m)
        num = wa * o1[nat, :] + wb * o4[rows, :] + wc * o16[rows, :]
        den = wa * l1[nat, :] + wb * l4[rows, :] + wc * l16[rows, :]
        out_f[nat, :] = num / den
    o_ref[...] = out_f[...].astype(o_ref.dtype)


def _dilated_attn(q_hm, k_hm, v_hm):
    n_heads, rows, _ = q_hm.shape
    sb = SUPER_BLOCK
    assert rows % sb == 0 and [d for _, d in DILATIONS] == [1, CLASSES, CLASSES ** 2]
    assert all(w == BAND * d for w, d in DILATIONS) and sb == BAND * CLASSES ** 2
    block = pl.BlockSpec((None, sb, HEAD_DIM), lambda h, i: (h, i, 0))
    blk = lambda n, dt, width=HEAD_DIM: pltpu.VMEM((n, width), dt)
    lead = (BAND, CLASSES * BAND, sb)
    scratch = ([blk(sb, F32)] * 3
               + [s for n in lead for s in (blk(sb, BF16), blk(n + sb, BF16), blk(n + sb, BF16, 2 * HEAD_DIM))]
               + [blk(sb, F32)] * 10
               + [pltpu.VMEM((2, BAND, 2 * BAND), F32)])
    vmem = (2 * 3 * sb * HEAD_DIM * 4 + 2 * sb * HEAD_DIM * 2 + 2 * 2 * sb * HEAD_DIM * 4 + 13 * sb * HEAD_DIM * 4
            + (12 * sb + 3 * sum(lead)) * HEAD_DIM * 2 + 2 * BAND * 2 * BAND * 4 + (12 << 20))
    last = pl.BlockSpec((sb, HEAD_DIM), lambda h, i: (0, h))
    last_shape = jax.ShapeDtypeStruct((sb, n_heads * HEAD_DIM), F32)
    return pl.pallas_call(
        _dilated_attn_kernel,
        grid=(n_heads, rows // sb),
        in_specs=[block, block, block],
        out_specs=[pl.BlockSpec((sb, HEAD_DIM), lambda h, i: (i, h)), last, last],
        out_shape=[jax.ShapeDtypeStruct((rows, n_heads * HEAD_DIM), BF16), last_shape, last_shape],
        scratch_shapes=scratch,
        compiler_params=_params(2, vmem),
        name="dilated_attn",
    )(q_hm, k_hm, v_hm)


FAR_STEP = DILATIONS[2][1]
NEAR_ROWS = DILATIONS[1][0]


def _sample_attn_kernel(q_ref, kn_ref, vn_ref, *refs, n_heads, past_len, cache, n_far):
    far_k, far_v = (refs[0:1], refs[1:2]) if n_far else ((), ())
    near_k, near_v, o_ref, count = refs[2 if n_far else 0:]
    tq = q_ref.shape[0]
    far_m = (cache - NEAR_ROWS) // FAR_STEP
    scale = HEAD_DIM ** -0.5
    n_rows, n_cols = count.shape
    log_tq, log_heads = int(math.log2(tq)), int(math.log2(n_heads))

    def multiplicity(t, key, ok):
        dist = cache + t - key
        seen = ok & (dist >= 0) & (past_len + t - dist >= 0)
        n = jnp.zeros(key.shape, F32)
        for window, dil in DILATIONS:
            hit = seen & (dist <= window) & ((dist & (dil - 1)) == 0)
            n = n + jnp.where(hit, 1.0, 0.0)
        return n

    @pl.when(pl.program_id(0) == 0)
    def _():
        row = lax.broadcasted_iota(jnp.int32, (n_rows, n_cols), 0)
        col = lax.broadcasted_iota(jnp.int32, (n_rows, n_cols), 1)
        pos = col >> log_heads
        far_key = FAR_STEP * (pos >> int(math.log2(max(n_far, 1)))) + (pos & (n_far - 1))
        key = jnp.where(pos < n_far * far_m, far_key, cache - NEAR_ROWS + pos - n_far * far_m)
        count[...] = multiplicity(row & (tq - 1), key, (row >> log_tq) == (col & (n_heads - 1)))

    def all_heads(far, near):
        parts = [f.reshape(far_m * n_far * n_heads, HEAD_DIM)[...] for f in far] + [near[...]]
        return jnp.concatenate(parts, axis=0).astype(BF16)

    head = lambda a, hh: a[:, hh * HEAD_DIM:(hh + 1) * HEAD_DIM]
    rows_of = lambda a, hh: a[hh * tq:(hh + 1) * tq, :]
    q = [head(q_ref, hh).astype(BF16) for hh in range(n_heads)]
    n_old = count[...]
    s_o = jnp.where(n_old > 0, _dot_nt(jnp.concatenate(q, axis=0), all_heads(far_k, near_k)) * scale, NEG_INF)
    t_new = lax.broadcasted_iota(jnp.int32, (n_rows, tq), 0) & (tq - 1)
    n_new = multiplicity(t_new, cache + lax.broadcasted_iota(jnp.int32, (n_rows, tq), 1), True)
    s_n = jnp.concatenate([_dot_nt(q[hh], head(kn_ref, hh).astype(BF16)) for hh in range(n_heads)], axis=0) * scale
    s_n = jnp.where(n_new > 0, s_n, NEG_INF)
    m = jnp.maximum(jnp.max(s_o, axis=-1, keepdims=True), jnp.max(s_n, axis=-1, keepdims=True))
    p_o = n_old * jnp.exp(s_o - m)
    p_n = n_new * jnp.exp(s_n - m)
    den = jnp.sum(p_o, axis=-1, keepdims=True) + jnp.sum(p_n, axis=-1, keepdims=True)
    num = _dot(p_o.astype(BF16), all_heads(far_v, near_v))
    for hh in range(n_heads):
        new = _dot(rows_of(p_n, hh).astype(BF16), head(vn_ref, hh).astype(BF16))
        o_ref[:, hh * HEAD_DIM:(hh + 1) * HEAD_DIM] = ((rows_of(num, hh) + new) / rows_of(den, hh)).astype(o_ref.dtype)


def _sample_attn(q, k_new, v_new, cache_k, cache_v, *, n_queries, past_len):
    batch, tq, width = q.shape
    _, cache, n_heads, _ = cache_k.shape
    far_m = (cache - NEAR_ROWS) // FAR_STEP
    n_far = min(n_queries, FAR_STEP) if far_m else 0
    assert all(d & (d - 1) == 0 for _, d in DILATIONS) and cache % NEAR_ROWS == 0 and cache % FAR_STEP == 0
    assert DILATIONS[0][0] <= NEAR_ROWS and tq == SUBLANES and NEAR_ROWS % FAR_STEP == 0
    assert n_heads & (n_heads - 1) == 0 and n_far & (n_far - 1) == 0
    new = pl.BlockSpec((None, tq, width), lambda b: (b, 0, 0))
    far_view = lambda c: c.reshape(batch, cache // FAR_STEP, FAR_STEP, n_heads, HEAD_DIM)
    near_view = lambda c: c.reshape(batch, cache * n_heads, HEAD_DIM)
    far = [pl.BlockSpec((None, far_m, n_far, n_heads, HEAD_DIM), lambda b: (b, 0, 0, 0, 0))] if n_far else []
    near = pl.BlockSpec((None, NEAR_ROWS * n_heads, HEAD_DIM), lambda b: (b, cache // NEAR_ROWS - 1, 0))
    n_cols = (n_far * far_m + NEAR_ROWS) * n_heads
    vmem = 2 * 2 * n_cols * HEAD_DIM * 4 + 8 * n_heads * tq * n_cols * 4 + 4 * n_cols * HEAD_DIM * 2 + (16 << 20)
    return pl.pallas_call(
        functools.partial(_sample_attn_kernel, n_heads=n_heads, past_len=past_len, cache=cache, n_far=n_far),
        grid=(batch,),
        in_specs=[new, new, new] + far + far + [near, near],
        out_specs=new,
        out_shape=jax.ShapeDtypeStruct((batch, tq, width), BF16),
        scratch_shapes=[pltpu.VMEM((n_heads * tq, n_cols), F32)],
        compiler_params=_params(1, vmem),
        name="sample_attn",
    )(q, k_new, v_new, *([far_view(cache_k), far_view(cache_v)] if n_far else []),
      near_view(cache_k), near_view(cache_v))


def _memory_kv_kernel(mem_ref, norm_ref, wk_ref, wv_ref, kn_ref, k_ref, v_ref):
    m = _rms(mem_ref[...], norm_ref[...]).astype(BF16)
    n_mem = m.shape[0]
    k = _dot(m, wk_ref[...])
    v = _dot(m, wv_ref[...])
    heads = k.shape[1] // MEM_HEAD_DIM
    for hh in range(heads):
        cols = slice(hh * MEM_HEAD_DIM, (hh + 1) * MEM_HEAD_DIM)
        k_ref[pl.ds(hh, n_mem, stride=heads), :] = _rms(k[:, cols], kn_ref[...])
        v_ref[pl.ds(hh, n_mem, stride=heads), :] = v[:, cols]


def _memory_kv(mem, norm_src, w_k, w_v, k_norm):
    n_mem = mem.shape[0]
    heads = w_k.shape[1] // MEM_HEAD_DIM
    args = (mem, norm_src, w_k, w_v, k_norm)
    out = jax.ShapeDtypeStruct((n_mem * heads, MEM_HEAD_DIM), F32)
    return pl.pallas_call(
        _memory_kv_kernel,
        grid=(1,),
        in_specs=[_const_spec(a.shape) for a in args],
        out_specs=[pl.BlockSpec((n_mem * heads, MEM_HEAD_DIM), lambda i: (0, 0))] * 2,
        out_shape=[out, out],
        compiler_params=_params(1, 32 << 20),
        name="memory_kv",
    )(*args)


def _mem_mix_kernel(x_ref, pool_ref, attn_ref, wout_ref, nmem_ref, wq_ref, qn_ref, mk_ref, mv_ref, wo_ref, nffn_ref,
                    x2_ref, h3_ref, *, nb, n_mem):
    tm = x_ref.shape[0]
    pool_width = pool_ref.shape[1]
    x1 = x_ref[...] + _dot(pool_ref[...], wout_ref[0:pool_width, :]) + _dot(attn_ref[...], wout_ref[pool_width:, :])
    h2 = _rms(x1, nmem_ref[...]).astype(BF16)
    qm = _dot(h2, wq_ref[...])
    scale = MEM_HEAD_DIM ** -0.5
    n_heads = qm.shape[1] // MEM_HEAD_DIM
    n_keys = mk_ref.shape[0] // n_heads
    if nb > 1:
        seq_q = (pl.program_id(0) * tm + lax.broadcasted_iota(jnp.int32, (tm, n_keys), 0)) & (nb - 1)
        seq_k = lax.broadcasted_iota(jnp.int32, (tm, n_keys), 1) >> int(math.log2(n_mem))
        own = seq_q == seq_k
    def scores(hh):
        q = _rms(qm[:, hh * MEM_HEAD_DIM:(hh + 1) * MEM_HEAD_DIM], qn_ref[...]).astype(BF16)
        s = _dot_nt(q, mk_ref[pl.ds(hh, n_keys, stride=n_heads), :].astype(BF16)) * scale
        return jnp.where(own, s, NEG_INF) if nb > 1 else s

    heads = []
    s_next = scores(0)
    for hh in range(n_heads):
        s = s_next
        if hh + 1 < n_heads:
            s_next = scores(hh + 1)
        p = jnp.exp(s - jnp.max(s, axis=-1, keepdims=True))
        o = _dot(p.astype(BF16), mv_ref[pl.ds(hh, n_keys, stride=n_heads), :].astype(BF16))
        heads.append((o / jnp.sum(p, axis=-1, keepdims=True)).astype(BF16))
    x2 = x1 + _dot(jnp.concatenate(heads, axis=-1), wo_ref[...])
    x2_ref[...] = x2
    h3_ref[...] = _rms(x2, nffn_ref[...]).astype(h3_ref.dtype)


def _mem_mix(x, pool_out, attn_out, w_out, norm_mem, w_q, q_norm, mem_k, mem_v, w_o, norm_ffn, *, tm, nb, n_mem):
    rows, d_model = x.shape
    n_keys = mem_k.shape[0] * MEM_HEAD_DIM // w_q.shape[1]
    assert rows % tm == 0 and nb & (nb - 1) == 0 and n_mem & (n_mem - 1) == 0 and n_keys == nb * n_mem
    tile = lambda a: pl.BlockSpec((tm, a.shape[1]), lambda i: (i, 0))
    consts = (w_out, norm_mem, w_q, q_norm, mem_k, mem_v, w_o, norm_ffn)
    vmem = (sum(c.size * c.dtype.itemsize for c in consts) + 2 * tm * d_model * (4 + 2 + 4 + 2)
            + tm * d_model * 16 + 4 * tm * n_keys * 4 + (6 << 20))
    return pl.pallas_call(
        functools.partial(_mem_mix_kernel, nb=nb, n_mem=n_mem),
        grid=(rows // tm,),
        in_specs=[tile(x), tile(pool_out), tile(attn_out)] + [_const_spec(c.shape) for c in consts],
        out_specs=[pl.BlockSpec((tm, d_model), lambda i: (i, 0))] * 2,
        out_shape=[jax.ShapeDtypeStruct((rows, d_model), F32), jax.ShapeDtypeStruct((rows, d_model), BF16)],
        compiler_params=_params(1, vmem),
        name="mem_mix",
    )(x, pool_out, attn_out, *consts)


def _conv_ffn_kernel(h_ref, x_ref, wg_ref, wu_ref, cw_ref, cb_ref, wd_ref, prev_ref, y_ref, state_ref,
                     *rest, nb):
    *rounded_refs, g_ext, g_scr = rest
    i = pl.program_id(0)
    j = pl.program_id(1)
    tm = h_ref.shape[0]
    tail = CONV_TAIL * nb
    if rounded_refs:
        for src, dst in zip((wg_ref, wu_ref, wd_ref), rounded_refs):
            dst[...] = src[...].astype(BF16)
        wg_ref, wu_ref, wd_ref = rounded_refs

    @pl.when(j == 0)
    def _():
        y_ref[...] = x_ref[...]

    @pl.when(i == 0)
    def _():
        g_ext[0:tail, :] = prev_ref[...]

    @pl.when(i > 0)
    def _():
        g_ext[0:tail, :] = g_scr[j]

    g_ext[tail:tail + tm, :] = _dot(h_ref[...], wg_ref[...])
    up = _dot(h_ref[...], wu_ref[...])
    c = cb_ref[...]
    for tap in range(CONV_WIDTH):
        c = c + g_ext[tap * nb:tap * nb + tm, :] * cw_ref[tap:tap + 1, :]
    act = (c * jax.nn.sigmoid(c) * up).astype(BF16)
    y_ref[...] += _dot(act, wd_ref[...])

    last = g_ext[tm:tm + tail, :]
    g_scr[j] = last
    state_ref[:, pl.ds(pl.multiple_of(j * last.shape[1], LANES), last.shape[1])] = last


def _conv_ffn(h3, x2, w_gate, w_up, conv_w, conv_b, w_down, conv_prev, *, tm, tf, nb):
    rows, d_model = x2.shape
    d_ff = w_gate.shape[1]
    weights = (w_gate, w_up, w_down)
    rounds = w_gate.dtype != BF16
    assert rows % tm == 0 and d_ff % tf == 0 and tm >= CONV_TAIL * nb and (rows == tm or not rounds)
    assert all(w.dtype == w_gate.dtype for w in weights)
    tail = CONV_TAIL * nb
    row_tile = pl.BlockSpec((tm, d_model), lambda i, j: (i, 0))
    chunk = lambda n: pl.BlockSpec((n, tf), lambda i, j: (0, j))
    down_chunk = pl.BlockSpec((tf, d_model), lambda i, j: (j, 0))
    w_bytes = w_gate.dtype.itemsize + (2 if rounds else 0)
    vmem = (2 * tm * d_model * (2 + 4 + 4) + 2 * 3 * d_model * tf * w_bytes + (tm + 3 * tail) * tf * 4
            + d_ff // tf * tail * tf * 4 + 5 * tm * tf * 4 + tm * d_model * 4 + (6 << 20))
    y, state, *rounded = pl.pallas_call(
        functools.partial(_conv_ffn_kernel, nb=nb),
        grid=(rows // tm, d_ff // tf),
        in_specs=[row_tile, row_tile, chunk(d_model), chunk(d_model), chunk(CONV_WIDTH), chunk(1),
                  down_chunk, chunk(tail)],
        out_specs=([row_tile, pl.BlockSpec((tail, d_ff), lambda i, j: (0, 0))]
                   + ([chunk(d_model), chunk(d_model), down_chunk] if rounds else [])),
        out_shape=([jax.ShapeDtypeStruct((rows, d_model), F32), jax.ShapeDtypeStruct((tail, d_ff), F32)]
                   + ([jax.ShapeDtypeStruct(w.shape, BF16) for w in weights] if rounds else [])),
        scratch_shapes=[pltpu.VMEM((tail + tm, tf), F32), pltpu.VMEM((d_ff // tf, tail, tf), F32)],
        compiler_params=_params(2, vmem),
        name="conv_ffn",
    )(h3, x2, w_gate, w_up, conv_w, conv_b, w_down, conv_prev)
    return y, state, tuple(rounded) if rounds else weights


ROPE_SPLIT = 128


def _rope_tables(pos0, n_pos):
    half = HEAD_DIM // 2
    inv = 1.0 / (ROPE_THETA ** (jnp.arange(half, dtype=F32) * (2.0 / HEAD_DIM)))
    angle = lambda pos: pos.astype(F32)[:, None] * inv[None, :]
    if pos0 % ROPE_SPLIT == 0 and n_pos % ROPE_SPLIT == 0:
        a = angle(pos0 + ROPE_SPLIT * jnp.arange(n_pos // ROPE_SPLIT))[:, None, :]
        b = angle(jnp.arange(ROPE_SPLIT))[None, :, :]
        cos = (jnp.cos(a) * jnp.cos(b) - jnp.sin(a) * jnp.sin(b)).reshape(n_pos, half)
        sin = (jnp.sin(a) * jnp.cos(b) + jnp.cos(a) * jnp.sin(b)).reshape(n_pos, half)
    else:
        ang = angle(pos0 + jnp.arange(n_pos))
        cos, sin = jnp.cos(ang), jnp.sin(ang)
    return cos, sin


def _row_tile(rows, want):
    return want if rows % want == 0 else rows


def _layer(x, pos0, nb, pool_prev, conv_prev, mem_k, mem_v, n_mem, attend, p):
    rows = x.shape[0]
    cos, sin = _rope_tables(pos0, rows // nb)
    if nb > 1:
        cos, sin = jnp.repeat(cos, nb, axis=0), jnp.repeat(sin, nb, axis=0)
    pool_out, q_hm, k_hm, v_hm, pool_state = _in_proj(
        x, cos, sin, pool_prev, p["norm_mix"], p["w_in"], p["q_norm"], p["k_norm"], p["w_pool"], p["pool_scale"],
        tm=_row_tile(rows, IN_PROJ_ROWS), nb=nb, pos0=pos0)
    attn_out, k_keep, v_keep = attend(q_hm, k_hm, v_hm)
    x2, h3 = _mem_mix(x, pool_out, attn_out, p["w_out"], p["norm_mem"], p["w_mem_q"], p["mem_q_norm"], mem_k, mem_v,
                      p["w_mem_o"], p["norm_ffn"], tm=_row_tile(rows, MEM_MIX_ROWS), nb=nb, n_mem=n_mem)
    y, conv_state, ffn_weights = _conv_ffn(h3, x2, p["w_gate"], p["w_up"], p["conv_w"], p["conv_b"], p["w_down"],
                                           conv_prev, tm=_row_tile(rows, FFN_ROWS), tf=FFN_COLS, nb=nb)
    return y, pool_state, k_keep, v_keep, conv_state, ffn_weights


def kernel(x_prompt, x_sample, state_pool, cache_win_k, cache_win_v, cache_mem_k, cache_mem_v, state_conv, mem_prompt, norm_mix, w_in, q_norm, k_norm, w_pool, pool_scale, w_out, norm_mem, norm_mem_src, w_mem_q, w_mem_k, w_mem_v, mem_q_norm, mem_k_norm, w_mem_o, norm_ffn, w_gate, w_up, conv_w, conv_b, w_down):
    bp, tp, d_model = x_prompt.shape
    bs, ts, _ = x_sample.shape
    depth = w_in.shape[0]
    assert bp == 1
    pool_width = pool_scale.shape[-1]
    d_ff = w_gate.shape[-1]
    n_mem = mem_prompt.shape[1]
    n_heads = cache_win_k.shape[3]
    mem_heads = cache_mem_k.shape[3]
    attn_width = n_heads * HEAD_DIM
    keep_p = min(MAX_WINDOW, tp)
    assert keep_p == SUPER_BLOCK
    row = lambda a: a.reshape(1, -1)

    xp = x_prompt.reshape(tp, d_model)
    xs = x_sample.transpose(1, 0, 2).reshape(ts * bs, d_model)
    outs = {k: [] for k in ("p_pool", "p_k", "p_v", "p_mk", "p_mv", "p_conv", "s_pool", "s_k", "s_v", "s_conv")}
    for l in range(depth):
        prm = {
            "norm_mix": row(norm_mix[l]), "w_in": w_in[l].astype(BF16), "q_norm": row(q_norm[l]),
            "k_norm": row(k_norm[l]), "w_pool": w_pool[l].astype(BF16), "pool_scale": row(pool_scale[l]),
            "w_out": w_out[l].astype(BF16), "norm_mem": row(norm_mem[l]), "w_mem_q": w_mem_q[l].astype(BF16),
            "mem_q_norm": row(mem_q_norm[l]), "w_mem_o": w_mem_o[l].astype(BF16), "norm_ffn": row(norm_ffn[l]),
            "w_gate": w_gate[l], "w_up": w_up[l], "conv_w": conv_w[l], "conv_b": row(conv_b[l]), "w_down": w_down[l],
        }
        pool_prev = jnp.pad(state_pool[l].transpose(1, 0, 2), ((POOL_TAIL - POOL_STATE, 0), (0, 0), (0, 0)))
        conv_prev = state_conv[l].transpose(1, 0, 2).reshape(CONV_TAIL * bs, d_ff)

        def attend_samples(q_hm, k_hm, v_hm, l=l):
            pad = lambda a: jnp.pad(a, ((0, 0), (0, -ts % SUBLANES), (0, 0)))
            to_seq = lambda a: a.reshape(n_heads, ts, bs, HEAD_DIM).transpose(2, 1, 0, 3).reshape(bs, ts, attn_width)
            k_new, v_new = to_seq(k_hm), to_seq(v_hm)
            o = _sample_attn(pad(to_seq(q_hm)), pad(k_new), pad(v_new), cache_win_k[l], cache_win_v[l],
                             n_queries=ts, past_len=PAST_LEN)
            return o[:, :ts].transpose(1, 0, 2).reshape(ts * bs, attn_width), k_new, v_new

        xs, pool_st, ks, vs, conv_st, ffn_weights = _layer(
            xs, PAST_LEN, bs, pool_prev.reshape(POOL_TAIL * bs, pool_width), conv_prev,
            cache_mem_k[l].reshape(bs * n_mem * mem_heads, MEM_HEAD_DIM),
            cache_mem_v[l].reshape(bs * n_mem * mem_heads, MEM_HEAD_DIM), n_mem, attend_samples, prm)
        seq_major = lambda a, t: a.reshape(t, bs, -1).transpose(1, 0, 2)
        outs["s_pool"].append(seq_major(pool_st, POOL_STATE))
        outs["s_k"].append(ks.reshape(bs, ts, n_heads, HEAD_DIM))
        outs["s_v"].append(vs.reshape(bs, ts, n_heads, HEAD_DIM))
        outs["s_conv"].append(seq_major(conv_st, CONV_TAIL))

        mk, mv = _memory_kv(mem_prompt[0], row(norm_mem_src[l]), w_mem_k[l].astype(BF16), w_mem_v[l].astype(BF16),
                            row(mem_k_norm[l]))
        xp, pool_st, kp, vp, conv_st, _ = _layer(
            xp, 0, 1, jnp.zeros((POOL_TAIL, pool_width), F32), jnp.zeros((CONV_TAIL, d_ff), F32), mk, mv, n_mem,
            _dilated_attn, dict(prm, **dict(zip(("w_gate", "w_up", "w_down"), ffn_weights))))
        outs["p_pool"].append(pool_st[None])
        outs["p_k"].append(kp.reshape(1, keep_p, n_heads, HEAD_DIM))
        outs["p_v"].append(vp.reshape(1, keep_p, n_heads, HEAD_DIM))
        outs["p_mk"].append(mk.reshape(1, n_mem, mem_heads, MEM_HEAD_DIM))
        outs["p_mv"].append(mv.reshape(1, n_mem, mem_heads, MEM_HEAD_DIM))
        outs["p_conv"].append(conv_st[None])
    stack = lambda k: jnp.stack(outs[k])
    return (xp.reshape(bp, tp, d_model), xs.reshape(ts, bs, d_model).transpose(1, 0, 2),
            stack("p_pool"), stack("p_k"), stack("p_v"), stack("p_mk"), stack("p_mv"), stack("p_conv"),
            stack("s_pool"), stack("s_k"), stack("s_v"), stack("s_conv"))
```

```python
import functools
import math

import jax
import jax.numpy as jnp
from jax import lax
from jax.experimental import pallas as pl
from jax.experimental.pallas import tpu as pltpu

PAST_LEN = 16384
POOL_WINDOWS = (2, 4, 8, 16)
HEAD_DIM = 128
DILATIONS = ((128, 1), (512, 4), (2048, 16))
MAX_WINDOW = 2048
ROPE_THETA = 10000.0
MEM_HEAD_DIM = 128
CONV_WIDTH = 3
EPS = 1e-6
NEG_INF = -1e30

POOL_STATE = max(POOL_WINDOWS) - 1
POOL_TAIL = max(POOL_WINDOWS)
CONV_TAIL = CONV_WIDTH - 1
BAND = DILATIONS[0][0]
SUPER_BLOCK = MAX_WINDOW

V7X_SCOPED_VMEM_LIMIT = 60000 * 1024
LANES = 128
SUBLANES = 8

IN_PROJ_ROWS = 512
IN_PROJ_GROUP_ROWS = 256
MEM_MIX_ROWS = 512
FFN_ROWS = 512
FFN_COLS = 512

F32 = jnp.float32
BF16 = jnp.bfloat16


def _const_spec(shape):
    return pl.BlockSpec(shape, lambda *_: (0,) * len(shape), pipeline_mode=pl.Buffered(1))


def _params(n_axes, vmem_bytes):
    return pltpu.CompilerParams(
        dimension_semantics=("arbitrary",) * n_axes,
        vmem_limit_bytes=min(int(vmem_bytes), V7X_SCOPED_VMEM_LIMIT),
    )


def _rms(x, gain):
    return x * lax.rsqrt(jnp.mean(x * x, axis=-1, keepdims=True) + EPS) * gain


def _dot(a, b):
    return jnp.dot(a, b, preferred_element_type=F32)


def _dot_nt(a, b):
    return lax.dot_general(a, b, (((1,), (1,)), ((), ())), preferred_element_type=F32)


def _in_proj_kernel(x_ref, cos_ref, sin_ref, prev_ref, norm_ref, win_ref, qn_ref, kn_ref, wpool_ref, pscale_ref,
                    pool_ref, q_ref, k_ref, v_ref, pstate_ref, u_scr,
                    *, tm, nb, pos0, n_tiles, n_heads):
    i = pl.program_id(0)
    tail = POOL_TAIL * nb
    pool_width = pool_ref.shape[1]
    attn_width = n_heads * HEAD_DIM
    group = pool_width // len(POOL_WINDOWS)
    sr = min(tm, IN_PROJ_GROUP_ROWS)

    @pl.when(i == 0)
    def _():
        u_scr[0:tail, :] = prev_ref[...]

    for r0 in range(0, tm, sr):
        rows = slice(r0, r0 + sr)
        h = _rms(x_ref[rows, :], norm_ref[...]).astype(BF16)
        u_scr[tail + r0:tail + r0 + sr, :] = _dot(h, win_ref[:, 0:pool_width])
        row = i * tm + r0 + lax.broadcasted_iota(jnp.int32, (sr, 1), 0)
        pos = pos0 + (row >> int(math.log2(nb)))
        diffs = []
        for g, w in enumerate(POOL_WINDOWS):
            cols = slice(g * group, (g + 1) * group)
            u_t = u_scr[tail + r0:tail + r0 + sr, cols]
            acc = u_t
            for j in range(1, w):
                acc = acc + u_scr[tail + r0 - j * nb:tail + r0 - j * nb + sr, cols]
            cnt = jnp.minimum(pos + 1, w).astype(F32)
            diffs.append((acc / cnt - u_t).astype(BF16))

        cosv = jnp.concatenate([cos_ref[rows, :]] * 2, axis=-1)
        sinv = jnp.concatenate([-sin_ref[rows, :], sin_ref[rows, :]], axis=-1)

        def normed_rope(col0, gain_ref, hm_ref):
            proj = _dot(h, win_ref[:, col0:col0 + attn_width])
            for hh in range(n_heads):
                xn = _rms(proj[:, hh * HEAD_DIM:(hh + 1) * HEAD_DIM], gain_ref[...])
                hm_ref[hh, rows, :] = xn * cosv + pltpu.roll(xn, HEAD_DIM // 2, 1) * sinv

        normed_rope(pool_width, qn_ref, q_ref)
        normed_rope(pool_width + attn_width, kn_ref, k_ref)
        vproj = _dot(h, win_ref[:, pool_width + 2 * attn_width:pool_width + 3 * attn_width])
        for hh in range(n_heads):
            v_ref[hh, rows, :] = vproj[:, hh * HEAD_DIM:(hh + 1) * HEAD_DIM]

        for g, diff in enumerate(diffs):
            cols = slice(g * group, (g + 1) * group)
            pool_ref[rows, cols] = (_dot(diff, wpool_ref[g]) * pscale_ref[:, cols]).astype(pool_ref.dtype)

    @pl.when(i == n_tiles - 1)
    def _():
        pstate_ref[...] = u_scr[tm + nb:tm + tail, :]

    if n_tiles > 1:
        u_scr[0:tail, :] = u_scr[tm:tm + tail, :]


def _in_proj(x, cos, sin, pool_prev, norm, w_in, q_norm, k_norm, w_pool, pool_scale, *, tm, nb, pos0):
    rows, d_model = x.shape
    pool_width = pool_scale.shape[-1]
    attn_width = (w_in.shape[1] - pool_width) // 3
    n_heads = attn_width // HEAD_DIM
    n_tiles = rows // tm
    assert rows % tm == 0 and nb & (nb - 1) == 0 and (tm >= POOL_TAIL * nb or n_tiles == 1)
    tile = lambda width: pl.BlockSpec((tm, width), lambda i: (i, 0))
    heads = pl.BlockSpec((n_heads, tm, HEAD_DIM), lambda i: (0, i, 0))
    vmem = (2 * tm * d_model * 4 + w_in.size * 2 + 2 * 2 * tm * LANES * 4 + 2 * tm * pool_width * 2
            + 3 * 2 * tm * attn_width * 4 + (POOL_TAIL * nb + tm) * pool_width * 4
            + 2 * POOL_TAIL * nb * pool_width * 4 + 6 * tm * attn_width * 4 + tm * d_model * 6 + (4 << 20))
    hm_shape = jax.ShapeDtypeStruct((n_heads, rows, HEAD_DIM), F32)
    return pl.pallas_call(
        functools.partial(_in_proj_kernel, tm=tm, nb=nb, pos0=pos0, n_tiles=n_tiles, n_heads=n_heads),
        grid=(n_tiles,),
        in_specs=[tile(d_model), tile(HEAD_DIM // 2), tile(HEAD_DIM // 2), _const_spec(pool_prev.shape),
                  _const_spec(norm.shape), _const_spec(w_in.shape), _const_spec(q_norm.shape),
                  _const_spec(k_norm.shape), _const_spec(w_pool.shape), _const_spec(pool_scale.shape)],
        out_specs=[tile(pool_width), heads, heads, heads,
                   pl.BlockSpec((POOL_STATE * nb, pool_width), lambda i: (0, 0))],
        out_shape=[jax.ShapeDtypeStruct((rows, pool_width), BF16), hm_shape, hm_shape, hm_shape,
                   jax.ShapeDtypeStruct((POOL_STATE * nb, pool_width), F32)],
        scratch_shapes=[pltpu.VMEM((POOL_TAIL * nb + tm, pool_width), F32)],
        compiler_params=_params(1, vmem),
        name="in_proj",
    )(x, cos, sin, pool_prev, norm, w_in, q_norm, k_norm, w_pool, pool_scale)


CLASSES = 4
LOG2E = 1.4426950408889634


def _dilated_attn_kernel(q_ref, k_ref, v_ref, o_ref, klast_ref, vlast_ref,
                         yq, yk, yv, q1, k1, v1, q4, k4, v4, q16, k16, v16,
                         o1, m1, l1, o4, m4, l4, o16, m16, l16, out_f, bias):
    i = pl.program_id(1)
    sb = q_ref.shape[0]

    @pl.when(i == pl.num_programs(1) - 1)
    def _():
        klast_ref[...] = k_ref[...]
        vlast_ref[...] = v_ref[...]

    n4 = sb // CLASSES
    nk4 = BAND + n4
    nk16 = 2 * BAND
    slots = sb // BAND
    scale = HEAD_DIM ** -0.5 * LOG2E
    keys = ((k1, v1), (k4, v4), (k16, v16))
    prev_rows = ([(0, sb)], [(r * nk4, r * nk4 + n4) for r in range(CLASSES)],
                 [(s * nk16, s * nk16 + BAND) for s in range(slots)])
    val = slice(0, HEAD_DIM)

    @pl.when(i == 0)
    def _():
        for (kd, vd), rows in zip(keys, prev_rows):
            vd[:, HEAD_DIM:] = jnp.ones((vd.shape[0], HEAD_DIM), BF16)
            for dst, _ in rows:
                kd[dst:dst + BAND, :] = jnp.zeros((BAND, HEAD_DIM), BF16)
                vd[dst:dst + BAND, val] = jnp.zeros((BAND, HEAD_DIM), BF16)
        qa = lax.broadcasted_iota(jnp.int32, (BAND, 2 * BAND), 0)
        col = lax.broadcasted_iota(jnp.int32, (BAND, 2 * BAND), 1)
        in_band = (col >= qa) & (col <= qa + BAND)
        bias[0] = jnp.where(in_band, 0.0, NEG_INF)
        bias[1] = jnp.where(in_band & (col >= BAND), 0.0, NEG_INF)

    @pl.when(i > 0)
    def _():
        for (kd, vd), rows in zip(keys, prev_rows):
            for dst, src in rows:
                kd[dst:dst + BAND, :] = kd[src:src + BAND, :]
                vd[dst:dst + BAND, val] = vd[src:src + BAND, val]

    q1[...] = q_ref[...].astype(BF16)
    k1[BAND:BAND + sb, :] = k_ref[...].astype(BF16)
    v1[BAND:BAND + sb, val] = v_ref[...].astype(BF16)
    for src, y, d4, d16, lead, lanes in ((q_ref, yq, q4, q16, 0, slice(None)), (k_ref, yk, k4, k16, BAND, slice(None)),
                                        (v_ref, yv, v4, v16, BAND, val)):
        for r1 in range(CLASSES):
            rows = src[pl.ds(r1, n4, stride=CLASSES), :]
            y[r1 * n4:(r1 + 1) * n4, :] = rows
            d4[r1 * (lead + n4) + lead:(r1 + 1) * (lead + n4), lanes] = rows.astype(BF16)
        for s in range(slots):
            r1, r2 = divmod(s, CLASSES)
            rows = y[pl.ds(r1 * n4 + r2, BAND, stride=CLASSES), :]
            d16[s * (lead + BAND) + lead:(s + 1) * (lead + BAND), lanes] = rows.astype(BF16)

    first = (i == 0).astype(jnp.int32)

    def branch(qd, kd, vd, units, od, md, ld, rows_of):
        for u, (q0, k0, leads) in enumerate(units):
            s = _dot_nt(qd[q0:q0 + BAND, :], kd[k0:k0 + 2 * BAND, :]) * scale + bias[first if leads else 0]
            m = jnp.max(s, axis=-1, keepdims=True)
            ol = _dot(jnp.exp2(s - m).astype(BF16), vd[k0:k0 + 2 * BAND, :])
            od[rows_of(u), :] = ol[:, val]
            md[rows_of(u), :] = jnp.broadcast_to(m, (BAND, HEAD_DIM))
            ld[rows_of(u), :] = ol[:, HEAD_DIM:]

    blocks4 = n4 // BAND
    in_order = lambda u: slice(u * BAND, (u + 1) * BAND)
    branch(q1, k1, v1, [(u * BAND, u * BAND, u == 0) for u in range(slots)], o1, m1, l1, in_order)
    branch(q4, k4, v4, [(r * n4 + b * BAND, r * nk4 + b * BAND, b == 0)
                        for r in range(CLASSES) for b in range(blocks4)], o4, m4, l4, in_order)
    branch(q16, k16, v16, [(s * BAND, s * nk16, True) for s in range(slots)], o16, m16, l16,
           lambda s: pl.ds((s // CLASSES) * n4 + s % CLASSES, BAND, stride=CLASSES))

    for piece in range(slots):
        r1, j = divmod(piece, blocks4)
        rows = in_order(piece)
        nat = pl.ds(r1 + CLASSES * BAND * j, BAND, stride=CLASSES)
        ma, mb, mc = m1[nat, :], m4[rows, :], m16[rows, :]
        m = jnp.maximum(jnp.maximum(ma, mb), mc)
        wa, wb, wc = jnp.exp2(ma - m), jnp.exp2(mb - m), jnp.exp2(mc - m)
        num = wa * o1[nat, :] + wb * o4[rows, :] + wc * o16[rows, :]
        den = wa * l1[nat, :] + wb * l4[rows, :] + wc * l16[rows, :]
        out_f[nat, :] = num / den
    o_ref[...] = out_f[...].astype(o_ref.dtype)


def _dilated_attn(q_hm, k_hm, v_hm):
    n_heads, rows, _ = q_hm.shape
    sb = SUPER_BLOCK
    assert rows % sb == 0 and [d for _, d in DILATIONS] == [1, CLASSES, CLASSES ** 2]
    assert all(w == BAND * d for w, d in DILATIONS) and sb == BAND * CLASSES ** 2
    block = pl.BlockSpec((None, sb, HEAD_DIM), lambda h, i: (h, i, 0))
    blk = lambda n, dt, width=HEAD_DIM: pltpu.VMEM((n, width), dt)
    lead = (BAND, CLASSES * BAND, sb)
    scratch = ([blk(sb, F32)] * 3
               + [s for n in lead for s in (blk(sb, BF16), blk(n + sb, BF16), blk(n + sb, BF16, 2 * HEAD_DIM))]
               + [blk(sb, F32)] * 10
               + [pltpu.VMEM((2, BAND, 2 * BAND), F32)])
    vmem = (2 * 3 * sb * HEAD_DIM * 4 + 2 * sb * HEAD_DIM * 2 + 2 * 2 * sb * HEAD_DIM * 4 + 13 * sb * HEAD_DIM * 4
            + (12 * sb + 3 * sum(lead)) * HEAD_DIM * 2 + 2 * BAND * 2 * BAND * 4 + (12 << 20))
    last = pl.BlockSpec((sb, HEAD_DIM), lambda h, i: (0, h))
    last_shape = jax.ShapeDtypeStruct((sb, n_heads * HEAD_DIM), F32)
    return pl.pallas_call(
        _dilated_attn_kernel,
        grid=(n_heads, rows // sb),
        in_specs=[block, block, block],
        out_specs=[pl.BlockSpec((sb, HEAD_DIM), lambda h, i: (i, h)), last, last],
        out_shape=[jax.ShapeDtypeStruct((rows, n_heads * HEAD_DIM), BF16), last_shape, last_shape],
        scratch_shapes=scratch,
        compiler_params=_params(2, vmem),
        name="dilated_attn",
    )(q_hm, k_hm, v_hm)


FAR_STEP = DILATIONS[2][1]
NEAR_ROWS = DILATIONS[1][0]
CACHE_BUFFERS = 3


def _sample_attn_kernel(q_ref, kn_ref, vn_ref, *refs, n_heads, past_len, cache, n_far, n_seq):
    far_k, far_v = refs[0:n_far], refs[n_far:2 * n_far]
    near_k_hbm, near_v_hbm, o_ref, count, near_buf, near_sem = refs[2 * n_far:]
    tq = q_ref.shape[0]
    seq = pl.program_id(0)

    def near_copy(s, which):
        src = (near_k_hbm, near_v_hbm)[which]
        slot = s % CACHE_BUFFERS
        rows = pl.ds((cache - NEAR_ROWS) * n_heads, NEAR_ROWS * n_heads)
        return pltpu.make_async_copy(src.at[s, rows, :], near_buf.at[which, slot], near_sem.at[which, slot])

    @pl.when(seq == 0)
    def _():
        for s in range(min(CACHE_BUFFERS - 1, n_seq)):
            near_copy(s, 0).start()
            near_copy(s, 1).start()

    @pl.when(seq + CACHE_BUFFERS - 1 < n_seq)
    def _():
        near_copy(seq + CACHE_BUFFERS - 1, 0).start()
        near_copy(seq + CACHE_BUFFERS - 1, 1).start()

    near_copy(seq, 0).wait()
    near_copy(seq, 1).wait()
    near_k = near_buf.at[0, seq % CACHE_BUFFERS]
    near_v = near_buf.at[1, seq % CACHE_BUFFERS]
    far_m = (cache - NEAR_ROWS) // FAR_STEP
    scale = HEAD_DIM ** -0.5
    n_rows, n_cols = count.shape
    log_tq, log_heads = int(math.log2(tq)), int(math.log2(n_heads))

    def multiplicity(t, key, ok):
        dist = cache + t - key
        seen = ok & (dist >= 0) & (past_len + t - dist >= 0)
        n = jnp.zeros(key.shape, F32)
        for window, dil in DILATIONS:
            hit = seen & (dist <= window) & ((dist & (dil - 1)) == 0)
            n = n + jnp.where(hit, 1.0, 0.0)
        return n

    @pl.when(pl.program_id(0) == 0)
    def _():
        row = lax.broadcasted_iota(jnp.int32, (n_rows, n_cols), 0)
        col = lax.broadcasted_iota(jnp.int32, (n_rows, n_cols), 1)
        pos = col >> log_heads
        cls = sum((pos >= r * far_m).astype(jnp.int32) for r in range(1, n_far))
        far_key = FAR_STEP * (pos - cls * far_m) + cls
        key = jnp.where(pos < n_far * far_m, far_key, cache - NEAR_ROWS + pos - n_far * far_m)
        count[...] = multiplicity(row & (tq - 1), key, (row >> log_tq) == (col & (n_heads - 1)))

    def all_heads(far, near):
        parts = [f.reshape(far_m * n_heads, HEAD_DIM)[...] for f in far] + [near[...]]
        return jnp.concatenate(parts, axis=0).astype(BF16)

    head = lambda a, hh: a[:, hh * HEAD_DIM:(hh + 1) * HEAD_DIM]
    rows_of = lambda a, hh: a[hh * tq:(hh + 1) * tq, :]
    q = [head(q_ref, hh).astype(BF16) for hh in range(n_heads)]
    n_old = count[...]
    s_o = jnp.where(n_old > 0, _dot_nt(jnp.concatenate(q, axis=0), all_heads(far_k, near_k)) * scale, NEG_INF)
    t_new = lax.broadcasted_iota(jnp.int32, (n_rows, tq), 0) & (tq - 1)
    n_new = multiplicity(t_new, cache + lax.broadcasted_iota(jnp.int32, (n_rows, tq), 1), True)
    s_n = jnp.concatenate([_dot_nt(q[hh], head(kn_ref, hh).astype(BF16)) for hh in range(n_heads)], axis=0) * scale
    s_n = jnp.where(n_new > 0, s_n, NEG_INF)
    m = jnp.maximum(jnp.max(s_o, axis=-1, keepdims=True), jnp.max(s_n, axis=-1, keepdims=True))
    p_o = n_old * jnp.exp(s_o - m)
    p_n = n_new * jnp.exp(s_n - m)
    den = jnp.sum(p_o, axis=-1, keepdims=True) + jnp.sum(p_n, axis=-1, keepdims=True)
    num = _dot(p_o.astype(BF16), all_heads(far_v, near_v))
    for hh in range(n_heads):
        new = _dot(rows_of(p_n, hh).astype(BF16), head(vn_ref, hh).astype(BF16))
        o_ref[:, hh * HEAD_DIM:(hh + 1) * HEAD_DIM] = ((rows_of(num, hh) + new) / rows_of(den, hh)).astype(o_ref.dtype)


def _sample_attn(q, k_new, v_new, cache_k, cache_v, *, n_queries, past_len):
    batch, tq, width = q.shape
    _, cache, n_heads, _ = cache_k.shape
    far_m = (cache - NEAR_ROWS) // FAR_STEP
    n_far = min(n_queries, FAR_STEP) if far_m else 0
    assert all(d & (d - 1) == 0 for _, d in DILATIONS) and cache % NEAR_ROWS == 0 and cache % FAR_STEP == 0
    assert DILATIONS[0][0] <= NEAR_ROWS and tq == SUBLANES and NEAR_ROWS % FAR_STEP == 0
    assert n_heads & (n_heads - 1) == 0
    new = pl.BlockSpec((None, tq, width), lambda b: (b, 0, 0))
    far_view = lambda c: c.reshape(batch, cache // FAR_STEP, FAR_STEP, n_heads, HEAD_DIM)
    near_view = lambda c: c.reshape(batch, cache * n_heads, HEAD_DIM)
    far = [pl.BlockSpec((None, far_m, None, n_heads, HEAD_DIM),
                        functools.partial(lambda b, r: (b, 0, r, 0, 0), r=r)) for r in range(n_far)]
    near = pl.BlockSpec(memory_space=pl.ANY)
    n_cols = (n_far * far_m + NEAR_ROWS) * n_heads
    vmem = ((2 * n_far * far_m + CACHE_BUFFERS * NEAR_ROWS) * 2 * n_heads * HEAD_DIM * 4
            + 8 * n_heads * tq * n_cols * 4 + 4 * n_cols * HEAD_DIM * 2 + (16 << 20))
    return pl.pallas_call(
        functools.partial(_sample_attn_kernel, n_heads=n_heads, past_len=past_len, cache=cache, n_far=n_far,
                          n_seq=batch),
        grid=(batch,),
        in_specs=[new, new, new] + far + far + [near, near],
        out_specs=new,
        out_shape=jax.ShapeDtypeStruct((batch, tq, width), BF16),
        scratch_shapes=[pltpu.VMEM((n_heads * tq, n_cols), F32),
                        pltpu.VMEM((2, CACHE_BUFFERS, NEAR_ROWS * n_heads, HEAD_DIM), F32),
                        pltpu.SemaphoreType.DMA((2, CACHE_BUFFERS))],
        compiler_params=_params(1, vmem),
        name="sample_attn",
    )(q, k_new, v_new, *([far_view(cache_k)] * n_far), *([far_view(cache_v)] * n_far),
      near_view(cache_k), near_view(cache_v))


def _memory_kv_kernel(mem_ref, norm_ref, wk_ref, wv_ref, kn_ref, k_ref, v_ref):
    m = _rms(mem_ref[...], norm_ref[...]).astype(BF16)
    n_mem = m.shape[0]
    k = _dot(m, wk_ref[...])
    v = _dot(m, wv_ref[...])
    heads = k.shape[1] // MEM_HEAD_DIM
    for hh in range(heads):
        cols = slice(hh * MEM_HEAD_DIM, (hh + 1) * MEM_HEAD_DIM)
        k_ref[pl.ds(hh, n_mem, stride=heads), :] = _rms(k[:, cols], kn_ref[...])
        v_ref[pl.ds(hh, n_mem, stride=heads), :] = v[:, cols]


def _memory_kv(mem, norm_src, w_k, w_v, k_norm):
    n_mem = mem.shape[0]
    heads = w_k.shape[1] // MEM_HEAD_DIM
    args = (mem, norm_src, w_k, w_v, k_norm)
    out = jax.ShapeDtypeStruct((n_mem * heads, MEM_HEAD_DIM), F32)
    return pl.pallas_call(
        _memory_kv_kernel,
        grid=(1,),
        in_specs=[_const_spec(a.shape) for a in args],
        out_specs=[pl.BlockSpec((n_mem * heads, MEM_HEAD_DIM), lambda i: (0, 0))] * 2,
        out_shape=[out, out],
        compiler_params=_params(1, 32 << 20),
        name="memory_kv",
    )(*args)


def _mem_mix_kernel(x_ref, pool_ref, attn_ref, wout_ref, nmem_ref, wq_ref, qn_ref, mk_ref, mv_ref, wo_ref, nffn_ref,
                    x2_ref, h3_ref, *, nb, n_mem):
    tm = x_ref.shape[0]
    pool_width = pool_ref.shape[1]
    x1 = x_ref[...] + _dot(pool_ref[...], wout_ref[0:pool_width, :]) + _dot(attn_ref[...], wout_ref[pool_width:, :])
    h2 = _rms(x1, nmem_ref[...]).astype(BF16)
    qm = _dot(h2, wq_ref[...])
    scale = MEM_HEAD_DIM ** -0.5
    n_heads = qm.shape[1] // MEM_HEAD_DIM
    n_keys = mk_ref.shape[0] // n_heads
    if nb > 1:
        seq_q = (pl.program_id(0) * tm + lax.broadcasted_iota(jnp.int32, (tm, n_keys), 0)) & (nb - 1)
        seq_k = lax.broadcasted_iota(jnp.int32, (tm, n_keys), 1) >> int(math.log2(n_mem))
        own = seq_q == seq_k
    def scores(hh):
        q = _rms(qm[:, hh * MEM_HEAD_DIM:(hh + 1) * MEM_HEAD_DIM], qn_ref[...]).astype(BF16)
        s = _dot_nt(q, mk_ref[pl.ds(hh, n_keys, stride=n_heads), :].astype(BF16)) * scale
        return jnp.where(own, s, NEG_INF) if nb > 1 else s

    heads = []
    s_next = scores(0)
    for hh in range(n_heads):
        s = s_next
        if hh + 1 < n_heads:
            s_next = scores(hh + 1)
        p = jnp.exp(s - jnp.max(s, axis=-1, keepdims=True))
        o = _dot(p.astype(BF16), mv_ref[pl.ds(hh, n_keys, stride=n_heads), :].astype(BF16))
        heads.append((o / jnp.sum(p, axis=-1, keepdims=True)).astype(BF16))
    x2 = x1 + _dot(jnp.concatenate(heads, axis=-1), wo_ref[...])
    x2_ref[...] = x2
    h3_ref[...] = _rms(x2, nffn_ref[...]).astype(h3_ref.dtype)


def _mem_mix(x, pool_out, attn_out, w_out, norm_mem, w_q, q_norm, mem_k, mem_v, w_o, norm_ffn, *, tm, nb, n_mem):
    rows, d_model = x.shape
    n_keys = mem_k.shape[0] * MEM_HEAD_DIM // w_q.shape[1]
    assert rows % tm == 0 and nb & (nb - 1) == 0 and n_mem & (n_mem - 1) == 0 and n_keys == nb * n_mem
    tile = lambda a: pl.BlockSpec((tm, a.shape[1]), lambda i: (i, 0))
    consts = (w_out, norm_mem, w_q, q_norm, mem_k, mem_v, w_o, norm_ffn)
    vmem = (sum(c.size * c.dtype.itemsize for c in consts) + 2 * tm * d_model * (4 + 2 + 4 + 2)
            + tm * d_model * 16 + 4 * tm * n_keys * 4 + (6 << 20))
    return pl.pallas_call(
        functools.partial(_mem_mix_kernel, nb=nb, n_mem=n_mem),
        grid=(rows // tm,),
        in_specs=[tile(x), tile(pool_out), tile(attn_out)] + [_const_spec(c.shape) for c in consts],
        out_specs=[pl.BlockSpec((tm, d_model), lambda i: (i, 0))] * 2,
        out_shape=[jax.ShapeDtypeStruct((rows, d_model), F32), jax.ShapeDtypeStruct((rows, d_model), BF16)],
        compiler_params=_params(1, vmem),
        name="mem_mix",
    )(x, pool_out, attn_out, *consts)


def _conv_ffn_kernel(h_ref, x_ref, wg_ref, wu_ref, cw_ref, cb_ref, wd_ref, prev_ref, y_ref, state_ref,
                     *rest, nb):
    *rounded_refs, g_ext, g_scr = rest
    i = pl.program_id(0)
    j = pl.program_id(1)
    tm = h_ref.shape[0]
    tail = CONV_TAIL * nb
    if rounded_refs:
        for src, dst in zip((wg_ref, wu_ref, wd_ref), rounded_refs):
            dst[...] = src[...].astype(BF16)
        wg_ref, wu_ref, wd_ref = rounded_refs

    @pl.when(j == 0)
    def _():
        y_ref[...] = x_ref[...]

    @pl.when(i == 0)
    def _():
        g_ext[0:tail, :] = prev_ref[...]

    @pl.when(i > 0)
    def _():
        g_ext[0:tail, :] = g_scr[j]

    g_ext[tail:tail + tm, :] = _dot(h_ref[...], wg_ref[...])
    up = _dot(h_ref[...], wu_ref[...])
    c = cb_ref[...]
    for tap in range(CONV_WIDTH):
        c = c + g_ext[tap * nb:tap * nb + tm, :] * cw_ref[tap:tap + 1, :]
    act = (c * jax.nn.sigmoid(c) * up).astype(BF16)
    y_ref[...] += _dot(act, wd_ref[...])

    last = g_ext[tm:tm + tail, :]
    g_scr[j] = last
    state_ref[:, pl.ds(pl.multiple_of(j * last.shape[1], LANES), last.shape[1])] = last


def _conv_ffn(h3, x2, w_gate, w_up, conv_w, conv_b, w_down, conv_prev, *, tm, tf, nb):
    rows, d_model = x2.shape
    d_ff = w_gate.shape[1]
    weights = (w_gate, w_up, w_down)
    rounds = w_gate.dtype != BF16
    assert rows % tm == 0 and d_ff % tf == 0 and tm >= CONV_TAIL * nb and (rows == tm or not rounds)
    assert all(w.dtype == w_gate.dtype for w in weights)
    tail = CONV_TAIL * nb
    row_tile = pl.BlockSpec((tm, d_model), lambda i, j: (i, 0))
    chunk = lambda n: pl.BlockSpec((n, tf), lambda i, j: (0, j))
    down_chunk = pl.BlockSpec((tf, d_model), lambda i, j: (j, 0))
    w_bytes = w_gate.dtype.itemsize + (2 if rounds else 0)
    vmem = (2 * tm * d_model * (2 + 4 + 4) + 2 * 3 * d_model * tf * w_bytes + (tm + 3 * tail) * tf * 4
            + d_ff // tf * tail * tf * 4 + 5 * tm * tf * 4 + tm * d_model * 4 + (6 << 20))
    y, state, *rounded = pl.pallas_call(
        functools.partial(_conv_ffn_kernel, nb=nb),
        grid=(rows // tm, d_ff // tf),
        in_specs=[row_tile, row_tile, chunk(d_model), chunk(d_model), chunk(CONV_WIDTH), chunk(1),
                  down_chunk, chunk(tail)],
        out_specs=([row_tile, pl.BlockSpec((tail, d_ff), lambda i, j: (0, 0))]
                   + ([chunk(d_model), chunk(d_model), down_chunk] if rounds else [])),
        out_shape=([jax.ShapeDtypeStruct((rows, d_model), F32), jax.ShapeDtypeStruct((tail, d_ff), F32)]
                   + ([jax.ShapeDtypeStruct(w.shape, BF16) for w in weights] if rounds else [])),
        scratch_shapes=[pltpu.VMEM((tail + tm, tf), F32), pltpu.VMEM((d_ff // tf, tail, tf), F32)],
        compiler_params=_params(2, vmem),
        name="conv_ffn",
    )(h3, x2, w_gate, w_up, conv_w, conv_b, w_down, conv_prev)
    return y, state, tuple(rounded) if rounds else weights


ROPE_SPLIT = 128


def _rope_tables(pos0, n_pos):
    half = HEAD_DIM // 2
    inv = 1.0 / (ROPE_THETA ** (jnp.arange(half, dtype=F32) * (2.0 / HEAD_DIM)))
    angle = lambda pos: pos.astype(F32)[:, None] * inv[None, :]
    if pos0 % ROPE_SPLIT == 0 and n_pos % ROPE_SPLIT == 0:
        a = angle(pos0 + ROPE_SPLIT * jnp.arange(n_pos // ROPE_SPLIT))[:, None, :]
        b = angle(jnp.arange(ROPE_SPLIT))[None, :, :]
        cos = (jnp.cos(a) * jnp.cos(b) - jnp.sin(a) * jnp.sin(b)).reshape(n_pos, half)
        sin = (jnp.sin(a) * jnp.cos(b) + jnp.cos(a) * jnp.sin(b)).reshape(n_pos, half)
    else:
        ang = angle(pos0 + jnp.arange(n_pos))
        cos, sin = jnp.cos(ang), jnp.sin(ang)
    return cos, sin


def _row_tile(rows, want):
    return want if rows % want == 0 else rows


def _layer(x, pos0, nb, pool_prev, conv_prev, mem_k, mem_v, n_mem, attend, p):
    rows = x.shape[0]
    cos, sin = _rope_tables(pos0, rows // nb)
    if nb > 1:
        cos, sin = jnp.repeat(cos, nb, axis=0), jnp.repeat(sin, nb, axis=0)
    pool_out, q_hm, k_hm, v_hm, pool_state = _in_proj(
        x, cos, sin, pool_prev, p["norm_mix"], p["w_in"], p["q_norm"], p["k_norm"], p["w_pool"], p["pool_scale"],
        tm=_row_tile(rows, IN_PROJ_ROWS), nb=nb, pos0=pos0)
    attn_out, k_keep, v_keep = attend(q_hm, k_hm, v_hm)
    x2, h3 = _mem_mix(x, pool_out, attn_out, p["w_out"], p["norm_mem"], p["w_mem_q"], p["mem_q_norm"], mem_k, mem_v,
                      p["w_mem_o"], p["norm_ffn"], tm=_row_tile(rows, MEM_MIX_ROWS), nb=nb, n_mem=n_mem)
    y, conv_state, ffn_weights = _conv_ffn(h3, x2, p["w_gate"], p["w_up"], p["conv_w"], p["conv_b"], p["w_down"],
                                           conv_prev, tm=_row_tile(rows, FFN_ROWS), tf=FFN_COLS, nb=nb)
    return y, pool_state, k_keep, v_keep, conv_state, ffn_weights


def kernel(x_prompt, x_sample, state_pool, cache_win_k, cache_win_v, cache_mem_k, cache_mem_v, state_conv, mem_prompt, norm_mix, w_in, q_norm, k_norm, w_pool, pool_scale, w_out, norm_mem, norm_mem_src, w_mem_q, w_mem_k, w_mem_v, mem_q_norm, mem_k_norm, w_mem_o, norm_ffn, w_gate, w_up, conv_w, conv_b, w_down):
    bp, tp, d_model = x_prompt.shape
    bs, ts, _ = x_sample.shape
    depth = w_in.shape[0]
    assert bp == 1
    pool_width = pool_scale.shape[-1]
    d_ff = w_gate.shape[-1]
    n_mem = mem_prompt.shape[1]
    n_heads = cache_win_k.shape[3]
    mem_heads = cache_mem_k.shape[3]
    attn_width = n_heads * HEAD_DIM
    keep_p = min(MAX_WINDOW, tp)
    assert keep_p == SUPER_BLOCK
    row = lambda a: a.reshape(1, -1)

    xp = x_prompt.reshape(tp, d_model)
    xs = x_sample.transpose(1, 0, 2).reshape(ts * bs, d_model)
    outs = {k: [] for k in ("p_pool", "p_k", "p_v", "p_mk", "p_mv", "p_conv", "s_pool", "s_k", "s_v", "s_conv")}
    for l in range(depth):
        prm = {
            "norm_mix": row(norm_mix[l]), "w_in": w_in[l].astype(BF16), "q_norm": row(q_norm[l]),
            "k_norm": row(k_norm[l]), "w_pool": w_pool[l].astype(BF16), "pool_scale": row(pool_scale[l]),
            "w_out": w_out[l].astype(BF16), "norm_mem": row(norm_mem[l]), "w_mem_q": w_mem_q[l].astype(BF16),
            "mem_q_norm": row(mem_q_norm[l]), "w_mem_o": w_mem_o[l].astype(BF16), "norm_ffn": row(norm_ffn[l]),
            "w_gate": w_gate[l], "w_up": w_up[l], "conv_w": conv_w[l], "conv_b": row(conv_b[l]), "w_down": w_down[l],
        }
        pool_prev = jnp.pad(state_pool[l].transpose(1, 0, 2), ((POOL_TAIL - POOL_STATE, 0), (0, 0), (0, 0)))
        conv_prev = state_conv[l].transpose(1, 0, 2).reshape(CONV_TAIL * bs, d_ff)

        def attend_samples(q_hm, k_hm, v_hm, l=l):
            pad = lambda a: jnp.pad(a, ((0, 0), (0, -ts % SUBLANES), (0, 0)))
            to_seq = lambda a: a.reshape(n_heads, ts, bs, HEAD_DIM).transpose(2, 1, 0, 3).reshape(bs, ts, attn_width)
            k_new, v_new = to_seq(k_hm), to_seq(v_hm)
            o = _sample_attn(pad(to_seq(q_hm)), pad(k_new), pad(v_new), cache_win_k[l], cache_win_v[l],
                             n_queries=ts, past_len=PAST_LEN)
            return o[:, :ts].transpose(1, 0, 2).reshape(ts * bs, attn_width), k_new, v_new

        xs, pool_st, ks, vs, conv_st, ffn_weights = _layer(
            xs, PAST_LEN, bs, pool_prev.reshape(POOL_TAIL * bs, pool_width), conv_prev,
            cache_mem_k[l].reshape(bs * n_mem * mem_heads, MEM_HEAD_DIM),
            cache_mem_v[l].reshape(bs * n_mem * mem_heads, MEM_HEAD_DIM), n_mem, attend_samples, prm)
        seq_major = lambda a, t: a.reshape(t, bs, -1).transpose(1, 0, 2)
        outs["s_pool"].append(seq_major(pool_st, POOL_STATE))
        outs["s_k"].append(ks.reshape(bs, ts, n_heads, HEAD_DIM))
        outs["s_v"].append(vs.reshape(bs, ts, n_heads, HEAD_DIM))
        outs["s_conv"].append(seq_major(conv_st, CONV_TAIL))

        mk, mv = _memory_kv(mem_prompt[0], row(norm_mem_src[l]), w_mem_k[l].astype(BF16), w_mem_v[l].astype(BF16),
                            row(mem_k_norm[l]))
        xp, pool_st, kp, vp, conv_st, _ = _layer(
            xp, 0, 1, jnp.zeros((POOL_TAIL, pool_width), F32), jnp.zeros((CONV_TAIL, d_ff), F32), mk, mv, n_mem,
            _dilated_attn, dict(prm, **dict(zip(("w_gate", "w_up", "w_down"), ffn_weights))))
        outs["p_pool"].append(pool_st[None])
        outs["p_k"].append(kp.reshape(1, keep_p, n_heads, HEAD_DIM))
        outs["p_v"].append(vp.reshape(1, keep_p, n_heads, HEAD_DIM))
        outs["p_mk"].append(mk.reshape(1, n_mem, mem_heads, MEM_HEAD_DIM))
        outs["p_mv"].append(mv.reshape(1, n_mem, mem_heads, MEM_HEAD_DIM))
        outs["p_conv"].append(conv_st[None])
    stack = lambda k: jnp.stack(outs[k])
    return (xp.reshape(bp, tp, d_model), xs.reshape(ts, bs, d_model).transpose(1, 0, 2),
            stack("p_pool"), stack("p_k"), stack("p_v"), stack("p_mk"), stack("p_mv"), stack("p_conv"),
            stack("s_pool"), stack("s_k"), stack("s_v"), stack("s_conv"))
```
